```python
import math
import jax, jax.numpy as jnp
from jax import lax
import numpy as np

D_MODEL = 2048
BATCH = 8
SEQ = 4096
DEPTH = 4

CTX_LEN = 256
GRID_W = 64
MIX_WIDTH = D_MODEL
POOL_WIDTH = 3 * MIX_WIDTH // 4
SSM_WIDTH = MIX_WIDTH - POOL_WIDTH
POOL_WINDOWS = (2, 4, 8, 16)
N_POOL_GROUPS = len(POOL_WINDOWS)
POOL_GROUP = POOL_WIDTH // N_POOL_GROUPS
SSM_GROUP = 16
N_SSM_GROUPS = SSM_WIDTH // SSM_GROUP
SSM_STATE = 64
D_FF = 5632
CONV_K = 3
DT_MIN, DT_MAX = 1e-3, 1e-1
EPS = 1e-6

kernel_name = "hybrid_pool_s5_prefix_dit_block"


def rms_norm(x, gain):
    xf = x.astype(jnp.float32)
    y = xf * lax.rsqrt(jnp.mean(xf * xf, axis=-1, keepdims=True) + EPS)
    return (y * gain.astype(jnp.float32)).astype(x.dtype)


def modulate(h, shift, scale):
    return h * (1 + scale[:, None, :]) + shift[:, None, :]


def multiscale_pool(u, w_pool, pool_scale):
    bsz, n, _ = u.shape
    uf = u.astype(jnp.float32)
    cs = jnp.concatenate([jnp.zeros((bsz, 1, POOL_WIDTH), jnp.float32), jnp.cumsum(uf, axis=1)], axis=1)
    t = jnp.arange(n)
    parts = []
    for g, w in enumerate(POOL_WINDOWS):
        lo = jnp.maximum(t - w // 2, 0)
        hi = jnp.minimum(t + w // 2, n)
        sl = slice(g * POOL_GROUP, (g + 1) * POOL_GROUP)
        csg = cs[..., sl]
        cnt = (hi - lo).astype(jnp.float32)[None, :, None]
        mean = (jnp.take(csg, hi, axis=1) - jnp.take(csg, lo, axis=1)) / cnt
        parts.append(mean - uf[..., sl])
    p = jnp.stack(parts, axis=2)
    y = jnp.einsum('blgc,gcd->blgd', p, w_pool.astype(jnp.float32))
    return (y.reshape(bsz, n, POOL_WIDTH) * pool_scale.astype(jnp.float32)).astype(u.dtype)


def s5_discretise(a_re, a_im, log_dt, b_re, b_im):
    a_re = a_re.astype(jnp.float32)
    a_im = a_im.astype(jnp.float32)
    dt = jnp.exp(log_dt.astype(jnp.float32))[:, None]
    mag = jnp.exp(a_re * dt)
    lam_re = mag * jnp.cos(a_im * dt)
    lam_im = mag * jnp.sin(a_im * dt)
    denom = a_re * a_re + a_im * a_im
    nr, ni = lam_re - 1.0, lam_im
    f_re = (nr * a_re + ni * a_im) / denom
    f_im = (ni * a_re - nr * a_im) / denom
    b_re = b_re.astype(jnp.float32)
    b_im = b_im.astype(jnp.float32)
    bb_re = f_re[..., None] * b_re - f_im[..., None] * b_im
    bb_im = f_re[..., None] * b_im + f_im[..., None] * b_re
    return lam_re, lam_im, bb_re, bb_im


def _complex_linear_recurrence_op(left, right):
    a1r, a1i, b1r, b1i = left
    a2r, a2i, b2r, b2i = right
    return (a2r * a1r - a2i * a1i,
            a2r * a1i + a2i * a1r,
            a2r * b1r - a2i * b1i + b2r,
            a2r * b1i + a2i * b1r + b2i)


def s5_scan(u_g, lam_re, lam_im, bb_re, bb_im, h0, reverse):
    b_re = jnp.einsum('blgh,gph->blgp', u_g, bb_re)
    b_im = jnp.einsum('blgh,gph->blgp', u_g, bb_im)
    if h0 is not None:
        pos = -1 if reverse else 0
        h0_re, h0_im = h0
        b_re = b_re.at[:, pos].add(lam_re * h0_re - lam_im * h0_im)
        b_im = b_im.at[:, pos].add(lam_re * h0_im + lam_im * h0_re)
    a_re = jnp.broadcast_to(lam_re, b_re.shape)
    a_im = jnp.broadcast_to(lam_im, b_im.shape)
    _, _, h_re, h_im = lax.associative_scan(
        _complex_linear_recurrence_op, (a_re, a_im, b_re, b_im), axis=1, reverse=reverse)
    return h_re, h_im


def s5_readout(h, c_re, c_im):
    h_re, h_im = h
    return (jnp.einsum('blgp,ghp->blgh', h_re, c_re.astype(jnp.float32))
            - jnp.einsum('blgp,ghp->blgh', h_im, c_im.astype(jnp.float32)))


def to_ssm_groups(u_ssm):
    bsz, n, _ = u_ssm.shape
    return u_ssm.astype(jnp.float32).reshape(bsz, n, N_SSM_GROUPS, SSM_GROUP)


def s5_head_output(u_ssm, y, ssm_d, w_glu):
    bsz, n, _ = u_ssm.shape
    yf = y.reshape(bsz, n, SSM_WIDTH) + ssm_d.astype(jnp.float32) * u_ssm.astype(jnp.float32)
    yf = jax.nn.gelu(yf)
    return (yf * jax.nn.sigmoid(yf @ w_glu.astype(jnp.float32))).astype(u_ssm.dtype)


def mix_project(u, ssm_y, w_pool, pool_scale, ssm_d, w_glu, w_out):
    pool_out = multiscale_pool(u[..., :POOL_WIDTH], w_pool, pool_scale)
    ssm_out = s5_head_output(u[..., POOL_WIDTH:], ssm_y, ssm_d, w_glu)
    return jnp.concatenate([pool_out, ssm_out], axis=-1) @ w_out


def conv_glu_ffn(h, w_up, w_conv, w_down, rows):
    bsz, n, _ = h.shape
    z = h @ w_up
    if rows is None:
        grid = z[:, None]
        k = w_conv[1:2]
    else:
        grid = z.reshape(bsz, rows, GRID_W, 2 * D_FF)
        k = w_conv
    grid = lax.conv_general_dilated(grid, k[:, :, None, :], (1, 1), 'SAME',
                                    dimension_numbers=('NHWC', 'HWIO', 'NHWC'),
                                    feature_group_count=2 * D_FF)
    val, gate = jnp.split(grid.reshape(bsz, n, 2 * D_FF), 2, axis=-1)
    return (val * jax.nn.silu(gate)) @ w_down


def _fwd_setup_inputs(seed: int = 0) -> dict:
    key = jax.random.key(seed)
    ks = jax.random.split(key, 26)
    f32 = jnp.float32
    nrm = lambda k, shape, s: jax.random.normal(k, shape, f32) * s
    G, P, H = N_SSM_GROUPS, SSM_STATE, SSM_GROUP
    a_im_base = jnp.pi * jnp.arange(P, dtype=f32)
    return {
        "x": nrm(ks[0], (BATCH, SEQ, D_MODEL), 1.0),
        "c": nrm(ks[1], (BATCH, D_MODEL), 1.0),
        "ctx": nrm(ks[2], (BATCH, CTX_LEN, D_MODEL), 1.0),
        "c_ctx": nrm(ks[3], (D_MODEL,), 1.0),
        "w_ada": nrm(ks[4], (DEPTH, D_MODEL, 6 * D_MODEL), 0.5 * D_MODEL ** -0.5),
        "b_ada": nrm(ks[5], (DEPTH, 6 * D_MODEL), 0.02),
        "w_in": nrm(ks[6], (DEPTH, D_MODEL, MIX_WIDTH), D_MODEL ** -0.5),
        "w_pool": nrm(ks[7], (DEPTH, N_POOL_GROUPS, POOL_GROUP, POOL_GROUP), POOL_GROUP ** -0.5),
        "pool_scale": 1.0 + nrm(ks[8], (DEPTH, POOL_WIDTH), 0.02),
        "ssm_a_re": -0.5 + nrm(ks[9], (DEPTH, 2, G, P), 0.01),
        "ssm_a_im": a_im_base + nrm(ks[10], (DEPTH, 2, G, P), 0.01),
        "ssm_log_dt": jax.random.uniform(ks[11], (DEPTH, 2, G), f32, math.log(DT_MIN), math.log(DT_MAX)),
        "ssm_b_re": nrm(ks[12], (DEPTH, 2, G, P, H), (2 * H) ** -0.5),
        "ssm_b_im": nrm(ks[13], (DEPTH, 2, G, P, H), (2 * H) ** -0.5),
        "ssm_c_re": nrm(ks[14], (DEPTH, 2, G, H, P), P ** -0.5),
        "ssm_c_im": nrm(ks[15], (DEPTH, 2, G, H, P), P ** -0.5),
        "ssm_d": nrm(ks[16], (DEPTH, SSM_WIDTH), 1.0),
        "w_glu": nrm(ks[17], (DEPTH, SSM_WIDTH, SSM_WIDTH), SSM_WIDTH ** -0.5),
        "w_out": nrm(ks[18], (DEPTH, MIX_WIDTH, D_MODEL), MIX_WIDTH ** -0.5),
        "g_pre_mix": 1.0 + nrm(ks[19], (DEPTH, D_MODEL), 0.02),
        "g_post_mix": 1.0 + nrm(ks[20], (DEPTH, D_MODEL), 0.02),
        "g_pre_ffn": 1.0 + nrm(ks[21], (DEPTH, D_MODEL), 0.02),
        "g_post_ffn": 1.0 + nrm(ks[22], (DEPTH, D_MODEL), 0.02),
        "w_up": nrm(ks[23], (DEPTH, D_MODEL, 2 * D_FF), D_MODEL ** -0.5),
        "w_conv": nrm(ks[24], (DEPTH, CONV_K, CONV_K, 2 * D_FF), 1.0 / CONV_K),
        "w_down": nrm(ks[25], (DEPTH, D_FF, D_MODEL), D_FF ** -0.5),
    }


def _fwd_reference(x, c, ctx, c_ctx, w_ada, b_ada, w_in, w_pool, pool_scale, ssm_a_re, ssm_a_im,
              ssm_log_dt, ssm_b_re, ssm_b_im, ssm_c_re, ssm_c_im, ssm_d, w_glu, w_out,
              g_pre_mix, g_post_mix, g_pre_ffn, g_post_ffn, w_up, w_conv, w_down):
    n_lat = x.shape[1]
    rows = n_lat // GRID_W
    s_c = jax.nn.silu(c)
    s_ctx = jax.nn.silu(c_ctx)[None, :]
    for l in range(DEPTH):
        last = l == DEPTH - 1
        mx = jnp.split(s_c @ w_ada[l] + b_ada[l], 6, axis=-1)
        mc = jnp.split(s_ctx @ w_ada[l] + b_ada[l], 6, axis=-1)
        disc = [s5_discretise(ssm_a_re[l, d], ssm_a_im[l, d], ssm_log_dt[l, d],
                              ssm_b_re[l, d], ssm_b_im[l, d]) for d in range(2)]

        h_ctx = modulate(rms_norm(ctx, g_pre_mix[l]), mc[0], mc[1])
        h_lat = modulate(rms_norm(x, g_pre_mix[l]), mx[0], mx[1])
        u_lat = h_lat @ w_in[l]
        if last:
            u_ctx_ssm = h_ctx @ w_in[l][:, POOL_WIDTH:]
        else:
            u_ctx = h_ctx @ w_in[l]
            u_ctx_ssm = u_ctx[..., POOL_WIDTH:]
        ctx_g = to_ssm_groups(u_ctx_ssm)
        lat_g = to_ssm_groups(u_lat[..., POOL_WIDTH:])
        lat_dirs, ctx_dirs = [], []
        for d in range(2):
            rev = d == 1
            lam_re, lam_im, bb_re, bb_im = disc[d]
            hc = s5_scan(ctx_g, lam_re, lam_im, bb_re, bb_im, None, rev)
            fin = 0 if rev else -1
            h0 = (hc[0][:, fin], hc[1][:, fin])
            hl = s5_scan(lat_g, lam_re, lam_im, bb_re, bb_im, h0, rev)
            lat_dirs.append(s5_readout(hl, ssm_c_re[l, d], ssm_c_im[l, d]))
            if not last:
                ctx_dirs.append(s5_readout(hc, ssm_c_re[l, d], ssm_c_im[l, d]))
        mix_lat = mix_project(u_lat, lat_dirs[0] + lat_dirs[1], w_pool[l], pool_scale[l],
                              ssm_d[l], w_glu[l], w_out[l])
        x = x + mx[2][:, None, :] * rms_norm(mix_lat, g_post_mix[l])

        f_lat = conv_glu_ffn(modulate(rms_norm(x, g_pre_ffn[l]), mx[3], mx[4]),
                             w_up[l], w_conv[l], w_down[l], rows)
        x = x + mx[5][:, None, :] * rms_norm(f_lat, g_post_ffn[l])

        if not last:
            mix_ctx = mix_project(u_ctx, ctx_dirs[0] + ctx_dirs[1], w_pool[l], pool_scale[l],
                                  ssm_d[l], w_glu[l], w_out[l])
            ctx = ctx + mc[2][:, None, :] * rms_norm(mix_ctx, g_post_mix[l])
            f_ctx = conv_glu_ffn(modulate(rms_norm(ctx, g_pre_ffn[l]), mc[3], mc[4]),
                                 w_up[l], w_conv[l], w_down[l], None)
            ctx = ctx + mc[5][:, None, :] * rms_norm(f_ctx, g_post_ffn[l])
    return x


import jax as _jax
import jax.numpy as _jnp

TWIN_FORMAT = 'train_step'
FWD_PARAMS = ['x', 'c', 'ctx', 'c_ctx', 'w_ada', 'b_ada', 'w_in', 'w_pool', 'pool_scale', 'ssm_a_re', 'ssm_a_im', 'ssm_log_dt', 'ssm_b_re', 'ssm_b_im', 'ssm_c_re', 'ssm_c_im', 'ssm_d', 'w_glu', 'w_out', 'g_pre_mix', 'g_post_mix', 'g_pre_ffn', 'g_post_ffn', 'w_up', 'w_conv', 'w_down']
TWIN_WEIGHTS = ['c_ctx', 'w_ada', 'b_ada', 'w_in', 'w_pool', 'pool_scale', 'ssm_a_re', 'ssm_a_im', 'ssm_log_dt', 'ssm_b_re', 'ssm_b_im', 'ssm_c_re', 'ssm_c_im', 'ssm_d', 'w_glu', 'w_out', 'g_pre_mix', 'g_post_mix', 'g_pre_ffn', 'g_post_ffn', 'w_up', 'w_conv', 'w_down']
TWIN_DIFF_INPUT = 'x'
TWIN_INPUTS = ['x', 'c', 'ctx', 'c_ctx', 'w_ada', 'b_ada', 'w_in', 'w_pool', 'pool_scale', 'ssm_a_re', 'ssm_a_im', 'ssm_log_dt', 'ssm_b_re', 'ssm_b_im', 'ssm_c_re', 'ssm_c_im', 'ssm_d', 'w_glu', 'w_out', 'g_pre_mix', 'g_post_mix', 'g_pre_ffn', 'g_post_ffn', 'w_up', 'w_conv', 'w_down', 'loss_target', 'm_c_ctx', 'm_w_ada', 'm_b_ada', 'm_w_in', 'm_w_pool', 'm_pool_scale', 'm_ssm_a_re', 'm_ssm_a_im', 'm_ssm_log_dt', 'm_ssm_b_re', 'm_ssm_b_im', 'm_ssm_c_re', 'm_ssm_c_im', 'm_ssm_d', 'm_w_glu', 'm_w_out', 'm_g_pre_mix', 'm_g_post_mix', 'm_g_pre_ffn', 'm_g_post_ffn', 'm_w_up', 'm_w_conv', 'm_w_down', 'v_c_ctx', 'v_w_ada', 'v_b_ada', 'v_w_in', 'v_w_pool', 'v_pool_scale', 'v_ssm_a_re', 'v_ssm_a_im', 'v_ssm_log_dt', 'v_ssm_b_re', 'v_ssm_b_im', 'v_ssm_c_re', 'v_ssm_c_im', 'v_ssm_d', 'v_w_glu', 'v_w_out', 'v_g_pre_mix', 'v_g_post_mix', 'v_g_pre_ffn', 'v_g_post_ffn', 'v_w_up', 'v_w_conv', 'v_w_down']
TWIN_OUTPUTS = ['loss', 'grad_x', 'grad_c_ctx', 'grad_w_ada', 'grad_b_ada', 'grad_w_in', 'grad_w_pool', 'grad_pool_scale', 'grad_ssm_a_re', 'grad_ssm_a_im', 'grad_ssm_log_dt', 'grad_ssm_b_re', 'grad_ssm_b_im', 'grad_ssm_c_re', 'grad_ssm_c_im', 'grad_ssm_d', 'grad_w_glu', 'grad_w_out', 'grad_g_pre_mix', 'grad_g_post_mix', 'grad_g_pre_ffn', 'grad_g_post_ffn', 'grad_w_up', 'grad_w_conv', 'grad_w_down', 'delta_c_ctx', 'delta_w_ada', 'delta_b_ada', 'delta_w_in', 'delta_w_pool', 'delta_pool_scale', 'delta_ssm_a_re', 'delta_ssm_a_im', 'delta_ssm_log_dt', 'delta_ssm_b_re', 'delta_ssm_b_im', 'delta_ssm_c_re', 'delta_ssm_c_im', 'delta_ssm_d', 'delta_w_glu', 'delta_w_out', 'delta_g_pre_mix', 'delta_g_post_mix', 'delta_g_pre_ffn', 'delta_g_post_ffn', 'delta_w_up', 'delta_w_conv', 'delta_w_down', 'new_m_c_ctx', 'new_m_w_ada', 'new_m_b_ada', 'new_m_w_in', 'new_m_w_pool', 'new_m_pool_scale', 'new_m_ssm_a_re', 'new_m_ssm_a_im', 'new_m_ssm_log_dt', 'new_m_ssm_b_re', 'new_m_ssm_b_im', 'new_m_ssm_c_re', 'new_m_ssm_c_im', 'new_m_ssm_d', 'new_m_w_glu', 'new_m_w_out', 'new_m_g_pre_mix', 'new_m_g_post_mix', 'new_m_g_pre_ffn', 'new_m_g_post_ffn', 'new_m_w_up', 'new_m_w_conv', 'new_m_w_down', 'new_v_c_ctx', 'new_v_w_ada', 'new_v_b_ada', 'new_v_w_in', 'new_v_w_pool', 'new_v_pool_scale', 'new_v_ssm_a_re', 'new_v_ssm_a_im', 'new_v_ssm_log_dt', 'new_v_ssm_b_re', 'new_v_ssm_b_im', 'new_v_ssm_c_re', 'new_v_ssm_c_im', 'new_v_ssm_d', 'new_v_w_glu', 'new_v_w_out', 'new_v_g_pre_mix', 'new_v_g_post_mix', 'new_v_g_pre_ffn', 'new_v_g_post_ffn', 'new_v_w_up', 'new_v_w_conv', 'new_v_w_down']
TWIN_LEAF_KINDS = {'loss': 'loss', 'grad_x': 'grad_x', 'grad_c_ctx': 'grad_w', 'grad_w_ada': 'grad_w', 'grad_b_ada': 'grad_w', 'grad_w_in': 'grad_w', 'grad_w_pool': 'grad_w', 'grad_pool_scale': 'grad_w', 'grad_ssm_a_re': 'grad_w', 'grad_ssm_a_im': 'grad_w', 'grad_ssm_log_dt': 'grad_w', 'grad_ssm_b_re': 'grad_w', 'grad_ssm_b_im': 'grad_w', 'grad_ssm_c_re': 'grad_w', 'grad_ssm_c_im': 'grad_w', 'grad_ssm_d': 'grad_w', 'grad_w_glu': 'grad_w', 'grad_w_out': 'grad_w', 'grad_g_pre_mix': 'grad_w', 'grad_g_post_mix': 'grad_w', 'grad_g_pre_ffn': 'grad_w', 'grad_g_post_ffn': 'grad_w', 'grad_w_up': 'grad_w', 'grad_w_conv': 'grad_w', 'grad_w_down': 'grad_w', 'delta_c_ctx': 'delta_w', 'delta_w_ada': 'delta_w', 'delta_b_ada': 'delta_w', 'delta_w_in': 'delta_w', 'delta_w_pool': 'delta_w', 'delta_pool_scale': 'delta_w', 'delta_ssm_a_re': 'delta_w', 'delta_ssm_a_im': 'delta_w', 'delta_ssm_log_dt': 'delta_w', 'delta_ssm_b_re': 'delta_w', 'delta_ssm_b_im': 'delta_w', 'delta_ssm_c_re': 'delta_w', 'delta_ssm_c_im': 'delta_w', 'delta_ssm_d': 'delta_w', 'delta_w_glu': 'delta_w', 'delta_w_out': 'delta_w', 'delta_g_pre_mix': 'delta_w', 'delta_g_post_mix': 'delta_w', 'delta_g_pre_ffn': 'delta_w', 'delta_g_post_ffn': 'delta_w', 'delta_w_up': 'delta_w', 'delta_w_conv': 'delta_w', 'delta_w_down': 'delta_w', 'new_m_c_ctx': 'new_m', 'new_m_w_ada': 'new_m', 'new_m_b_ada': 'new_m', 'new_m_w_in': 'new_m', 'new_m_w_pool': 'new_m', 'new_m_pool_scale': 'new_m', 'new_m_ssm_a_re': 'new_m', 'new_m_ssm_a_im': 'new_m', 'new_m_ssm_log_dt': 'new_m', 'new_m_ssm_b_re': 'new_m', 'new_m_ssm_b_im': 'new_m', 'new_m_ssm_c_re': 'new_m', 'new_m_ssm_c_im': 'new_m', 'new_m_ssm_d': 'new_m', 'new_m_w_glu': 'new_m', 'new_m_w_out': 'new_m', 'new_m_g_pre_mix': 'new_m', 'new_m_g_post_mix': 'new_m', 'new_m_g_pre_ffn': 'new_m', 'new_m_g_post_ffn': 'new_m', 'new_m_w_up': 'new_m', 'new_m_w_conv': 'new_m', 'new_m_w_down': 'new_m', 'new_v_c_ctx': 'new_v', 'new_v_w_ada': 'new_v', 'new_v_b_ada': 'new_v', 'new_v_w_in': 'new_v', 'new_v_w_pool': 'new_v', 'new_v_pool_scale': 'new_v', 'new_v_ssm_a_re': 'new_v', 'new_v_ssm_a_im': 'new_v', 'new_v_ssm_log_dt': 'new_v', 'new_v_ssm_b_re': 'new_v', 'new_v_ssm_b_im': 'new_v', 'new_v_ssm_c_re': 'new_v', 'new_v_ssm_c_im': 'new_v', 'new_v_ssm_d': 'new_v', 'new_v_w_glu': 'new_v', 'new_v_w_out': 'new_v', 'new_v_g_pre_mix': 'new_v', 'new_v_g_post_mix': 'new_v', 'new_v_g_pre_ffn': 'new_v', 'new_v_g_post_ffn': 'new_v', 'new_v_w_up': 'new_v', 'new_v_w_conv': 'new_v', 'new_v_w_down': 'new_v'}


def _forward(args):
    return _fwd_reference(*[args[k] for k in FWD_PARAMS])


def _output_shape():
    out = _jax.eval_shape(lambda: _forward(_fwd_setup_inputs(0)))
    return out.shape, out.dtype

N_MICROBATCH = 1
ADAM_LR = 0.001
ADAM_B1 = 0.9
ADAM_B2 = 0.999
ADAM_EPS = 1e-08
ADAM_WD = 0.01
ADAM_STEP = 10
PER_EXAMPLE_BATCH_AXIS = {'x': 0, 'c': 0, 'ctx': 0, 'loss_target': 0}
SHARED_INPUTS = []
_WEIGHT_DTYPES = {'c_ctx': _jnp.float32, 'w_ada': _jnp.float32, 'b_ada': _jnp.float32, 'w_in': _jnp.float32, 'w_pool': _jnp.float32, 'pool_scale': _jnp.float32, 'ssm_a_re': _jnp.float32, 'ssm_a_im': _jnp.float32, 'ssm_log_dt': _jnp.float32, 'ssm_b_re': _jnp.float32, 'ssm_b_im': _jnp.float32, 'ssm_c_re': _jnp.float32, 'ssm_c_im': _jnp.float32, 'ssm_d': _jnp.float32, 'w_glu': _jnp.float32, 'w_out': _jnp.float32, 'g_pre_mix': _jnp.float32, 'g_post_mix': _jnp.float32, 'g_pre_ffn': _jnp.float32, 'g_post_ffn': _jnp.float32, 'w_up': _jnp.float32, 'w_conv': _jnp.float32, 'w_down': _jnp.float32}
MOMENT_SCALE = {'c_ctx': 1.233530e-03, 'w_ada': 8.081563e-01, 'b_ada': 1.542490e+00, 'w_in': 7.130944e-02, 'w_pool': 8.174905e-02, 'pool_scale': 8.493958e-02, 'ssm_a_re': 3.962053e-03, 'ssm_a_im': 4.497173e-03, 'ssm_log_dt': 1.238400e+00, 'ssm_b_re': 2.553497e-03, 'ssm_b_im': 2.363099e-03, 'ssm_c_re': 3.339852e-03, 'ssm_c_im': 3.423574e-03, 'ssm_d': 5.571015e-02, 'w_glu': 1.107263e-02, 'w_out': 7.686504e-02, 'g_pre_mix': 7.057715e-02, 'g_post_mix': 1.812722e+00, 'g_pre_ffn': 7.171972e-02, 'g_post_ffn': 1.785847e+00, 'w_up': 3.317856e-02, 'w_conv': 3.359702e-02, 'w_down': 5.663369e-02}


def _to_microbatches(a, axis):
    t = _jnp.moveaxis(a, axis, 0)
    t = t.reshape((N_MICROBATCH, t.shape[0] // N_MICROBATCH) + t.shape[1:])
    return _jnp.moveaxis(t, 1, axis + 1)


def setup_inputs(seed: int = 0) -> dict:
    inp = _fwd_setup_inputs(seed)
    key = _jax.random.fold_in(_jax.random.key(seed), 7919)
    shape, _ = _output_shape()
    out = dict(inp)
    out["loss_target"] = _jax.random.normal(_jax.random.fold_in(key, 0), shape, _jnp.float32)
    for i, name in enumerate(TWIN_WEIGHTS):
        w = inp[name].astype(_jnp.float32)
        if MOMENT_SCALE is None:
            s = _jnp.sqrt(_jnp.mean(_jnp.square(w)) + 1e-30)
        else:
            s = MOMENT_SCALE[name]
        km, kv = _jax.random.split(_jax.random.fold_in(key, i + 1))
        out[name] = w
        out["m_" + name] = s * _jax.random.normal(km, w.shape, _jnp.float32)
        out["v_" + name] = (s * s) * _jax.random.uniform(kv, w.shape, _jnp.float32, 0.5, 1.5)
    if N_MICROBATCH > 1:
        for name, axis in PER_EXAMPLE_BATCH_AXIS.items():
            out[name] = _to_microbatches(out[name], axis)
    return {'x': out['x'], 'c': out['c'], 'ctx': out['ctx'], 'c_ctx': out['c_ctx'], 'w_ada': out['w_ada'], 'b_ada': out['b_ada'], 'w_in': out['w_in'], 'w_pool': out['w_pool'], 'pool_scale': out['pool_scale'], 'ssm_a_re': out['ssm_a_re'], 'ssm_a_im': out['ssm_a_im'], 'ssm_log_dt': out['ssm_log_dt'], 'ssm_b_re': out['ssm_b_re'], 'ssm_b_im': out['ssm_b_im'], 'ssm_c_re': out['ssm_c_re'], 'ssm_c_im': out['ssm_c_im'], 'ssm_d': out['ssm_d'], 'w_glu': out['w_glu'], 'w_out': out['w_out'], 'g_pre_mix': out['g_pre_mix'], 'g_post_mix': out['g_post_mix'], 'g_pre_ffn': out['g_pre_ffn'], 'g_post_ffn': out['g_post_ffn'], 'w_up': out['w_up'], 'w_conv': out['w_conv'], 'w_down': out['w_down'], 'loss_target': out['loss_target'], 'm_c_ctx': out['m_c_ctx'], 'm_w_ada': out['m_w_ada'], 'm_b_ada': out['m_b_ada'], 'm_w_in': out['m_w_in'], 'm_w_pool': out['m_w_pool'], 'm_pool_scale': out['m_pool_scale'], 'm_ssm_a_re': out['m_ssm_a_re'], 'm_ssm_a_im': out['m_ssm_a_im'], 'm_ssm_log_dt': out['m_ssm_log_dt'], 'm_ssm_b_re': out['m_ssm_b_re'], 'm_ssm_b_im': out['m_ssm_b_im'], 'm_ssm_c_re': out['m_ssm_c_re'], 'm_ssm_c_im': out['m_ssm_c_im'], 'm_ssm_d': out['m_ssm_d'], 'm_w_glu': out['m_w_glu'], 'm_w_out': out['m_w_out'], 'm_g_pre_mix': out['m_g_pre_mix'], 'm_g_post_mix': out['m_g_post_mix'], 'm_g_pre_ffn': out['m_g_pre_ffn'], 'm_g_post_ffn': out['m_g_post_ffn'], 'm_w_up': out['m_w_up'], 'm_w_conv': out['m_w_conv'], 'm_w_down': out['m_w_down'], 'v_c_ctx': out['v_c_ctx'], 'v_w_ada': out['v_w_ada'], 'v_b_ada': out['v_b_ada'], 'v_w_in': out['v_w_in'], 'v_w_pool': out['v_w_pool'], 'v_pool_scale': out['v_pool_scale'], 'v_ssm_a_re': out['v_ssm_a_re'], 'v_ssm_a_im': out['v_ssm_a_im'], 'v_ssm_log_dt': out['v_ssm_log_dt'], 'v_ssm_b_re': out['v_ssm_b_re'], 'v_ssm_b_im': out['v_ssm_b_im'], 'v_ssm_c_re': out['v_ssm_c_re'], 'v_ssm_c_im': out['v_ssm_c_im'], 'v_ssm_d': out['v_ssm_d'], 'v_w_glu': out['v_w_glu'], 'v_w_out': out['v_w_out'], 'v_g_pre_mix': out['v_g_pre_mix'], 'v_g_post_mix': out['v_g_post_mix'], 'v_g_pre_ffn': out['v_g_pre_ffn'], 'v_g_post_ffn': out['v_g_post_ffn'], 'v_w_up': out['v_w_up'], 'v_w_conv': out['v_w_conv'], 'v_w_down': out['v_w_down']}


def _loss(weights, diff, rest, loss_target):
    with _jax.named_scope("forward"):
        args = {**rest, TWIN_DIFF_INPUT: diff, **{k: w.astype(_WEIGHT_DTYPES[k]) for k, w in weights.items()}}
        y = _forward(args)
    with _jax.named_scope("loss_head"):
        err = _jnp.square(y.astype(_jnp.float32) - loss_target)
        return 0.5 * _jnp.sum(_jnp.mean(err, axis=-1)) if err.ndim else 0.5 * err


def _adamw(w, g, m, v):
    m = ADAM_B1 * m + (1.0 - ADAM_B1) * g
    v = ADAM_B2 * v + (1.0 - ADAM_B2) * _jnp.square(g)
    m_hat = m / (1.0 - ADAM_B1 ** ADAM_STEP)
    v_hat = v / (1.0 - ADAM_B2 ** ADAM_STEP)
    delta = -ADAM_LR * (m_hat / (_jnp.sqrt(v_hat) + ADAM_EPS) + ADAM_WD * w)
    return delta, m, v


def reference(x, c, ctx, c_ctx, w_ada, b_ada, w_in, w_pool, pool_scale, ssm_a_re, ssm_a_im, ssm_log_dt, ssm_b_re, ssm_b_im, ssm_c_re, ssm_c_im, ssm_d, w_glu, w_out, g_pre_mix, g_post_mix, g_pre_ffn, g_post_ffn, w_up, w_conv, w_down, loss_target, m_c_ctx, m_w_ada, m_b_ada, m_w_in, m_w_pool, m_pool_scale, m_ssm_a_re, m_ssm_a_im, m_ssm_log_dt, m_ssm_b_re, m_ssm_b_im, m_ssm_c_re, m_ssm_c_im, m_ssm_d, m_w_glu, m_w_out, m_g_pre_mix, m_g_post_mix, m_g_pre_ffn, m_g_post_ffn, m_w_up, m_w_conv, m_w_down, v_c_ctx, v_w_ada, v_b_ada, v_w_in, v_w_pool, v_pool_scale, v_ssm_a_re, v_ssm_a_im, v_ssm_log_dt, v_ssm_b_re, v_ssm_b_im, v_ssm_c_re, v_ssm_c_im, v_ssm_d, v_w_glu, v_w_out, v_g_pre_mix, v_g_post_mix, v_g_pre_ffn, v_g_post_ffn, v_w_up, v_w_conv, v_w_down):
    given = dict(x=x, c=c, ctx=ctx, c_ctx=c_ctx, w_ada=w_ada, b_ada=b_ada, w_in=w_in, w_pool=w_pool, pool_scale=pool_scale, ssm_a_re=ssm_a_re, ssm_a_im=ssm_a_im, ssm_log_dt=ssm_log_dt, ssm_b_re=ssm_b_re, ssm_b_im=ssm_b_im, ssm_c_re=ssm_c_re, ssm_c_im=ssm_c_im, ssm_d=ssm_d, w_glu=w_glu, w_out=w_out, g_pre_mix=g_pre_mix, g_post_mix=g_post_mix, g_pre_ffn=g_pre_ffn, g_post_ffn=g_post_ffn, w_up=w_up, w_conv=w_conv, w_down=w_down, loss_target=loss_target, m_c_ctx=m_c_ctx, m_w_ada=m_w_ada, m_b_ada=m_b_ada, m_w_in=m_w_in, m_w_pool=m_w_pool, m_pool_scale=m_pool_scale, m_ssm_a_re=m_ssm_a_re, m_ssm_a_im=m_ssm_a_im, m_ssm_log_dt=m_ssm_log_dt, m_ssm_b_re=m_ssm_b_re, m_ssm_b_im=m_ssm_b_im, m_ssm_c_re=m_ssm_c_re, m_ssm_c_im=m_ssm_c_im, m_ssm_d=m_ssm_d, m_w_glu=m_w_glu, m_w_out=m_w_out, m_g_pre_mix=m_g_pre_mix, m_g_post_mix=m_g_post_mix, m_g_pre_ffn=m_g_pre_ffn, m_g_post_ffn=m_g_post_ffn, m_w_up=m_w_up, m_w_conv=m_w_conv, m_w_down=m_w_down, v_c_ctx=v_c_ctx, v_w_ada=v_w_ada, v_b_ada=v_b_ada, v_w_in=v_w_in, v_w_pool=v_w_pool, v_pool_scale=v_pool_scale, v_ssm_a_re=v_ssm_a_re, v_ssm_a_im=v_ssm_a_im, v_ssm_log_dt=v_ssm_log_dt, v_ssm_b_re=v_ssm_b_re, v_ssm_b_im=v_ssm_b_im, v_ssm_c_re=v_ssm_c_re, v_ssm_c_im=v_ssm_c_im, v_ssm_d=v_ssm_d, v_w_glu=v_w_glu, v_w_out=v_w_out, v_g_pre_mix=v_g_pre_mix, v_g_post_mix=v_g_post_mix, v_g_pre_ffn=v_g_pre_ffn, v_g_post_ffn=v_g_post_ffn, v_w_up=v_w_up, v_w_conv=v_w_conv, v_w_down=v_w_down)
    weights = {n: given[n] for n in TWIN_WEIGHTS}
    shared = {n: given[n] for n in SHARED_INPUTS}
    per_example = {n: given[n] for n in ['x', 'c', 'ctx']}
    grad_fn = _jax.value_and_grad(_loss, argnums=(0, 1))

    def one_microbatch(ex, loss_target):
        ex = dict(ex)
        diff = ex.pop(TWIN_DIFF_INPUT)
        return grad_fn(weights, diff, {**shared, **ex}, loss_target)

    if N_MICROBATCH == 1:
        loss, (grad_w, grad_x) = one_microbatch(per_example, given["loss_target"])
    else:
        def body(carry, xs):
            loss_sum, grad_sum = carry
            l_k, (gw_k, gx_k) = one_microbatch(xs[0], xs[1])
            with _jax.named_scope("update"):
                return (loss_sum + l_k, _jax.tree.map(_jnp.add, grad_sum, gw_k)), gx_k

        init = (_jnp.zeros((), _jnp.float32), _jax.tree.map(_jnp.zeros_like, weights))
        (loss, grad_w), grad_x = _jax.lax.scan(body, init, (per_example, given["loss_target"]))
    with _jax.named_scope("update"):
        delta_w, new_m, new_v = {}, {}, {}
        for n in TWIN_WEIGHTS:
            delta_w[n], new_m[n], new_v[n] = _adamw(weights[n], grad_w[n], given["m_" + n], given["v_" + n])
    return (loss, grad_x, *[grad_w[n] for n in TWIN_WEIGHTS], *[delta_w[n] for n in TWIN_WEIGHTS],
            *[new_m[n] for n in TWIN_WEIGHTS], *[new_v[n] for n in TWIN_WEIGHTS])
```

```python
import functools
import math

import jax
import jax.numpy as jnp
from jax import lax
from jax.experimental import pallas as pl
from jax.experimental.pallas import tpu as pltpu

F32 = jnp.float32
BF16 = jnp.bfloat16

N_DEV = 8
GRID_W = 64
POOL_WINDOWS = (2, 4, 8, 16)
SSM_GROUP = 16
SSM_STATE = 64
EPS = 1e-6
ADAM_LR = 0.001
ADAM_B1 = 0.9
ADAM_B2 = 0.999
ADAM_EPS = 1e-08
ADAM_WD = 0.01
ADAM_STEP = 10

SUBLANE = 8
PACK = 16
LANE = 128
VMEM_LIMIT = 56 * 1024 * 1024
MESH = pl.DeviceIdType.MESH

WEIGHTS = ['c_ctx', 'w_ada', 'b_ada', 'w_in', 'w_pool', 'pool_scale', 'ssm_a_re', 'ssm_a_im', 'ssm_log_dt',
           'ssm_b_re', 'ssm_b_im', 'ssm_c_re', 'ssm_c_im', 'ssm_d', 'w_glu', 'w_out', 'g_pre_mix',
           'g_post_mix', 'g_pre_ffn', 'g_post_ffn', 'w_up', 'w_conv', 'w_down']
REPLICATED = ['c_ctx', 'b_ada', 'pool_scale', 'ssm_a_re', 'ssm_a_im', 'ssm_log_dt', 'ssm_b_re', 'ssm_b_im',
              'ssm_c_re', 'ssm_c_im', 'ssm_d', 'g_pre_mix', 'g_post_mix', 'g_pre_ffn', 'g_post_ffn']
BIG_SHARDED = ['w_in', 'w_out', 'w_up', 'w_down']
SMALL_SHARDED = ['w_pool', 'w_glu', 'w_conv']


def _pcall(body, **kw):
    return pl.pallas_call(body, **kw)


def _params(sem=None):
    return pltpu.CompilerParams(dimension_semantics=sem, vmem_limit_bytes=VMEM_LIMIT)


def _tile(n, cap, mult):
    if n <= cap:
        return n
    best = None
    d = mult
    while d <= cap:
        if n % d == 0:
            best = d
        d += mult
    assert best is not None, (n, cap, mult)
    return best


def _sigmoid(x):
    return 1.0 / (1.0 + jnp.exp(-x))


def _mm(a, b, *, ta=False, tb=False, out_dtype=F32, out_blocked=False, name):
    b_blocked = b.ndim == 3
    if ta:
        K, M = a.shape
    else:
        M, K = a.shape
    if b_blocked:
        nb, br, bc = b.shape
        N, Kb = (br, nb * bc) if tb else (nb * bc, br)
    else:
        N, Kb = (b.shape if tb else b.shape[::-1])
    assert K == Kb, (a.shape, b.shape, ta, tb)
    tm = _tile(M, 1088, LANE if ta else PACK)
    if out_blocked:
        assert N % N_DEV == 0
        tn = N // N_DEV
    elif b_blocked and not tb:
        tn = bc
    else:
        tn = _tile(N, 512, LANE)
    if b_blocked and not tb:
        assert tn == bc
    if b_blocked and tb:
        tk = bc
    else:
        tk = _tile(K, 2048, LANE if (not ta or tb) else PACK)
    nm, nn, nk = M // tm, N // tn, K // tk

    a_spec = pl.BlockSpec((tk, tm), lambda i, j, k: (k, i)) if ta else pl.BlockSpec((tm, tk), lambda i, j, k: (i, k))
    if b_blocked:
        if tb:
            b_spec = pl.BlockSpec((None, tn, tk), lambda i, j, k: (k, j, 0))
        else:
            b_spec = pl.BlockSpec((None, tk, tn), lambda i, j, k: (j, k, 0))
    else:
        b_spec = pl.BlockSpec((tn, tk), lambda i, j, k: (j, k)) if tb else pl.BlockSpec((tk, tn), lambda i, j, k: (k, j))
    if out_blocked:
        o_spec = pl.BlockSpec((None, tm, tn), lambda i, j, k: (j, i, 0))
        o_shape = jax.ShapeDtypeStruct((N_DEV, M, tn), out_dtype)
    else:
        o_spec = pl.BlockSpec((tm, tn), lambda i, j, k: (i, j))
        o_shape = jax.ShapeDtypeStruct((M, N), out_dtype)
    dims = ((((0 if ta else 1),), ((1 if tb else 0),)), ((), ()))

    def body(a_ref, b_ref, o_ref, acc_ref):
        k = pl.program_id(2)
        av = a_ref[...]
        if ta:
            av = av.astype(F32).T
        part = lax.dot_general(av.astype(BF16), b_ref[...].astype(BF16),
                               (((1,), dims[0][1]), ((), ())), preferred_element_type=F32)
        if nk == 1:
            o_ref[...] = part.astype(o_ref.dtype)
        else:
            @pl.when(k == 0)
            def _():
                acc_ref[...] = part

            @pl.when(k > 0)
            def _():
                acc_ref[...] += part

            @pl.when(k == nk - 1)
            def _():
                o_ref[...] = acc_ref[...].astype(o_ref.dtype)

    return _pcall(body, name=name, grid=(nm, nn, nk), in_specs=[a_spec, b_spec], out_specs=o_spec,
                  out_shape=o_shape, scratch_shapes=[pltpu.VMEM((tm, tn), F32)],
                  compiler_params=_params(("parallel", "parallel", "arbitrary")))(a, b)


def _make_linear(out_dtype, name):
    @jax.custom_vjp
    def lin(a, w):
        return _mm(a, w, out_dtype=out_dtype, name=name)

    def fwd(a, w):
        return lin(a, w), (a, w)

    def bwd(res, dy):
        a, w = res
        da = _mm(dy, w, tb=True, out_dtype=a.dtype, name=name + "_da")
        dw = _mm(a, dy, ta=True, out_dtype=w.dtype, out_blocked=(w.ndim == 3), name=name + "_dw")
        return da, dw

    lin.defvjp(fwd, bwd)
    return lin


def _make_rowop(fn, n_row, n_seg, n_bc, out_dtypes, tr, name):
    def specs(args):
        rows, segs, bcs = args[:n_row], args[n_row:n_row + n_seg], args[n_row + n_seg:]
        sp = [pl.BlockSpec((tr, r.shape[1]), lambda i: (i, 0)) for r in rows]
        sp += [pl.BlockSpec((None, 1, s.shape[2]), lambda i: (jnp.minimum(i, 1), 0, 0)) for s in segs]
        sp += [pl.BlockSpec((1, b.shape[1]), lambda i: (0, 0)) for b in bcs]
        return sp

    def out_widths(args):
        tiles = [jax.ShapeDtypeStruct((tr, a.shape[-1]), a.dtype) for a in args[:n_row]]
        tiles += [jax.ShapeDtypeStruct((1, a.shape[-1]), a.dtype) for a in args[n_row:]]
        return [o.shape[1] for o in jax.eval_shape(fn, *tiles)]

    def fwd_call(*args):
        n = args[0].shape[0]
        widths = out_widths(args)
        n_in = len(args)

        def body(*refs):
            vals = [r[...] for r in refs[:n_in]]
            outs = fn(*vals)
            for o_ref, o in zip(refs[n_in:], outs):
                o_ref[...] = o.astype(o_ref.dtype)

        return _pcall(body, name=name, grid=(n // tr,), in_specs=specs(args),
                      out_specs=[pl.BlockSpec((tr, w), lambda i: (i, 0)) for w in widths],
                      out_shape=[jax.ShapeDtypeStruct((n, w), d) for w, d in zip(widths, out_dtypes)],
                      compiler_params=_params(("parallel",)))(*args)

    def bwd_call(args, cots):
        n = args[0].shape[0]
        n_in = len(args)
        n_ct = len(cots)
        rows, segs, bcs = args[:n_row], args[n_row:n_row + n_seg], args[n_row + n_seg:]

        def body(*refs):
            i = pl.program_id(0)
            vals = [r[...] for r in refs[:n_in]]
            cts = [r[...].astype(F32) for r in refs[n_in:n_in + n_ct]]
            outs = refs[n_in + n_ct:]
            _, vjp = jax.vjp(lambda *v: tuple(fn(*v)), *vals)
            grads = vjp(tuple(cts))
            for o_ref, g in zip(outs[:n_row], grads[:n_row]):
                o_ref[...] = g.astype(o_ref.dtype)
            for o_ref, g in zip(outs[n_row:n_row + n_seg], grads[n_row:n_row + n_seg]):
                @pl.when(i <= 1)
                def _():
                    o_ref[...] = jnp.zeros_like(o_ref)
                o_ref[...] += g.astype(F32)
            for o_ref, g in zip(outs[n_row + n_seg:], grads[n_row + n_seg:]):
                @pl.when(i == 0)
                def _():
                    o_ref[...] = jnp.zeros_like(o_ref)
                o_ref[...] += g.astype(F32)

        out_specs = [pl.BlockSpec((tr, r.shape[1]), lambda i: (i, 0)) for r in rows]
        out_specs += [pl.BlockSpec((None, 1, s.shape[2]), lambda i: (jnp.minimum(i, 1), 0, 0)) for s in segs]
        out_specs += [pl.BlockSpec((1, b.shape[1]), lambda i: (0, 0)) for b in bcs]
        out_shape = [jax.ShapeDtypeStruct(r.shape, r.dtype) for r in rows]
        out_shape += [jax.ShapeDtypeStruct(s.shape, F32) for s in segs]
        out_shape += [jax.ShapeDtypeStruct(b.shape, F32) for b in bcs]
        in_specs = specs(args) + [pl.BlockSpec((tr, c.shape[1]), lambda i: (i, 0)) for c in cots]
        return _pcall(body, name=name + "_bwd", grid=(n // tr,), in_specs=in_specs, out_specs=out_specs,
                      out_shape=out_shape, compiler_params=_params(("arbitrary",)))(*args, *cots)

    @jax.custom_vjp
    def op(*args):
        return tuple(fwd_call(*args))

    def op_fwd(*args):
        return op(*args), args

    def op_bwd(args, cots):
        return tuple(bwd_call(args, list(cots)))

    op.defvjp(op_fwd, op_bwd)
    return op


def _rms(x, g):
    return x * lax.rsqrt(jnp.mean(x * x, axis=-1, keepdims=True) + EPS) * g


def _fn_rmsmod(x, shift, scale, g):
    return (_rms(x.astype(F32), g) * (1.0 + scale) + shift,)


def _fn_resid(x, m, gate, g):
    return (x + gate * _rms(m.astype(F32), g),)


def _fn_gelu_head(y, u, d):
    return (jax.nn.gelu(y + d * u.astype(F32)),)


def _fn_mixcat(y0, y1, y2, y3, gl, s, ps):
    pool = jnp.concatenate([y0, y1, y2, y3], axis=1) * ps
    return (jnp.concatenate([pool, gl * _sigmoid(s)], axis=1),)


def _pool_call(x, col_blk0, ncol, w, n_ctx, n_lat, transpose, out_dtype, name):
    n = n_ctx + n_lat
    cw = LANE
    r = n_ctx
    gap = SUBLANE
    half = w // 2
    lat0 = 2 * gap + n_ctx
    nbuf = 3 * gap + n

    def body(x_ref, o_ref, buf):
        def inv_cnt(seg_len, t0):
            t = t0 + lax.broadcasted_iota(jnp.int32, (r, 1), 0)
            cnt = jnp.minimum(t + half, seg_len) - jnp.maximum(t - half, 0)
            return 1.0 / cnt.astype(F32)

        zero = jnp.zeros((gap, cw), F32)
        buf[0:gap, :] = zero
        buf[gap + n_ctx:lat0, :] = zero
        buf[lat0 + n_lat:nbuf, :] = zero

        def fill(src0, dst0, seg_len, t0):
            v = x_ref[pl.ds(src0, r), :].astype(F32)
            if transpose:
                v = v * inv_cnt(seg_len, t0)
            buf[pl.ds(dst0, r), :] = v

        def compute(src0, dst0, seg_len, t0):
            win = buf[pl.ds(dst0 - gap, r + 2 * gap), :]
            nw = r + 2 * gap
            s = win + pltpu.roll(win, (nw - 1) if transpose else 1, 0)
            for sh in (1, 2, 4):
                if w >= 4 * sh:
                    s = pltpu.roll(s, sh, 0) + pltpu.roll(s, nw - sh, 0)
            ws = s[gap:gap + r]
            if transpose:
                out = ws - x_ref[pl.ds(src0, r), :].astype(F32)
            else:
                out = ws * inv_cnt(seg_len, t0) - win[gap:gap + r]
            o_ref[pl.ds(src0, r), :] = out.astype(o_ref.dtype)

        for step in (fill, compute):
            step(0, gap, n_ctx, 0)

            def lat(i, c, step=step):
                off = pl.multiple_of(i * r, r)
                step(n_ctx + off, lat0 + off, n_lat, off)
                return c
            lax.fori_loop(0, n_lat // r, lat, 0)

    return _pcall(body, name=name, grid=(ncol,),
                  in_specs=[pl.BlockSpec((n, cw), lambda j: (0, col_blk0 + j))],
                  out_specs=pl.BlockSpec((n, cw), lambda j: (0, j)),
                  out_shape=jax.ShapeDtypeStruct((n, ncol * cw), out_dtype),
                  scratch_shapes=[pltpu.VMEM((nbuf, cw), F32)],
                  compiler_params=_params(("parallel",)))(x)


def _make_pool(n_ctx, n_lat, pool_width, mix_width):
    ncol = pool_width // len(POOL_WINDOWS) // LANE

    @jax.custom_vjp
    def pool(u):
        return tuple(_pool_call(u, g * ncol, ncol, w, n_ctx, n_lat, False, BF16, "pool_w%d" % w)
                     for g, w in enumerate(POOL_WINDOWS))

    def fwd(u):
        return pool(u), None

    def bwd(_, cots):
        parts = [_pool_call(ct, 0, ncol, w, n_ctx, n_lat, True, F32, "pool_w%d_bwd" % w)
                 for ct, w in zip(cots, POOL_WINDOWS)]
        parts.append(jnp.zeros((n_ctx + n_lat, mix_width - pool_width), F32))
        return (jnp.concatenate(parts, axis=1),)

    pool.defvjp(fwd, bwd)
    return pool


class _ConvGeom:
    def __init__(self, n_ctx, n_lat):
        self.n_ctx, self.n_lat = n_ctx, n_lat
        self.gap = GRID_W + SUBLANE
        self.r = 2 * GRID_W
        self.ctx0 = self.gap
        self.lat0 = 2 * self.gap + n_ctx
        self.nbuf = 3 * self.gap + n_ctx + n_lat
        self.nwin = self.r + 2 * self.gap
        self.n16 = self.r + 2 * SUBLANE
        assert n_lat % self.r == 0 and n_ctx % self.r == 0

    def zero_gaps(self, buf):
        z = jnp.zeros((self.gap, LANE), F32)
        buf[0:self.gap, :] = z
        buf[self.ctx0 + self.n_ctx:self.lat0, :] = z
        buf[self.lat0 + self.n_lat:self.nbuf, :] = z

    def fill(self, src_ref, buf):
        r = self.r

        def seg(src0, dst0, count):
            def one(i, c):
                off = pl.multiple_of(i * r, r)
                buf[pl.ds(dst0 + off, r), :] = src_ref[pl.ds(src0 + off, r), :].astype(F32)
                return c
            lax.fori_loop(0, count, one, 0)
        seg(0, self.ctx0, self.n_ctx // r)
        seg(self.n_ctx, self.lat0, self.n_lat // r)

    def col_masks(self, nrows, first_col):
        col = (lax.broadcasted_iota(jnp.int32, (nrows, 1), 0) + first_col) & (GRID_W - 1)
        return col == GRID_W - 1, col == 0

    def lat_window(self, buf, i):
        ws = pl.multiple_of(self.lat0 - self.gap + i * self.r, SUBLANE)
        return buf[pl.ds(ws, self.nwin), :]

    def ctx_window(self, buf):
        return buf[self.ctx0 - SUBLANE:self.ctx0 + self.n_ctx + SUBLANE, :]

    def lat_sources(self, win):
        last, first = self.col_masks(self.nwin, GRID_W - SUBLANE)
        return jnp.where(last, 0.0, win), win, jnp.where(first, 0.0, win)

    def row_slice(self, x, di, sign):
        st = self.gap - SUBLANE + sign * (di - 1) * GRID_W
        return x[st:st + self.n16]

    def lat_conv(self, win, w):
        srcs = self.lat_sources(win)
        cs = []
        for dj in range(3):
            acc = None
            for di in range(3):
                term = w[di * 3 + dj] * self.row_slice(srcs[dj], di, 1)
                acc = term if acc is None else acc + term
            cs.append(acc)
        out = pltpu.roll(cs[0], 1, 0) + cs[1] + pltpu.roll(cs[2], self.n16 - 1, 0)
        return out[SUBLANE:SUBLANE + self.r]

    def ctx_conv(self, win, w, transpose=False):
        n = win.shape[0]
        lo, hi = (w[5], w[3]) if transpose else (w[3], w[5])
        out = lo * pltpu.roll(win, 1, 0) + w[4] * win + hi * pltpu.roll(win, n - 1, 0)
        return out[SUBLANE:SUBLANE + self.n_ctx]

    def lat_conv_t(self, dwin, w):
        es = []
        for dj in range(3):
            acc = None
            for di in range(3):
                term = w[di * 3 + dj] * self.row_slice(dwin, di, -1)
                acc = term if acc is None else acc + term
            es.append(acc)
        last, first = self.col_masks(self.n16, GRID_W - SUBLANE)
        out = (jnp.where(last, 0.0, pltpu.roll(es[0], self.n16 - 1, 0)) + es[1]
               + jnp.where(first, 0.0, pltpu.roll(es[2], 1, 0)))
        return out[SUBLANE:SUBLANE + self.r]


def _taps(w_ref):
    return [w_ref[k:k + 1, :] for k in range(9)]


def _conv_specs(n, f_tiles):
    zv = pl.BlockSpec((n, LANE), lambda j: (0, j))
    zg = pl.BlockSpec((n, LANE), lambda j: (0, j + f_tiles))
    wv = pl.BlockSpec((9, LANE), lambda j: (0, j))
    wg = pl.BlockSpec((9, LANE), lambda j: (0, j + f_tiles))
    return zv, zg, wv, wg


def _conv_fwd_call(z, wc, n_ctx, n_lat):
    n, f2 = z.shape
    ft = f2 // 2 // LANE
    geo = _ConvGeom(n_ctx, n_lat)

    def body(zv_ref, zg_ref, wv_ref, wg_ref, o_ref, bv, bg):
        wv, wg = _taps(wv_ref), _taps(wg_ref)
        for src, buf in ((zv_ref, bv), (zg_ref, bg)):
            geo.zero_gaps(buf)
            geo.fill(src, buf)
        cv = geo.ctx_conv(geo.ctx_window(bv), wv)
        cg = geo.ctx_conv(geo.ctx_window(bg), wg)
        o_ref[0:n_ctx, :] = (cv * cg * _sigmoid(cg)).astype(o_ref.dtype)

        def chunk(i, c):
            cv = geo.lat_conv(geo.lat_window(bv, i), wv)
            cg = geo.lat_conv(geo.lat_window(bg, i), wg)
            off = pl.multiple_of(n_ctx + i * geo.r, SUBLANE)
            o_ref[pl.ds(off, geo.r), :] = (cv * cg * _sigmoid(cg)).astype(o_ref.dtype)
            return c
        lax.fori_loop(0, n_lat // geo.r, chunk, 0)

    return _pcall(body, name="conv_gate", grid=(ft,), in_specs=list(_conv_specs(n, ft)),
                  out_specs=pl.BlockSpec((n, LANE), lambda j: (0, j)),
                  out_shape=jax.ShapeDtypeStruct((n, f2 // 2), BF16),
                  scratch_shapes=[pltpu.VMEM((geo.nbuf, LANE), F32)] * 2,
                  compiler_params=_params(("parallel",)))(z, z, wc, wc)


def _conv_bwd_call(z, wc, da, n_ctx, n_lat):
    n, f2 = z.shape
    ft = f2 // 2 // LANE
    geo = _ConvGeom(n_ctx, n_lat)
    r, n16 = geo.r, geo.n16

    def body(zv_ref, zg_ref, wv_ref, wg_ref, da_ref, dzv_ref, dzg_ref, dwv_ref, dwg_ref, bv, bg, dv, dg):
        wv, wg = _taps(wv_ref), _taps(wg_ref)
        for buf in (bv, bg, dv, dg):
            geo.zero_gaps(buf)
        geo.fill(zv_ref, bv)
        geo.fill(zg_ref, bg)

        def gate_grads(cv, cg, d):
            sg = _sigmoid(cg)
            return d * cg * sg, d * cv * sg * (1.0 + cg * (1.0 - sg))

        def tap_sums(d_c, srcs, pad):
            zeros = jnp.zeros((SUBLANE, LANE), F32)
            dce = jnp.concatenate([zeros, d_c, zeros], axis=0)
            m = dce.shape[0]
            shifted = (pltpu.roll(dce, m - 1, 0), dce, pltpu.roll(dce, 1, 0))
            out = []
            for di in range(3):
                for dj in range(3):
                    src = srcs[dj] if pad is None else geo.row_slice(srcs[dj], di, 1)
                    out.append(jnp.sum(shifted[dj] * src, axis=0, keepdims=True))
            return out

        winv, wing = geo.ctx_window(bv), geo.ctx_window(bg)
        d_cv, d_cg = gate_grads(geo.ctx_conv(winv, wv), geo.ctx_conv(wing, wg), da_ref[0:n_ctx, :].astype(F32))
        dv[geo.ctx0:geo.ctx0 + n_ctx, :] = d_cv
        dg[geo.ctx0:geo.ctx0 + n_ctx, :] = d_cg
        zero_row = jnp.zeros((1, LANE), F32)
        acc0 = []
        for d_c, win in ((d_cv, winv), (d_cg, wing)):
            sums = tap_sums(d_c, (win, win, win), None)
            acc0 += [zero_row] * 3 + sums[3:6] + [zero_row] * 3

        def chunk(i, acc):
            winv, wing = geo.lat_window(bv, i), geo.lat_window(bg, i)
            off = pl.multiple_of(n_ctx + i * r, SUBLANE)
            d_cv, d_cg = gate_grads(geo.lat_conv(winv, wv), geo.lat_conv(wing, wg),
                                    da_ref[pl.ds(off, r), :].astype(F32))
            dst = pl.multiple_of(geo.lat0 + i * r, SUBLANE)
            dv[pl.ds(dst, r), :] = d_cv
            dg[pl.ds(dst, r), :] = d_cg
            sums = tap_sums(d_cv, geo.lat_sources(winv), True) + tap_sums(d_cg, geo.lat_sources(wing), True)
            return tuple(a + s for a, s in zip(acc, sums))
        acc = lax.fori_loop(0, n_lat // r, chunk, tuple(acc0))
        for k in range(9):
            dwv_ref[k:k + 1, :] = acc[k]
            dwg_ref[k:k + 1, :] = acc[9 + k]

        for dbuf, w, dz_ref in ((dv, wv, dzv_ref), (dg, wg, dzg_ref)):
            dz_ref[0:n_ctx, :] = geo.ctx_conv(geo.ctx_window(dbuf), w, transpose=True).astype(dz_ref.dtype)

            def chunk_t(i, c, dbuf=dbuf, w=w, dz_ref=dz_ref):
                off = pl.multiple_of(n_ctx + i * r, SUBLANE)
                dz_ref[pl.ds(off, r), :] = geo.lat_conv_t(geo.lat_window(dbuf, i), w).astype(dz_ref.dtype)
                return c
            lax.fori_loop(0, n_lat // r, chunk_t, 0)

    tile = pl.BlockSpec((n, LANE), lambda j: (0, j))
    wtile = pl.BlockSpec((9, LANE), lambda j: (0, j))
    dzv, dzg, dwv, dwg = _pcall(
        body, name="conv_gate_bwd", grid=(ft,), in_specs=list(_conv_specs(n, ft)) + [tile],
        out_specs=[tile, tile, wtile, wtile],
        out_shape=[jax.ShapeDtypeStruct((n, f2 // 2), z.dtype)] * 2 + [jax.ShapeDtypeStruct((9, f2 // 2), F32)] * 2,
        scratch_shapes=[pltpu.VMEM((geo.nbuf, LANE), F32)] * 4,
        compiler_params=_params(("parallel",)))(z, z, wc, wc, da)
    return jnp.concatenate([dzv, dzg], axis=1), jnp.concatenate([dwv, dwg], axis=1)


def _make_convgate(n_ctx, n_lat):
    @jax.custom_vjp
    def conv(z, wc):
        return _conv_fwd_call(z, wc, n_ctx, n_lat)

    def fwd(z, wc):
        return conv(z, wc), (z, wc)

    def bwd(res, da):
        return _conv_bwd_call(res[0], res[1], da, n_ctx, n_lat)

    conv.defvjp(fwd, bwd)
    return conv


def _cmul(ar, ai, br, bi):
    return ar * br - ai * bi, ar * bi + ai * br


def _segment_carries(fr, fi, pr, pi, down):
    row = lax.broadcasted_iota(jnp.int32, fr.shape, 0)

    def shift(t):
        if down:
            return jnp.where(row == 0, 0.0, pltpu.roll(t, 1, 0))
        return jnp.where(row == SUBLANE - 1, 0.0, pltpu.roll(t, SUBLANE - 1, 0))

    tr, ti = fr, fi
    for _ in range(SUBLANE - 1):
        mr, mi = _cmul(pr, pi, shift(tr), shift(ti))
        tr, ti = fr + mr, fi + mi
    return shift(tr), shift(ti)


def _scan_fwd_call(br, bi, lr, li):
    n, w = br.shape
    cb = 2 * LANE
    steps = n // SUBLANE

    def body(br_ref, bi_ref, lr_ref, li_ref, hr_ref, hi_ref):
        lam_r = jnp.broadcast_to(lr_ref[...], (SUBLANE, cb))
        lam_i = jnp.broadcast_to(li_ref[...], (SUBLANE, cb))
        zero = jnp.zeros((SUBLANE, cb), F32)

        def load(j):
            r0 = pl.multiple_of(j * SUBLANE, SUBLANE)
            return br_ref[pl.ds(r0, SUBLANE), :], bi_ref[pl.ds(r0, SUBLANE), :], r0

        def first(j, c):
            hr, hi, pr, pi = c
            xr, xi, _ = load(j)
            mr, mi = _cmul(lam_r, lam_i, hr, hi)
            qr, qi = _cmul(lam_r, lam_i, pr, pi)
            return mr + xr, mi + xi, qr, qi
        fr, fi, pr, pi = lax.fori_loop(0, steps, first, (zero, zero, zero + 1.0, zero), unroll=8)
        cr, ci = _segment_carries(fr, fi, pr, pi, down=True)

        def second(j, c):
            xr, xi, r0 = load(j)
            mr, mi = _cmul(lam_r, lam_i, c[0], c[1])
            hr, hi = mr + xr, mi + xi
            hr_ref[pl.ds(r0, SUBLANE), :] = hr
            hi_ref[pl.ds(r0, SUBLANE), :] = hi
            return hr, hi
        lax.fori_loop(0, steps, second, (cr, ci), unroll=8)

    big = pl.BlockSpec((n, cb), lambda i: (0, i))
    small = pl.BlockSpec((1, cb), lambda i: (0, i))
    return _pcall(body, name="s5_scan", grid=(w // cb,), in_specs=[big, big, small, small], out_specs=[big, big],
                  out_shape=[jax.ShapeDtypeStruct((n, w), F32)] * 2, compiler_params=_params(("parallel",)))(br, bi, lr, li)


def _scan_bwd_call(dr, di, hr, hi, lr, li):
    n, w = dr.shape
    cb = LANE
    steps = n // SUBLANE

    def body(dr_ref, di_ref, hr_ref, hi_ref, lr_ref, li_ref, ar_ref, ai_ref, gr_ref, gi_ref):
        lam_r = jnp.broadcast_to(lr_ref[...], (SUBLANE, cb))
        lam_i = -jnp.broadcast_to(li_ref[...], (SUBLANE, cb))
        zero = jnp.zeros((SUBLANE, cb), F32)
        row = lax.broadcasted_iota(jnp.int32, (SUBLANE, cb), 0)

        def rows(j):
            return pl.ds(pl.multiple_of(j * SUBLANE, SUBLANE), SUBLANE)

        def first(jj, c):
            ar, ai, pr, pi = c
            j = steps - 1 - jj
            mr, mi = _cmul(lam_r, lam_i, ar, ai)
            qr, qi = _cmul(lam_r, lam_i, pr, pi)
            return mr + dr_ref[rows(j), :], mi + di_ref[rows(j), :], qr, qi
        fr, fi, pr, pi = lax.fori_loop(0, steps, first, (zero, zero, zero + 1.0, zero), unroll=8)
        cr, ci = _segment_carries(fr, fi, pr, pi, down=False)

        def advance(j, ar, ai, gr, gi, prev_r, prev_i):
            mr, mi = _cmul(lam_r, lam_i, ar, ai)
            ar, ai = mr + dr_ref[rows(j), :], mi + di_ref[rows(j), :]
            ar_ref[rows(j), :] = ar
            ai_ref[rows(j), :] = ai
            return ar, ai, gr + ar * prev_r + ai * prev_i, gi + ai * prev_r - ar * prev_i

        def second(jj, c):
            j = steps - 1 - jj
            return advance(j, *c, hr_ref[rows(j - 1), :], hi_ref[rows(j - 1), :])
        ar, ai, gr, gi = lax.fori_loop(0, steps - 1, second, (cr, ci, zero, zero), unroll=8)
        last_r = jnp.where(row == 0, 0.0, pltpu.roll(hr_ref[rows(steps - 1), :], 1, 0))
        last_i = jnp.where(row == 0, 0.0, pltpu.roll(hi_ref[rows(steps - 1), :], 1, 0))
        _, _, gr, gi = advance(0, ar, ai, gr, gi, last_r, last_i)
        gr_ref[...] = gr
        gi_ref[...] = gi

    big = pl.BlockSpec((n, cb), lambda i: (0, i))
    small = pl.BlockSpec((1, cb), lambda i: (0, i))
    part = pl.BlockSpec((SUBLANE, cb), lambda i: (0, i))
    return _pcall(body, name="s5_scan_bwd", grid=(w // cb,), in_specs=[big, big, big, big, small, small],
                  out_specs=[big, big, part, part],
                  out_shape=[jax.ShapeDtypeStruct((n, w), F32)] * 2 + [jax.ShapeDtypeStruct((SUBLANE, w), F32)] * 2,
                  compiler_params=_params(("parallel",)))(dr, di, hr, hi, lr, li)


@jax.custom_vjp
def _scan(br, bi, lr, li):
    return tuple(_scan_fwd_call(br, bi, lr, li))


def _scan_fwd(br, bi, lr, li):
    hr, hi = _scan_fwd_call(br, bi, lr, li)
    return (hr, hi), (hr, hi, lr, li)


def _scan_bwd(res, cots):
    hr, hi, lr, li = res
    ar, ai, gr, gi = _scan_bwd_call(cots[0], cots[1], hr, hi, lr, li)
    return ar, ai, jnp.sum(gr, axis=0, keepdims=True), jnp.sum(gi, axis=0, keepdims=True)


_scan.defvjp(_scan_fwd, _scan_bwd)


def _interleave(u):
    n, c = u.shape
    return u.reshape(SUBLANE, n // SUBLANE, c).transpose(1, 0, 2).reshape(n, c)


def _deinterleave(u):
    n, c = u.shape
    return u.reshape(n // SUBLANE, SUBLANE, c).transpose(1, 0, 2).reshape(n, c)


def _position():
    return lax.axis_index("x"), lax.axis_index("y"), lax.axis_index("c")


def _linear_index():
    x, y, c = _position()
    return 4 * x + 2 * y + c


def _ag_call(shard, name):
    any_spec = pl.BlockSpec(memory_space=pl.ANY)

    def body(x_ref, o_ref, send_sems, recv_sems, local_sem):
        x, y, c = _position()
        me, sibling = (x, y, c), (x, y, 1 - c)
        chips = [(1 - x, y), (x, 1 - y), (1 - x, 1 - y)]

        def block(px, py, pc):
            return o_ref.at[4 * px + 2 * py + pc]

        def copy(k, blk, to, src=None):
            return pltpu.make_async_remote_copy(
                src_ref=block(*blk) if src is None else src, dst_ref=block(*blk),
                send_sem=send_sems.at[k], recv_sem=recv_sems.at[k], device_id=to, device_id_type=MESH)

        mine = pltpu.make_async_copy(x_ref, block(*me), local_sem)
        mine.start()
        first = [copy(0, me, sibling, src=x_ref)]
        first += [copy(1 + j, me, (*chip, c), src=x_ref) for j, chip in enumerate(chips)]
        for cp in first:
            cp.start()
        passed = [copy(4 + j, (*chip, c), sibling) for j, chip in enumerate(chips)]
        for j, chip in enumerate(chips):
            copy(1 + j, (*chip, c), me).wait_recv()
            passed[j].start()
        copy(0, sibling, me).wait_recv()
        for j, chip in enumerate(chips):
            copy(4 + j, (*chip, 1 - c), me).wait_recv()
        for cp in first + passed:
            cp.wait_send()
        mine.wait()

    return _pcall(body, name=name, in_specs=[any_spec], out_specs=any_spec,
                  out_shape=jax.ShapeDtypeStruct((N_DEV,) + shard.shape, shard.dtype),
                  scratch_shapes=[pltpu.SemaphoreType.DMA((7,)), pltpu.SemaphoreType.DMA((7,)),
                                  pltpu.SemaphoreType.DMA(())])(shard)


def _rs_call(g, name):
    any_spec = pl.BlockSpec(memory_space=pl.ANY)

    def body(g_ref, o_ref, send_sems, recv_sems, local_sem):
        x, y, c = _position()
        mine = pltpu.make_async_copy(g_ref.at[4 * x + 2 * y + c], o_ref.at[0], local_sem)
        mine.start()
        copies = []
        for k in range(1, N_DEV):
            px = 1 - x if k & 4 else x
            py = 1 - y if k & 2 else y
            pc = 1 - c if k & 1 else c
            cp = pltpu.make_async_remote_copy(
                src_ref=g_ref.at[4 * px + 2 * py + pc], dst_ref=o_ref.at[k],
                send_sem=send_sems.at[k - 1], recv_sem=recv_sems.at[k - 1],
                device_id=(px, py, pc), device_id_type=MESH)
            cp.start()
            copies.append(cp)
        for cp in copies:
            cp.wait()
        mine.wait()

    return _pcall(body, name=name, in_specs=[any_spec], out_specs=any_spec,
                  out_shape=jax.ShapeDtypeStruct(g.shape, g.dtype),
                  scratch_shapes=[pltpu.SemaphoreType.DMA((7,)), pltpu.SemaphoreType.DMA((7,)),
                                  pltpu.SemaphoreType.DMA(())])(g)


def _sum8_call(parts, name):
    _, r, c = parts.shape
    tr = _tile(r, max(PACK, (4 << 20) // (N_DEV * c * parts.dtype.itemsize) // PACK * PACK), PACK)

    def body(p_ref, o_ref):
        acc = p_ref[0].astype(F32)
        for k in range(1, N_DEV):
            acc = acc + p_ref[k].astype(F32)
        o_ref[...] = acc

    return _pcall(body, name=name, grid=(r // tr,),
                  in_specs=[pl.BlockSpec((N_DEV, tr, c), lambda i: (0, i, 0))],
                  out_specs=pl.BlockSpec((tr, c), lambda i: (i, 0)),
                  out_shape=jax.ShapeDtypeStruct((r, c), F32), compiler_params=_params(("parallel",)))(parts)


def _reduce_scatter(g, name):
    return _sum8_call(_rs_call(g, name), name + "_sum")


def _make_gather(dtype, name):
    @jax.custom_vjp
    def gather(shard):
        return _ag_call(shard.astype(dtype), name)

    def fwd(shard):
        return gather(shard), None

    def bwd(_, ct):
        return (_reduce_scatter(ct, name + "_rs"),)

    gather.defvjp(fwd, bwd)
    return gather


def _adam_call(w, g, m, v, name):
    r, c = w.shape
    tr = _tile(r, max(SUBLANE, (1 << 20) // (4 * c) // SUBLANE * SUBLANE), SUBLANE)

    def body(w_ref, g_ref, m_ref, v_ref, d_ref, mo_ref, vo_ref):
        gv = g_ref[...]
        m2 = ADAM_B1 * m_ref[...] + (1.0 - ADAM_B1) * gv
        v2 = ADAM_B2 * v_ref[...] + (1.0 - ADAM_B2) * (gv * gv)
        m_hat = m2 / (1.0 - ADAM_B1 ** ADAM_STEP)
        v_hat = v2 / (1.0 - ADAM_B2 ** ADAM_STEP)
        d_ref[...] = -ADAM_LR * (m_hat / (jnp.sqrt(v_hat) + ADAM_EPS) + ADAM_WD * w_ref[...])
        mo_ref[...] = m2
        vo_ref[...] = v2

    spec = pl.BlockSpec((tr, c), lambda i: (i, 0))
    return _pcall(body, name=name, grid=(r // tr,), in_specs=[spec] * 4, out_specs=[spec] * 3,
                  out_shape=[jax.ShapeDtypeStruct((r, c), F32)] * 3, compiler_params=_params(("parallel",)))(w, g, m, v)


def _loss_call(y, target, tr):
    n, d = y.shape

    def body(y_ref, t_ref, s_ref, dy_ref):
        i = pl.program_id(0)
        err = y_ref[...] - t_ref[...]
        dy_ref[...] = err * (1.0 / d)
        part = jnp.sum(jnp.sum(err * err, axis=1, keepdims=True), axis=0, keepdims=True)

        @pl.when(i == 0)
        def _():
            s_ref[...] = jnp.zeros_like(s_ref)
        s_ref[...] += part

    spec = pl.BlockSpec((tr, d), lambda i: (i, 0))
    return _pcall(body, name="loss_head", grid=(n // tr,), in_specs=[spec, spec],
                  out_specs=[pl.BlockSpec((1, 1), lambda i: (0, 0)), spec],
                  out_shape=[jax.ShapeDtypeStruct((1, 1), F32), jax.ShapeDtypeStruct((n, d), F32)],
                  compiler_params=_params(("arbitrary",)))(y, target)


def _pack(arrays, rows_mult):
    flat = jnp.concatenate([a.reshape(-1).astype(F32) for a in arrays])
    rows = -(-flat.shape[0] // LANE)
    rows = -(-rows // rows_mult) * rows_mult
    return jnp.pad(flat, (0, rows * LANE - flat.shape[0])).reshape(rows, LANE)


def _unpack(buf, shapes, lead=()):
    flat = buf.reshape(lead + (-1,))
    out, pos = [], 0
    for s in shapes:
        size = math.prod(s)
        out.append(flat[..., pos:pos + size].reshape(lead + tuple(s)))
        pos += size
    return out


def _s5_discretise(a_re, a_im, log_dt, b_re, b_im):
    dt = jnp.exp(log_dt)[:, None]
    mag = jnp.exp(a_re * dt)
    lam_re = mag * jnp.cos(a_im * dt)
    lam_im = mag * jnp.sin(a_im * dt)
    denom = a_re * a_re + a_im * a_im
    nr, ni = lam_re - 1.0, lam_im
    f_re = (nr * a_re + ni * a_im) / denom
    f_im = (ni * a_re - nr * a_im) / denom
    bb_re = f_re[..., None] * b_re - f_im[..., None] * b_im
    bb_im = f_re[..., None] * b_im + f_im[..., None] * b_re
    return lam_re, lam_im, bb_re, bb_im


def _block_diag_in(bb):
    g, p, h = bb.shape
    return jnp.einsum('gph,gk->ghkp', bb, jnp.eye(g, dtype=F32)).reshape(g * h, g * p)


def _block_diag_out(cc):
    g, h, p = cc.shape
    return jnp.einsum('ghp,gk->gpkh', cc, jnp.eye(g, dtype=F32)).reshape(g * p, g * h)


def _flip_segments(u, n_ctx):
    return jnp.concatenate([jnp.flip(u[:n_ctx], axis=0), jnp.flip(u[n_ctx:], axis=0)], axis=0)


def _forward(x, p, ctx, s_c, n_ctx, n_lat):
    d_model = x.shape[1]
    depth = len(p['w_in'])
    mix = p['w_in'][0].shape[1]
    pool_width = p['pool_scale'].shape[1]
    tr = n_ctx
    me = _linear_index()

    rmsmod = _make_rowop(_fn_rmsmod, 1, 2, 1, [BF16], tr, "rms_modulate")
    resid = _make_rowop(_fn_resid, 2, 1, 1, [F32], tr, "gated_residual")
    gelu_head = _make_rowop(_fn_gelu_head, 2, 0, 1, [F32], tr, "ssm_gelu")
    mixcat = _make_rowop(_fn_mixcat, 6, 0, 1, [BF16], tr, "mix_concat")
    pool = _make_pool(n_ctx, n_lat, pool_width, mix)
    convgate = _make_convgate(n_ctx, n_lat)
    lin_f32 = {k: _make_linear(F32, k) for k in ("w_in", "w_pool", "ssm_in", "ssm_out", "w_glu", "w_out", "w_down", "ada")}
    lin_up = _make_linear(BF16, "w_up")
    gather_big = {k: _make_gather(BF16, "gather_" + k) for k in BIG_SHARDED}
    gather_small = _make_gather(F32, "gather_small")
    gather_mod = _make_gather(F32, "gather_mod")

    w_in = [gather_big['w_in'](w).reshape(d_model, mix) for w in p['w_in']]
    w_out = [gather_big['w_out'](w).reshape(mix, d_model) for w in p['w_out']]
    w_up = [gather_big['w_up'](w) for w in p['w_up']]
    w_down = [gather_big['w_down'](w).reshape(-1, d_model) for w in p['w_down']]
    small = [p[k] for k in SMALL_SHARDED]
    packed = gather_small(_pack(small, PACK))
    w_pool, w_glu, w_conv = _unpack(packed, [a.shape for a in small], lead=(N_DEV,))
    w_pool = jnp.moveaxis(w_pool, 0, 2).reshape(depth, len(POOL_WINDOWS), -1, w_pool.shape[-1])
    w_glu = jnp.moveaxis(w_glu, 0, 1).reshape(depth, -1, w_glu.shape[-1])
    w_conv = jnp.moveaxis(w_conv, 0, 3).reshape(depth, 9, -1)

    s_rows = jnp.concatenate([s_c, jax.nn.silu(p['c_ctx'])[None, :],
                              jnp.zeros((PACK - N_DEV - 1, d_model), F32)], axis=0)
    cols = p['w_ada'][0].shape[1]
    b_loc = lax.dynamic_slice_in_dim(p['b_ada'], me * cols, cols, axis=1)
    mod_loc = jnp.stack([lin_f32['ada'](s_rows, p['w_ada'][l]) + b_loc[l][None, :] for l in range(depth)])
    mod = gather_mod(mod_loc.reshape(depth * PACK, cols)).reshape(N_DEV, depth, PACK, cols)
    mod = jnp.moveaxis(mod, 0, 2).reshape(depth, PACK, 6, d_model)
    mod_lat = lax.dynamic_index_in_dim(mod, me, axis=1, keepdims=False)
    mod_ctx = mod[:, N_DEV]

    xs = jnp.concatenate([ctx, x], axis=0)
    for l in range(depth):
        def seg(k):
            return jnp.stack([mod_ctx[l, k], mod_lat[l, k]]).reshape(2, 1, d_model)

        def row(name):
            return p[name][l].reshape(1, -1)

        h1, = rmsmod(xs, seg(0), seg(1), row('g_pre_mix'))
        u = lin_f32['w_in'](h1, w_in[l])
        pooled = pool(u)
        yp = [lin_f32['w_pool'](pooled[g], w_pool[l, g]) for g in range(len(POOL_WINDOWS))]
        u_ssm = u[:, pool_width:]
        ys = []
        for d, u_dir in enumerate((u_ssm, _flip_segments(u_ssm, n_ctx))):
            lam_re, lam_im, bb_re, bb_im = _s5_discretise(
                p['ssm_a_re'][l, d], p['ssm_a_im'][l, d], p['ssm_log_dt'][l, d], p['ssm_b_re'][l, d], p['ssm_b_im'][l, d])
            ui = _interleave(u_dir)
            b_r = lin_f32['ssm_in'](ui, _block_diag_in(bb_re).astype(BF16))
            b_i = lin_f32['ssm_in'](ui, _block_diag_in(bb_im).astype(BF16))
            h_r, h_i = _scan(b_r, b_i, lam_re.reshape(1, -1), lam_im.reshape(1, -1))
            y = (lin_f32['ssm_out'](h_r, _block_diag_out(p['ssm_c_re'][l, d]).astype(BF16))
                 + lin_f32['ssm_out'](h_i, _block_diag_out(-p['ssm_c_im'][l, d]).astype(BF16)))
            ys.append(_deinterleave(y))
        y_ssm = ys[0] + _flip_segments(ys[1], n_ctx)
        gl, = gelu_head(y_ssm, u_ssm, row('ssm_d'))
        s = lin_f32['w_glu'](gl, w_glu[l])
        cat, = mixcat(*yp, gl, s, row('pool_scale'))
        mixed = lin_f32['w_out'](cat, w_out[l])
        x1, = resid(xs, mixed, seg(2), row('g_post_mix'))
        h2, = rmsmod(x1, seg(3), seg(4), row('g_pre_ffn'))
        z = lin_up(h2, w_up[l])
        a = convgate(z, w_conv[l])
        f = lin_f32['w_down'](a, w_down[l])
        xs, = resid(x1, f, seg(5), row('g_post_ffn'))
    return xs[n_ctx:]


def _as_rows(a):
    return a.reshape(-1, a.shape[-1])


def kernel(x, c, ctx, c_ctx, w_ada, b_ada, w_in, w_pool, pool_scale, ssm_a_re, ssm_a_im, ssm_log_dt, ssm_b_re, ssm_b_im, ssm_c_re, ssm_c_im, ssm_d, w_glu, w_out, g_pre_mix, g_post_mix, g_pre_ffn, g_post_ffn, w_up, w_conv, w_down, loss_target, m_c_ctx, m_w_ada, m_b_ada, m_w_in, m_w_pool, m_pool_scale, m_ssm_a_re, m_ssm_a_im, m_ssm_log_dt, m_ssm_b_re, m_ssm_b_im, m_ssm_c_re, m_ssm_c_im, m_ssm_d, m_w_glu, m_w_out, m_g_pre_mix, m_g_post_mix, m_g_pre_ffn, m_g_post_ffn, m_w_up, m_w_conv, m_w_down, v_c_ctx, v_w_ada, v_b_ada, v_w_in, v_w_pool, v_pool_scale, v_ssm_a_re, v_ssm_a_im, v_ssm_log_dt, v_ssm_b_re, v_ssm_b_im, v_ssm_c_re, v_ssm_c_im, v_ssm_d, v_w_glu, v_w_out, v_g_pre_mix, v_g_post_mix, v_g_pre_ffn, v_g_post_ffn, v_w_up, v_w_conv, v_w_down):
    weights = dict(zip(WEIGHTS, (c_ctx, w_ada, b_ada, w_in, w_pool, pool_scale, ssm_a_re, ssm_a_im, ssm_log_dt, ssm_b_re,
                                 ssm_b_im, ssm_c_re, ssm_c_im, ssm_d, w_glu, w_out, g_pre_mix, g_post_mix, g_pre_ffn,
                                 g_post_ffn, w_up, w_conv, w_down)))
    m_in = dict(zip(WEIGHTS, (m_c_ctx, m_w_ada, m_b_ada, m_w_in, m_w_pool, m_pool_scale, m_ssm_a_re, m_ssm_a_im,
                              m_ssm_log_dt, m_ssm_b_re, m_ssm_b_im, m_ssm_c_re, m_ssm_c_im, m_ssm_d, m_w_glu, m_w_out,
                              m_g_pre_mix, m_g_post_mix, m_g_pre_ffn, m_g_post_ffn, m_w_up, m_w_conv, m_w_down)))
    v_in = dict(zip(WEIGHTS, (v_c_ctx, v_w_ada, v_b_ada, v_w_in, v_w_pool, v_pool_scale, v_ssm_a_re, v_ssm_a_im,
                              v_ssm_log_dt, v_ssm_b_re, v_ssm_b_im, v_ssm_c_re, v_ssm_c_im, v_ssm_d, v_w_glu, v_w_out,
                              v_g_pre_mix, v_g_post_mix, v_g_pre_ffn, v_g_post_ffn, v_w_up, v_w_conv, v_w_down)))
    depth = w_in.shape[0]
    n_lat, d_model = x.shape[1], x.shape[2]
    n_ctx = ctx.shape[1]
    per_layer = BIG_SHARDED + ['w_ada']

    c_rows = jnp.concatenate([c, jnp.zeros((SUBLANE - 1, d_model), F32)], axis=0)
    s_c = jax.nn.silu(_ag_call(c_rows, "gather_c")[:, 0, :])

    params = {k: ([w[l] for l in range(depth)] if k in per_layer else w) for k, w in weights.items()}

    def run(x2d, prm):
        return _forward(x2d, prm, ctx[0], s_c, n_ctx, n_lat)

    y, vjp = jax.vjp(run, x[0], params)
    sq, dy = _loss_call(y, loss_target[0], n_ctx)
    loss = lax.psum(0.5 * sq[0, 0] / d_model, ("x", "y", "c"))
    gx, grads = vjp(dy)
    grads = {k: (jnp.stack(g) if k in per_layer else g) for k, g in grads.items()}

    rep_shapes = [weights[k].shape for k in REPLICATED]
    contrib = _pack([grads[k] for k in REPLICATED], N_DEV * PACK)
    rows = contrib.shape[0] // N_DEV
    total = _ag_call(_reduce_scatter(contrib.reshape(N_DEV, rows, LANE), "reduce_replicated"), "gather_replicated")
    total = total.reshape(N_DEV * rows, LANE)
    for k, g in zip(REPLICATED, _unpack(total, rep_shapes)):
        grads[k] = g

    delta, new_m, new_v = {}, {}, {}
    rep = [_pack([src[k] for k in REPLICATED], N_DEV * PACK) for src in (weights, m_in, v_in)]
    upd = _adam_call(rep[0], total, rep[1], rep[2], "adamw_replicated")
    for out, buf in zip((delta, new_m, new_v), upd):
        out.update(zip(REPLICATED, _unpack(buf, rep_shapes)))
    for k in WEIGHTS:
        if k in REPLICATED:
            continue
        upd = _adam_call(_as_rows(weights[k]), _as_rows(grads[k]), _as_rows(m_in[k]), _as_rows(v_in[k]), "adamw_" + k)
        for out, buf in zip((delta, new_m, new_v), upd):
            out[k] = buf.reshape(weights[k].shape)

    return (loss, gx[None], *[grads[k] for k in WEIGHTS], *[delta[k] for k in WEIGHTS],
            *[new_m[k] for k in WEIGHTS], *[new_v[k] for k in WEIGHTS])
```

```python
import functools
import math

import jax
import jax.numpy as jnp
from jax import lax
from jax.experimental import pallas as pl
from jax.experimental.pallas import tpu as pltpu

F32 = jnp.float32
BF16 = jnp.bfloat16

N_DEV = 8
GRID_W = 64
POOL_WINDOWS = (2, 4, 8, 16)
SSM_GROUP = 16
SSM_STATE = 64
EPS = 1e-6
ADAM_LR = 0.001
ADAM_B1 = 0.9
ADAM_B2 = 0.999
ADAM_EPS = 1e-08
ADAM_WD = 0.01
ADAM_STEP = 10

SUBLANE = 8
PACK = 16
LANE = 128
VMEM_LIMIT = 56 * 1024 * 1024
MESH = pl.DeviceIdType.MESH

WEIGHTS = ['c_ctx', 'w_ada', 'b_ada', 'w_in', 'w_pool', 'pool_scale', 'ssm_a_re', 'ssm_a_im', 'ssm_log_dt',
           'ssm_b_re', 'ssm_b_im', 'ssm_c_re', 'ssm_c_im', 'ssm_d', 'w_glu', 'w_out', 'g_pre_mix',
           'g_post_mix', 'g_pre_ffn', 'g_post_ffn', 'w_up', 'w_conv', 'w_down']
REPLICATED = ['c_ctx', 'b_ada', 'pool_scale', 'ssm_a_re', 'ssm_a_im', 'ssm_log_dt', 'ssm_b_re', 'ssm_b_im',
              'ssm_c_re', 'ssm_c_im', 'ssm_d', 'g_pre_mix', 'g_post_mix', 'g_pre_ffn', 'g_post_ffn']
BIG_SHARDED = ['w_in', 'w_out', 'w_up', 'w_down']
SMALL_SHARDED = ['w_pool', 'w_glu', 'w_conv']


def _pcall(body, **kw):
    return pl.pallas_call(body, **kw)


def _params(sem=None):
    return pltpu.CompilerParams(dimension_semantics=sem, vmem_limit_bytes=VMEM_LIMIT)


def _tile(n, cap, mult):
    if n <= cap:
        return n
    best = None
    d = mult
    while d <= cap:
        if n % d == 0:
            best = d
        d += mult
    assert best is not None, (n, cap, mult)
    return best


def _sigmoid(x):
    return 1.0 / (1.0 + jnp.exp(-x))


def _mm(a, b, *, tb=False, out_dtype=F32, out_blocked=False, side=None, name):
    b_blocked = b.ndim == 3
    M, K = a.shape
    if b_blocked:
        nb, br, bc = b.shape
        N, Kb = (br, nb * bc) if tb else (nb * bc, br)
    else:
        N, Kb = (b.shape if tb else b.shape[::-1])
    assert K == Kb, (a.shape, b.shape, tb)
    tm = _tile(M, 1408, PACK)
    if out_blocked:
        assert N % N_DEV == 0
        tn = N // N_DEV
    elif b_blocked and not tb:
        tn = bc
    else:
        tn = _tile(N, 512, LANE)
    if b_blocked and not tb:
        assert tn == bc
    if b_blocked and tb:
        tk = bc
    else:
        tk = _tile(K, 2304, LANE)
    nm, nn, nk = M // tm, N // tn, K // tk

    a_spec = pl.BlockSpec((tm, tk), lambda i, j, k: (i, k))
    if b_blocked:
        if tb:
            b_spec = pl.BlockSpec((None, tn, tk), lambda i, j, k: (k, j, 0))
        else:
            b_spec = pl.BlockSpec((None, tk, tn), lambda i, j, k: (j, k, 0))
    else:
        b_spec = pl.BlockSpec((tn, tk), lambda i, j, k: (j, k)) if tb else pl.BlockSpec((tk, tn), lambda i, j, k: (k, j))
    if out_blocked:
        o_spec = pl.BlockSpec((None, tm, tn), lambda i, j, k: (j, i, 0))
        o_shape = jax.ShapeDtypeStruct((N_DEV, M, tn), out_dtype)
    else:
        o_spec = pl.BlockSpec((tm, tn), lambda i, j, k: (i, j))
        o_shape = jax.ShapeDtypeStruct((M, N), out_dtype)
    dims = (((1,), ((1 if tb else 0),)), ((), ()))

    def matmul(a_ref, b_ref, o_ref, acc_ref):
        k = pl.program_id(2)
        part = lax.dot_general(a_ref[...].astype(BF16), b_ref[...].astype(BF16), dims, preferred_element_type=F32)
        if nk == 1:
            o_ref[...] = part.astype(o_ref.dtype)
        else:
            @pl.when(k == 0)
            def _():
                acc_ref[...] = part

            @pl.when(k > 0)
            def _():
                acc_ref[...] += part

            @pl.when(k == nk - 1)
            def _():
                o_ref[...] = acc_ref[...].astype(o_ref.dtype)

    acc = [pltpu.VMEM((tm, tn), F32)]
    if side is None:
        return _pcall(matmul, name=name, grid=(nm, nn, nk), in_specs=[a_spec, b_spec], out_specs=o_spec,
                      out_shape=o_shape, scratch_shapes=acc,
                      compiler_params=_params(("parallel", "parallel", "arbitrary")))(a, b)

    parts, src, side_shape = side

    def body(a_ref, b_ref, s_ref, o_ref, so_ref, acc_ref, *sems):
        step = (pl.program_id(0) * nn + pl.program_id(1)) * nk + pl.program_id(2)
        start, finish = parts(s_ref, so_ref, *sems)
        pl.when(step == 0)(start)
        matmul(a_ref, b_ref, o_ref, acc_ref)
        pl.when(step == nm * nn * nk - 1)(finish)

    return _pcall(body, name=name, grid=(nm, nn, nk), in_specs=[a_spec, b_spec, ANY_SPEC],
                  out_specs=[o_spec, ANY_SPEC], out_shape=[o_shape, side_shape], scratch_shapes=acc + COMM_SCRATCH,
                  compiler_params=_params(("arbitrary", "arbitrary", "arbitrary")))(a, b, src)


def _make_linear(out_dtype, name):
    @jax.custom_vjp
    def lin(a, w):
        return _mm(a, w, out_dtype=out_dtype, name=name)

    def fwd(a, w):
        return lin(a, w), (a, w)

    def bwd(res, dy):
        a, w = res
        da = _mm(dy, w, tb=True, out_dtype=a.dtype, name=name + "_da")
        dw = _mm(a.T, dy, out_dtype=w.dtype, out_blocked=(w.ndim == 3), name=name + "_dw")
        return da, dw

    lin.defvjp(fwd, bwd)
    return lin


def _make_hosting_linear(out_dtype, name):
    @jax.custom_vjp
    def lin(a, w, next_shard):
        nxt = next_shard.astype(BF16)
        side = (_gather_parts, nxt, jax.ShapeDtypeStruct((N_DEV,) + nxt.shape, BF16))
        return tuple(_mm(a, w, out_dtype=out_dtype, side=side, name=name + "_gather"))

    def fwd(a, w, next_shard):
        return lin(a, w, next_shard), (a, w)

    def bwd(res, cts):
        a, w = res
        dy, d_next = cts
        dw = _mm(a.T, dy, out_dtype=w.dtype, out_blocked=(w.ndim == 3), name=name + "_dw")
        side = (_scatter_parts, d_next, jax.ShapeDtypeStruct(d_next.shape, d_next.dtype))
        da, parts = _mm(dy, w, tb=True, out_dtype=a.dtype, side=side, name=name + "_da_scatter")
        return da, dw, _sum8_call(parts, name + "_sum")

    lin.defvjp(fwd, bwd)
    return lin


def _make_rowop(fn, n_row, n_seg, n_bc, out_dtypes, tr, name):
    def specs(args):
        rows, segs, bcs = args[:n_row], args[n_row:n_row + n_seg], args[n_row + n_seg:]
        sp = [pl.BlockSpec((tr, r.shape[1]), lambda i: (i, 0)) for r in rows]
        sp += [pl.BlockSpec((None, 1, s.shape[2]), lambda i: (jnp.minimum(i, 1), 0, 0)) for s in segs]
        sp += [pl.BlockSpec((1, b.shape[1]), lambda i: (0, 0)) for b in bcs]
        return sp

    def out_widths(args):
        tiles = [jax.ShapeDtypeStruct((tr, a.shape[-1]), a.dtype) for a in args[:n_row]]
        tiles += [jax.ShapeDtypeStruct((1, a.shape[-1]), a.dtype) for a in args[n_row:]]
        return [o.shape[1] for o in jax.eval_shape(fn, *tiles)]

    def fwd_call(*args):
        n = args[0].shape[0]
        widths = out_widths(args)
        n_in = len(args)

        def body(*refs):
            vals = [r[...] for r in refs[:n_in]]
            outs = fn(*vals)
            for o_ref, o in zip(refs[n_in:], outs):
                o_ref[...] = o.astype(o_ref.dtype)

        return _pcall(body, name=name, grid=(n // tr,), in_specs=specs(args),
                      out_specs=[pl.BlockSpec((tr, w), lambda i: (i, 0)) for w in widths],
                      out_shape=[jax.ShapeDtypeStruct((n, w), d) for w, d in zip(widths, out_dtypes)],
                      compiler_params=_params(("parallel",)))(*args)

    def bwd_call(args, cots):
        n = args[0].shape[0]
        n_in = len(args)
        n_ct = len(cots)
        rows, segs, bcs = args[:n_row], args[n_row:n_row + n_seg], args[n_row + n_seg:]

        def body(*refs):
            i = pl.program_id(0)
            vals = [r[...] for r in refs[:n_in]]
            cts = [r[...].astype(F32) for r in refs[n_in:n_in + n_ct]]
            outs = refs[n_in + n_ct:]
            _, vjp = jax.vjp(lambda *v: tuple(fn(*v)), *vals)
            grads = vjp(tuple(cts))
            for o_ref, g in zip(outs[:n_row], grads[:n_row]):
                o_ref[...] = g.astype(o_ref.dtype)
            for o_ref, g in zip(outs[n_row:n_row + n_seg], grads[n_row:n_row + n_seg]):
                @pl.when(i <= 1)
                def _():
                    o_ref[...] = jnp.zeros_like(o_ref)
                o_ref[...] += g.astype(F32)
            for o_ref, g in zip(outs[n_row + n_seg:], grads[n_row + n_seg:]):
                @pl.when(i == 0)
                def _():
                    o_ref[...] = jnp.zeros_like(o_ref)
                o_ref[...] += g.astype(F32)

        out_specs = [pl.BlockSpec((tr, r.shape[1]), lambda i: (i, 0)) for r in rows]
        out_specs += [pl.BlockSpec((None, 1, s.shape[2]), lambda i: (jnp.minimum(i, 1), 0, 0)) for s in segs]
        out_specs += [pl.BlockSpec((1, b.shape[1]), lambda i: (0, 0)) for b in bcs]
        out_shape = [jax.ShapeDtypeStruct(r.shape, r.dtype) for r in rows]
        out_shape += [jax.ShapeDtypeStruct(s.shape, F32) for s in segs]
        out_shape += [jax.ShapeDtypeStruct(b.shape, F32) for b in bcs]
        in_specs = specs(args) + [pl.BlockSpec((tr, c.shape[1]), lambda i: (i, 0)) for c in cots]
        return _pcall(body, name=name + "_bwd", grid=(n // tr,), in_specs=in_specs, out_specs=out_specs,
                      out_shape=out_shape, compiler_params=_params(("arbitrary",)))(*args, *cots)

    @jax.custom_vjp
    def op(*args):
        return tuple(fwd_call(*args))

    def op_fwd(*args):
        return op(*args), args

    def op_bwd(args, cots):
        return tuple(bwd_call(args, list(cots)))

    op.defvjp(op_fwd, op_bwd)
    return op


def _rms(x, g):
    return x * lax.rsqrt(jnp.mean(x * x, axis=-1, keepdims=True) + EPS) * g


def _fn_rmsmod(x, shift, scale, g):
    return (_rms(x.astype(F32), g) * (1.0 + scale) + shift,)


def _fn_resid(x, m, gate, g):
    return (x + gate * _rms(m.astype(F32), g),)


def _fn_gelu_head(y, u, d):
    return (jax.nn.gelu(y + d * u.astype(F32)),)


def _fn_mixcat(y0, y1, y2, y3, gl, s, ps):
    pool = jnp.concatenate([y0, y1, y2, y3], axis=1) * ps
    return (jnp.concatenate([pool, gl * _sigmoid(s)], axis=1),)


def _pool_call(x, col_blk0, ncol, w, n_ctx, n_lat, transpose, out_dtype, name):
    n = n_ctx + n_lat
    cw = LANE
    r = n_ctx
    gap = SUBLANE
    half = w // 2
    lat0 = 2 * gap + n_ctx
    nbuf = 3 * gap + n

    def body(x_ref, o_ref, buf):
        def inv_cnt(seg_len, t0):
            t = t0 + lax.broadcasted_iota(jnp.int32, (r, 1), 0)
            cnt = jnp.minimum(t + half, seg_len) - jnp.maximum(t - half, 0)
            return 1.0 / cnt.astype(F32)

        zero = jnp.zeros((gap, cw), F32)
        buf[0:gap, :] = zero
        buf[gap + n_ctx:lat0, :] = zero
        buf[lat0 + n_lat:nbuf, :] = zero

        def fill(src0, dst0, seg_len, t0):
            v = x_ref[pl.ds(src0, r), :].astype(F32)
            if transpose:
                v = v * inv_cnt(seg_len, t0)
            buf[pl.ds(dst0, r), :] = v

        def compute(src0, dst0, seg_len, t0):
            win = buf[pl.ds(dst0 - gap, r + 2 * gap), :]
            nw = r + 2 * gap
            s = win + pltpu.roll(win, (nw - 1) if transpose else 1, 0)
            for sh in (1, 2, 4):
                if w >= 4 * sh:
                    s = pltpu.roll(s, sh, 0) + pltpu.roll(s, nw - sh, 0)
            ws = s[gap:gap + r]
            if transpose:
                out = ws - x_ref[pl.ds(src0, r), :].astype(F32)
            else:
                out = ws * inv_cnt(seg_len, t0) - win[gap:gap + r]
            o_ref[pl.ds(src0, r), :] = out.astype(o_ref.dtype)

        for step in (fill, compute):
            step(0, gap, n_ctx, 0)

            def lat(i, c, step=step):
                off = pl.multiple_of(i * r, r)
                step(n_ctx + off, lat0 + off, n_lat, off)
                return c
            lax.fori_loop(0, n_lat // r, lat, 0)

    return _pcall(body, name=name, grid=(ncol,),
                  in_specs=[pl.BlockSpec((n, cw), lambda j: (0, col_blk0 + j))],
                  out_specs=pl.BlockSpec((n, cw), lambda j: (0, j)),
                  out_shape=jax.ShapeDtypeStruct((n, ncol * cw), out_dtype),
                  scratch_shapes=[pltpu.VMEM((nbuf, cw), F32)],
                  compiler_params=_params(("parallel",)))(x)


def _make_pool(n_ctx, n_lat, pool_width, mix_width):
    ncol = pool_width // len(POOL_WINDOWS) // LANE

    @jax.custom_vjp
    def pool(u):
        return tuple(_pool_call(u, g * ncol, ncol, w, n_ctx, n_lat, False, BF16, "pool_w%d" % w)
                     for g, w in enumerate(POOL_WINDOWS))

    def fwd(u):
        return pool(u), None

    def bwd(_, cots):
        parts = [_pool_call(ct, 0, ncol, w, n_ctx, n_lat, True, F32, "pool_w%d_bwd" % w)
                 for ct, w in zip(cots, POOL_WINDOWS)]
        parts.append(jnp.zeros((n_ctx + n_lat, mix_width - pool_width), F32))
        return (jnp.concatenate(parts, axis=1),)

    pool.defvjp(fwd, bwd)
    return pool


class _ConvGeom:
    def __init__(self, n_ctx, n_lat):
        self.n_ctx, self.n_lat = n_ctx, n_lat
        self.gap = GRID_W + SUBLANE
        self.r = 2 * GRID_W
        self.ctx0 = self.gap
        self.lat0 = 2 * self.gap + n_ctx
        self.nbuf = 3 * self.gap + n_ctx + n_lat
        self.nwin = self.r + 2 * self.gap
        self.n16 = self.r + 2 * SUBLANE
        assert n_lat % self.r == 0 and n_ctx % self.r == 0

    def zero_gaps(self, buf):
        z = jnp.zeros((self.gap, LANE), F32)
        buf[0:self.gap, :] = z
        buf[self.ctx0 + self.n_ctx:self.lat0, :] = z
        buf[self.lat0 + self.n_lat:self.nbuf, :] = z

    def fill(self, src_ref, buf):
        r = self.r

        def seg(src0, dst0, count):
            def one(i, c):
                off = pl.multiple_of(i * r, r)
                buf[pl.ds(dst0 + off, r), :] = src_ref[pl.ds(src0 + off, r), :].astype(F32)
                return c
            lax.fori_loop(0, count, one, 0)
        seg(0, self.ctx0, self.n_ctx // r)
        seg(self.n_ctx, self.lat0, self.n_lat // r)

    def col_masks(self, nrows, first_col):
        col = (lax.broadcasted_iota(jnp.int32, (nrows, 1), 0) + first_col) & (GRID_W - 1)
        return col == GRID_W - 1, col == 0

    def lat_window(self, buf, i):
        ws = pl.multiple_of(self.lat0 - self.gap + i * self.r, SUBLANE)
        return buf[pl.ds(ws, self.nwin), :]

    def ctx_window(self, buf):
        return buf[self.ctx0 - SUBLANE:self.ctx0 + self.n_ctx + SUBLANE, :]

    def lat_sources(self, win):
        last, first = self.col_masks(self.nwin, GRID_W - SUBLANE)
        return jnp.where(last, 0.0, win), win, jnp.where(first, 0.0, win)

    def row_slice(self, x, di, sign):
        st = self.gap - SUBLANE + sign * (di - 1) * GRID_W
        return x[st:st + self.n16]

    def lat_conv(self, win, w):
        srcs = self.lat_sources(win)
        cs = []
        for dj in range(3):
            acc = None
            for di in range(3):
                term = w[di * 3 + dj] * self.row_slice(srcs[dj], di, 1)
                acc = term if acc is None else acc + term
            cs.append(acc)
        out = pltpu.roll(cs[0], 1, 0) + cs[1] + pltpu.roll(cs[2], self.n16 - 1, 0)
        return out[SUBLANE:SUBLANE + self.r]

    def ctx_conv(self, win, w, transpose=False):
        n = win.shape[0]
        lo, hi = (w[5], w[3]) if transpose else (w[3], w[5])
        out = lo * pltpu.roll(win, 1, 0) + w[4] * win + hi * pltpu.roll(win, n - 1, 0)
        return out[SUBLANE:SUBLANE + self.n_ctx]

    def lat_conv_t(self, dwin, w):
        es = []
        for dj in range(3):
            acc = None
            for di in range(3):
                term = w[di * 3 + dj] * self.row_slice(dwin, di, -1)
                acc = term if acc is None else acc + term
            es.append(acc)
        last, first = self.col_masks(self.n16, GRID_W - SUBLANE)
        out = (jnp.where(last, 0.0, pltpu.roll(es[0], self.n16 - 1, 0)) + es[1]
               + jnp.where(first, 0.0, pltpu.roll(es[2], 1, 0)))
        return out[SUBLANE:SUBLANE + self.r]


def _taps(w_ref):
    return [w_ref[k:k + 1, :] for k in range(9)]


def _conv_specs(n, f_tiles):
    zv = pl.BlockSpec((n, LANE), lambda j: (0, j))
    zg = pl.BlockSpec((n, LANE), lambda j: (0, j + f_tiles))
    wv = pl.BlockSpec((9, LANE), lambda j: (0, j))
    wg = pl.BlockSpec((9, LANE), lambda j: (0, j + f_tiles))
    return zv, zg, wv, wg


def _conv_fwd_call(z, wc, n_ctx, n_lat):
    n, f2 = z.shape
    ft = f2 // 2 // LANE
    geo = _ConvGeom(n_ctx, n_lat)

    def body(zv_ref, zg_ref, wv_ref, wg_ref, o_ref, bv, bg):
        wv, wg = _taps(wv_ref), _taps(wg_ref)
        for src, buf in ((zv_ref, bv), (zg_ref, bg)):
            geo.zero_gaps(buf)
            geo.fill(src, buf)
        cv = geo.ctx_conv(geo.ctx_window(bv), wv)
        cg = geo.ctx_conv(geo.ctx_window(bg), wg)
        o_ref[0:n_ctx, :] = (cv * cg * _sigmoid(cg)).astype(o_ref.dtype)

        def chunk(i, c):
            cv = geo.lat_conv(geo.lat_window(bv, i), wv)
            cg = geo.lat_conv(geo.lat_window(bg, i), wg)
            off = pl.multiple_of(n_ctx + i * geo.r, SUBLANE)
            o_ref[pl.ds(off, geo.r), :] = (cv * cg * _sigmoid(cg)).astype(o_ref.dtype)
            return c
        lax.fori_loop(0, n_lat // geo.r, chunk, 0)

    return _pcall(body, name="conv_gate", grid=(ft,), in_specs=list(_conv_specs(n, ft)),
                  out_specs=pl.BlockSpec((n, LANE), lambda j: (0, j)),
                  out_shape=jax.ShapeDtypeStruct((n, f2 // 2), BF16),
                  scratch_shapes=[pltpu.VMEM((geo.nbuf, LANE), F32)] * 2,
                  compiler_params=_params(("parallel",)))(z, z, wc, wc)


def _conv_bwd_call(z, wc, da, n_ctx, n_lat):
    n, f2 = z.shape
    ft = f2 // 2 // LANE
    geo = _ConvGeom(n_ctx, n_lat)
    r, n16 = geo.r, geo.n16

    def body(zv_ref, zg_ref, wv_ref, wg_ref, da_ref, dzv_ref, dzg_ref, dwv_ref, dwg_ref, bv, bg, dv, dg):
        wv, wg = _taps(wv_ref), _taps(wg_ref)
        for buf in (bv, bg, dv, dg):
            geo.zero_gaps(buf)
        geo.fill(zv_ref, bv)
        geo.fill(zg_ref, bg)

        def gate_grads(cv, cg, d):
            sg = _sigmoid(cg)
            return d * cg * sg, d * cv * sg * (1.0 + cg * (1.0 - sg))

        def tap_sums(d_c, srcs, pad):
            zeros = jnp.zeros((SUBLANE, LANE), F32)
            dce = jnp.concatenate([zeros, d_c, zeros], axis=0)
            m = dce.shape[0]
            shifted = (pltpu.roll(dce, m - 1, 0), dce, pltpu.roll(dce, 1, 0))
            out = []
            for di in range(3):
                for dj in range(3):
                    src = srcs[dj] if pad is None else geo.row_slice(srcs[dj], di, 1)
                    out.append(jnp.sum(shifted[dj] * src, axis=0, keepdims=True))
            return out

        winv, wing = geo.ctx_window(bv), geo.ctx_window(bg)
        d_cv, d_cg = gate_grads(geo.ctx_conv(winv, wv), geo.ctx_conv(wing, wg), da_ref[0:n_ctx, :].astype(F32))
        dv[geo.ctx0:geo.ctx0 + n_ctx, :] = d_cv
        dg[geo.ctx0:geo.ctx0 + n_ctx, :] = d_cg
        zero_row = jnp.zeros((1, LANE), F32)
        acc0 = []
        for d_c, win in ((d_cv, winv), (d_cg, wing)):
            sums = tap_sums(d_c, (win, win, win), None)
            acc0 += [zero_row] * 3 + sums[3:6] + [zero_row] * 3

        def chunk(i, acc):
            winv, wing = geo.lat_window(bv, i), geo.lat_window(bg, i)
            off = pl.multiple_of(n_ctx + i * r, SUBLANE)
            d_cv, d_cg = gate_grads(geo.lat_conv(winv, wv), geo.lat_conv(wing, wg),
                                    da_ref[pl.ds(off, r), :].astype(F32))
            dst = pl.multiple_of(geo.lat0 + i * r, SUBLANE)
            dv[pl.ds(dst, r), :] = d_cv
            dg[pl.ds(dst, r), :] = d_cg
            sums = tap_sums(d_cv, geo.lat_sources(winv), True) + tap_sums(d_cg, geo.lat_sources(wing), True)
            return tuple(a + s for a, s in zip(acc, sums))
        acc = lax.fori_loop(0, n_lat // r, chunk, tuple(acc0))
        for k in range(9):
            dwv_ref[k:k + 1, :] = acc[k]
            dwg_ref[k:k + 1, :] = acc[9 + k]

        for dbuf, w, dz_ref in ((dv, wv, dzv_ref), (dg, wg, dzg_ref)):
            dz_ref[0:n_ctx, :] = geo.ctx_conv(geo.ctx_window(dbuf), w, transpose=True).astype(dz_ref.dtype)

            def chunk_t(i, c, dbuf=dbuf, w=w, dz_ref=dz_ref):
                off = pl.multiple_of(n_ctx + i * r, SUBLANE)
                dz_ref[pl.ds(off, r), :] = geo.lat_conv_t(geo.lat_window(dbuf, i), w).astype(dz_ref.dtype)
                return c
            lax.fori_loop(0, n_lat // r, chunk_t, 0)

    tile = pl.BlockSpec((n, LANE), lambda j: (0, j))
    wtile = pl.BlockSpec((9, LANE), lambda j: (0, j))
    dzv, dzg, dwv, dwg = _pcall(
        body, name="conv_gate_bwd", grid=(ft,), in_specs=list(_conv_specs(n, ft)) + [tile],
        out_specs=[tile, tile, wtile, wtile],
        out_shape=[jax.ShapeDtypeStruct((n, f2 // 2), z.dtype)] * 2 + [jax.ShapeDtypeStruct((9, f2 // 2), F32)] * 2,
        scratch_shapes=[pltpu.VMEM((geo.nbuf, LANE), F32)] * 4,
        compiler_params=_params(("parallel",)))(z, z, wc, wc, da)
    return jnp.concatenate([dzv, dzg], axis=1), jnp.concatenate([dwv, dwg], axis=1)


def _make_convgate(n_ctx, n_lat):
    @jax.custom_vjp
    def conv(z, wc):
        return _conv_fwd_call(z, wc, n_ctx, n_lat)

    def fwd(z, wc):
        return conv(z, wc), (z, wc)

    def bwd(res, da):
        return _conv_bwd_call(res[0], res[1], da, n_ctx, n_lat)

    conv.defvjp(fwd, bwd)
    return conv


GROUPS_PER_BLOCK = LANE // SSM_GROUP
STATE_BLOCK = GROUPS_PER_BLOCK * SSM_STATE
SCAN_LANES = 2 * LANE


def _cmul(ar, ai, br, bi):
    return ar * br - ai * bi, ar * bi + ai * br


def _lam_tables(lr, li, asc):
    row = lax.broadcasted_iota(jnp.int32, (SUBLANE, lr.shape[1]), 0)
    l1 = (jnp.broadcast_to(lr, row.shape), jnp.broadcast_to(li, row.shape))
    l2 = _cmul(*l1, *l1)
    l4 = _cmul(*l2, *l2)
    pw = l1
    pr = jnp.zeros(row.shape, F32)
    pi = jnp.zeros(row.shape, F32)
    for e in range(1, SUBLANE + 1):
        s = e - 1 if asc else SUBLANE - e
        pr = jnp.where(row == s, pw[0], pr)
        pi = jnp.where(row == s, pw[1], pi)
        pw = _cmul(*pw, *l1)
    return l1, l2, l4, (pr, pi)


def _tile_scan(br, bi, cr, ci, tables, asc):
    row = lax.broadcasted_iota(jnp.int32, br.shape, 0)
    hr, hi = br, bi
    for k, lam_k in zip((1, 2, 4), tables[:3]):
        if asc:
            keep = row >= k
            sr, si = pltpu.roll(hr, k, 0), pltpu.roll(hi, k, 0)
        else:
            keep = row < SUBLANE - k
            sr, si = pltpu.roll(hr, SUBLANE - k, 0), pltpu.roll(hi, SUBLANE - k, 0)
        mr, mi = _cmul(*lam_k, jnp.where(keep, sr, 0.0), jnp.where(keep, si, 0.0))
        hr, hi = hr + mr, hi + mi
    mr, mi = _cmul(*tables[3], jnp.broadcast_to(cr, br.shape), jnp.broadcast_to(ci, br.shape))
    hr, hi = hr + mr, hi + mi
    last = SUBLANE - 1 if asc else 0
    return hr, hi, hr[last:last + 1, :], hi[last:last + 1, :]


def _chunk_in_time_order(k, n_chunks, asc, adjoint):
    if asc:
        return n_chunks - 1 - k if adjoint else k
    if adjoint:
        return jnp.where(k == n_chunks - 1, 0, k + 1)
    return jnp.where(k == 0, 0, n_chunks - k)


def _dot(a, b):
    return jnp.dot(a, b, preferred_element_type=F32)


def _dot_nt(a, b):
    return lax.dot_general(a, b, (((1,), (1,)), ((), ())), preferred_element_type=F32)


def _scan_chunk(r_buf, i_buf, base, rows, carry, tables, asc, lam_grad=None):
    tiles = rows // SUBLANE
    out_carry, grads = [], []
    for h in range(STATE_BLOCK // SCAN_LANES):
        cols = slice(h * SCAN_LANES, (h + 1) * SCAN_LANES)

        def tile(kt, c, h=h, cols=cols):
            pt = kt if asc else tiles - 1 - kt
            t0 = pl.multiple_of(base + pt * SUBLANE, SUBLANE)
            sr, si, ncr, nci = _tile_scan(r_buf[pl.ds(t0, SUBLANE), cols], i_buf[pl.ds(t0, SUBLANE), cols],
                                          c[0], c[1], tables[h], asc)
            r_buf[pl.ds(t0, SUBLANE), cols] = sr
            i_buf[pl.ds(t0, SUBLANE), cols] = si
            if lam_grad is None:
                return ncr, nci
            h_r, h_i, h_base = lam_grad
            g0 = pl.multiple_of(h_base + pt * SUBLANE, SUBLANE)
            pr, pi = h_r[pl.ds(g0, SUBLANE), cols], h_i[pl.ds(g0, SUBLANE), cols]
            row = lax.broadcasted_iota(jnp.int32, sr.shape, 0)
            if asc:
                nr = jnp.where(row == 0, jnp.broadcast_to(c[0], sr.shape), pltpu.roll(sr, 1, 0))
                ni = jnp.where(row == 0, jnp.broadcast_to(c[1], sr.shape), pltpu.roll(si, 1, 0))
            else:
                nr = jnp.where(row == SUBLANE - 1, jnp.broadcast_to(c[0], sr.shape), pltpu.roll(sr, SUBLANE - 1, 0))
                ni = jnp.where(row == SUBLANE - 1, jnp.broadcast_to(c[1], sr.shape), pltpu.roll(si, SUBLANE - 1, 0))
            return ncr, nci, c[2] + nr * pr + ni * pi, c[3] + ni * pr - nr * pi

        init = (carry[2 * h], carry[2 * h + 1])
        if lam_grad is not None:
            zero = jnp.zeros((SUBLANE, SCAN_LANES), F32)
            init = init + (zero, zero)
        res = lax.fori_loop(0, tiles, tile, init, unroll=2)
        out_carry += [res[0], res[1]]
        grads.append(res[2:])
    return tuple(out_carry), grads


def _ssm_specs(n):
    tok = pl.BlockSpec((n, LANE), lambda q: (0, q))
    lam = pl.BlockSpec((1, STATE_BLOCK), lambda q: (0, q))
    w_in = pl.BlockSpec((None, LANE, STATE_BLOCK), lambda q: (q, 0, 0))
    w_out = pl.BlockSpec((None, STATE_BLOCK, LANE), lambda q: (q, 0, 0))
    return tok, lam, w_in, w_out


def _zero_carry():
    return tuple(jnp.zeros((1, SCAN_LANES), F32) for _ in range(2 * (STATE_BLOCK // SCAN_LANES)))


def _half_tables(lr_ref, li_ref, asc, conj):
    out = []
    for h in range(STATE_BLOCK // SCAN_LANES):
        cols = slice(h * SCAN_LANES, (h + 1) * SCAN_LANES)
        li = li_ref[:, cols]
        out.append(_lam_tables(lr_ref[:, cols], -li if conj else li, asc))
    return out


def _ssm_fwd_call(u, lr, li, b_re, b_im, c_re, c_imn, rc, asc):
    n, cs = u.shape
    nq, nchunks = cs // LANE, n // rc
    tok, lam, w_in, w_out = _ssm_specs(n)

    def body(u_ref, lr_ref, li_ref, br_ref, bi_ref, cr_ref, ci_ref, y_ref, h_r, h_i):
        tables = _half_tables(lr_ref, li_ref, asc, False)
        wbr, wbi = br_ref[...].astype(BF16), bi_ref[...].astype(BF16)
        wcr, wci = cr_ref[...].astype(BF16), ci_ref[...].astype(BF16)

        def chunk(k, carry):
            r0 = pl.multiple_of(_chunk_in_time_order(k, nchunks, asc, False) * rc, rc)
            ub = u_ref[pl.ds(r0, rc), :].astype(BF16)
            h_r[...] = _dot(ub, wbr)
            h_i[...] = _dot(ub, wbi)
            carry, _ = _scan_chunk(h_r, h_i, 0, rc, carry, tables, asc)
            y_ref[pl.ds(r0, rc), :] = _dot(h_r[...].astype(BF16), wcr) + _dot(h_i[...].astype(BF16), wci)
            return carry
        lax.fori_loop(0, nchunks, chunk, _zero_carry())

    return _pcall(body, name="s5_head", grid=(nq,), in_specs=[tok, lam, lam, w_in, w_in, w_out, w_out], out_specs=tok,
                  out_shape=jax.ShapeDtypeStruct((n, cs), F32),
                  scratch_shapes=[pltpu.VMEM((rc, STATE_BLOCK), F32)] * 2,
                  compiler_params=_params(("parallel",)))(u, lr, li, b_re, b_im, c_re, c_imn)


def _ssm_bwd_call(u, dy, lr, li, b_re, b_im, c_re, c_imn, rc, asc):
    n, cs = u.shape
    nq, nchunks = cs // LANE, n // rc
    tok, lam, w_in, w_out = _ssm_specs(n)

    def body(u_ref, dy_ref, lr_ref, li_ref, br_ref, bi_ref, cr_ref, ci_ref,
             du_ref, glr_ref, gli_ref, dbr_ref, dbi_ref, dcr_ref, dci_ref, h_r, h_i, a_r, a_i):
        wbr, wbi = br_ref[...].astype(BF16), bi_ref[...].astype(BF16)
        wcr, wci = cr_ref[...].astype(BF16), ci_ref[...].astype(BF16)

        tables = _half_tables(lr_ref, li_ref, asc, False)

        def chunk(k, carry):
            r0 = pl.multiple_of(_chunk_in_time_order(k, nchunks, asc, False) * rc, rc)
            ub = u_ref[pl.ds(r0, rc), :].astype(BF16)
            h_r[pl.ds(r0, rc), :] = _dot(ub, wbr)
            h_i[pl.ds(r0, rc), :] = _dot(ub, wbi)
            carry, _ = _scan_chunk(h_r, h_i, r0, rc, carry, tables, asc)
            return carry
        lax.fori_loop(0, nchunks, chunk, _zero_carry())

        adj = _half_tables(lr_ref, li_ref, not asc, True)
        for ref in (glr_ref, gli_ref, dbr_ref, dbi_ref, dcr_ref, dci_ref):
            ref[...] = jnp.zeros_like(ref)

        def chunk_adj(k, carry):
            r0 = pl.multiple_of(_chunk_in_time_order(k, nchunks, asc, True) * rc, rc)
            dyc = dy_ref[pl.ds(r0, rc), :]
            dyb = dyc.astype(BF16)
            a_r[...] = _dot_nt(dyb, wcr)
            a_i[...] = _dot_nt(dyb, wci)
            carry, grads = _scan_chunk(a_r, a_i, 0, rc, carry, adj, not asc, lam_grad=(h_r, h_i, r0))
            for h, (gr, gi) in enumerate(grads):
                cols = slice(h * SCAN_LANES, (h + 1) * SCAN_LANES)
                glr_ref[:, cols] += gr
                gli_ref[:, cols] += gi
            ab_r, ab_i = a_r[...].astype(BF16), a_i[...].astype(BF16)
            du_ref[pl.ds(r0, rc), :] = _dot_nt(ab_r, wbr) + _dot_nt(ab_i, wbi)
            ut = u_ref[pl.ds(r0, rc), :].T.astype(BF16)
            dbr_ref[...] += _dot(ut, ab_r)
            dbi_ref[...] += _dot(ut, ab_i)
            dyt = dyc.T.astype(BF16)
            dcr_ref[...] += _dot(dyt, h_r[pl.ds(r0, rc), :].astype(BF16))
            dci_ref[...] += _dot(dyt, h_i[pl.ds(r0, rc), :].astype(BF16))
            return carry
        lax.fori_loop(0, nchunks, chunk_adj, _zero_carry())

    part = pl.BlockSpec((SUBLANE, STATE_BLOCK), lambda q: (0, q))
    w_states = lr.shape[1]
    return _pcall(body, name="s5_head_bwd", grid=(nq,), in_specs=[tok, tok, lam, lam, w_in, w_in, w_out, w_out],
                  out_specs=[tok, part, part, w_in, w_in, w_in, w_in],
                  out_shape=[jax.ShapeDtypeStruct((n, cs), F32)] + [jax.ShapeDtypeStruct((SUBLANE, w_states), F32)] * 2
                  + [jax.ShapeDtypeStruct((nq, LANE, STATE_BLOCK), F32)] * 4,
                  scratch_shapes=[pltpu.VMEM((n, STATE_BLOCK), F32)] * 2 + [pltpu.VMEM((rc, STATE_BLOCK), F32)] * 2,
                  compiler_params=_params(("parallel",)))(u, dy, lr, li, b_re, b_im, c_re, c_imn)


def _make_ssm(rc, asc):
    @jax.custom_vjp
    def ssm(u, lr, li, b_re, b_im, c_re, c_imn):
        return _ssm_fwd_call(u, lr, li, b_re, b_im, c_re, c_imn, rc, asc)

    def fwd(*args):
        return ssm(*args), args

    def bwd(args, dy):
        du, glr, gli, dbr, dbi, dcr, dci = _ssm_bwd_call(args[0], dy, *args[1:], rc, asc)
        return (du, jnp.sum(glr, axis=0, keepdims=True), jnp.sum(gli, axis=0, keepdims=True), dbr, dbi,
                jnp.swapaxes(dcr, 1, 2), jnp.swapaxes(dci, 1, 2))

    ssm.defvjp(fwd, bwd)
    return ssm


def _blocks_in(bb):
    g, p, h = bb.shape
    k = GROUPS_PER_BLOCK
    out = jnp.einsum('qgph,gk->qghkp', bb.reshape(g // k, k, p, h), jnp.eye(k, dtype=F32))
    return out.reshape(g // k, k * h, k * p)


def _blocks_out(cc):
    g, h, p = cc.shape
    k = GROUPS_PER_BLOCK
    out = jnp.einsum('qghp,gk->qgpkh', cc.reshape(g // k, k, h, p), jnp.eye(k, dtype=F32))
    return out.reshape(g // k, k * p, k * h)


def _position():
    return lax.axis_index("x"), lax.axis_index("y"), lax.axis_index("c")


def _linear_index():
    x, y, c = _position()
    return 4 * x + 2 * y + c


COMM_SCRATCH = [pltpu.SemaphoreType.DMA((7,)), pltpu.SemaphoreType.DMA((7,)), pltpu.SemaphoreType.DMA(())]
ANY_SPEC = pl.BlockSpec(memory_space=pl.ANY)


def _gather_parts(x_ref, o_ref, send_sems, recv_sems, local_sem):
    x, y, c = _position()
    me, sibling = (x, y, c), (x, y, 1 - c)
    chips = [(1 - x, y), (x, 1 - y), (1 - x, 1 - y)]

    def block(px, py, pc):
        return o_ref.at[4 * px + 2 * py + pc]

    def copy(k, blk, to, src=None):
        return pltpu.make_async_remote_copy(
            src_ref=block(*blk) if src is None else src, dst_ref=block(*blk),
            send_sem=send_sems.at[k], recv_sem=recv_sems.at[k], device_id=to, device_id_type=MESH)

    mine = pltpu.make_async_copy(x_ref, block(*me), local_sem)
    first = [copy(0, me, sibling, src=x_ref)]
    first += [copy(1 + j, me, (*chip, c), src=x_ref) for j, chip in enumerate(chips)]
    passed = [copy(4 + j, (*chip, c), sibling) for j, chip in enumerate(chips)]

    def start():
        mine.start()
        for cp in first:
            cp.start()

    def finish():
        for j, chip in enumerate(chips):
            copy(1 + j, (*chip, c), me).wait_recv()
            passed[j].start()
        copy(0, sibling, me).wait_recv()
        for j, chip in enumerate(chips):
            copy(4 + j, (*chip, 1 - c), me).wait_recv()
        for cp in first + passed:
            cp.wait_send()
        mine.wait()

    return start, finish


def _scatter_parts(g_ref, o_ref, send_sems, recv_sems, local_sem):
    x, y, c = _position()
    mine = pltpu.make_async_copy(g_ref.at[4 * x + 2 * y + c], o_ref.at[0], local_sem)
    copies = []
    for k in range(1, N_DEV):
        px = 1 - x if k & 4 else x
        py = 1 - y if k & 2 else y
        pc = 1 - c if k & 1 else c
        copies.append(pltpu.make_async_remote_copy(
            src_ref=g_ref.at[4 * px + 2 * py + pc], dst_ref=o_ref.at[k],
            send_sem=send_sems.at[k - 1], recv_sem=recv_sems.at[k - 1],
            device_id=(px, py, pc), device_id_type=MESH))

    def start():
        mine.start()
        for cp in copies:
            cp.start()

    def finish():
        for cp in copies:
            cp.wait()
        mine.wait()

    return start, finish


def _comm_call(parts, src, out_shape, name):
    def body(s_ref, o_ref, *sems):
        start, finish = parts(s_ref, o_ref, *sems)
        start()
        finish()

    return _pcall(body, name=name, in_specs=[ANY_SPEC], out_specs=ANY_SPEC, out_shape=out_shape,
                  scratch_shapes=COMM_SCRATCH)(src)


def _ag_call(shard, name):
    return _comm_call(_gather_parts, shard, jax.ShapeDtypeStruct((N_DEV,) + shard.shape, shard.dtype), name)


def _rs_call(g, name):
    return _comm_call(_scatter_parts, g, jax.ShapeDtypeStruct(g.shape, g.dtype), name)


def _sum8_call(parts, name):
    _, r, c = parts.shape
    tr = _tile(r, max(PACK, (4 << 20) // (N_DEV * c * parts.dtype.itemsize) // PACK * PACK), PACK)

    def body(p_ref, o_ref):
        acc = p_ref[0].astype(F32)
        for k in range(1, N_DEV):
            acc = acc + p_ref[k].astype(F32)
        o_ref[...] = acc

    return _pcall(body, name=name, grid=(r // tr,),
                  in_specs=[pl.BlockSpec((N_DEV, tr, c), lambda i: (0, i, 0))],
                  out_specs=pl.BlockSpec((tr, c), lambda i: (i, 0)),
                  out_shape=jax.ShapeDtypeStruct((r, c), F32), compiler_params=_params(("parallel",)))(parts)


def _reduce_scatter(g, name):
    return _sum8_call(_rs_call(g, name), name + "_sum")


def _make_gather(dtype, name):
    @jax.custom_vjp
    def gather(shard):
        return _ag_call(shard.astype(dtype), name)

    def fwd(shard):
        return gather(shard), None

    def bwd(_, ct):
        return (_reduce_scatter(ct, name + "_rs"),)

    gather.defvjp(fwd, bwd)
    return gather


def _adam_call(w, g, m, v, name):
    r, c = w.shape
    tr = _tile(r, max(SUBLANE, (1 << 20) // (4 * c) // SUBLANE * SUBLANE), SUBLANE)

    def body(w_ref, g_ref, m_ref, v_ref, d_ref, mo_ref, vo_ref):
        gv = g_ref[...]
        m2 = ADAM_B1 * m_ref[...] + (1.0 - ADAM_B1) * gv
        v2 = ADAM_B2 * v_ref[...] + (1.0 - ADAM_B2) * (gv * gv)
        m_hat = m2 / (1.0 - ADAM_B1 ** ADAM_STEP)
        v_hat = v2 / (1.0 - ADAM_B2 ** ADAM_STEP)
        d_ref[...] = -ADAM_LR * (m_hat / (jnp.sqrt(v_hat) + ADAM_EPS) + ADAM_WD * w_ref[...])
        mo_ref[...] = m2
        vo_ref[...] = v2

    spec = pl.BlockSpec((tr, c), lambda i: (i, 0))
    return _pcall(body, name=name, grid=(r // tr,), in_specs=[spec] * 4, out_specs=[spec] * 3,
                  out_shape=[jax.ShapeDtypeStruct((r, c), F32)] * 3, compiler_params=_params(("parallel",)))(w, g, m, v)


def _loss_call(y, target, tr):
    n, d = y.shape

    def body(y_ref, t_ref, s_ref, dy_ref):
        i = pl.program_id(0)
        err = y_ref[...] - t_ref[...]
        dy_ref[...] = err * (1.0 / d)
        part = jnp.sum(jnp.sum(err * err, axis=1, keepdims=True), axis=0, keepdims=True)

        @pl.when(i == 0)
        def _():
            s_ref[...] = jnp.zeros_like(s_ref)
        s_ref[...] += part

    spec = pl.BlockSpec((tr, d), lambda i: (i, 0))
    return _pcall(body, name="loss_head", grid=(n // tr,), in_specs=[spec, spec],
                  out_specs=[pl.BlockSpec((1, 1), lambda i: (0, 0)), spec],
                  out_shape=[jax.ShapeDtypeStruct((1, 1), F32), jax.ShapeDtypeStruct((n, d), F32)],
                  compiler_params=_params(("arbitrary",)))(y, target)


def _pack(arrays, rows_mult):
    flat = jnp.concatenate([a.reshape(-1).astype(F32) for a in arrays])
    rows = -(-flat.shape[0] // LANE)
    rows = -(-rows // rows_mult) * rows_mult
    return jnp.pad(flat, (0, rows * LANE - flat.shape[0])).reshape(rows, LANE)


def _unpack(buf, shapes, lead=()):
    flat = buf.reshape(lead + (-1,))
    out, pos = [], 0
    for s in shapes:
        size = math.prod(s)
        out.append(flat[..., pos:pos + size].reshape(lead + tuple(s)))
        pos += size
    return out


def _s5_discretise(a_re, a_im, log_dt, b_re, b_im):
    dt = jnp.exp(log_dt)[:, None]
    mag = jnp.exp(a_re * dt)
    lam_re = mag * jnp.cos(a_im * dt)
    lam_im = mag * jnp.sin(a_im * dt)
    denom = a_re * a_re + a_im * a_im
    nr, ni = lam_re - 1.0, lam_im
    f_re = (nr * a_re + ni * a_im) / denom
    f_im = (ni * a_re - nr * a_im) / denom
    bb_re = f_re[..., None] * b_re - f_im[..., None] * b_im
    bb_im = f_re[..., None] * b_im + f_im[..., None] * b_re
    return lam_re, lam_im, bb_re, bb_im


def _forward(x, p, ctx, s_c, n_ctx, n_lat):
    d_model = x.shape[1]
    depth = len(p['w_in'])
    mix = p['w_in'][0].shape[1]
    pool_width = p['pool_scale'].shape[1]
    tr = n_ctx
    me = _linear_index()

    rmsmod = _make_rowop(_fn_rmsmod, 1, 2, 1, [BF16], tr, "rms_modulate")
    resid = _make_rowop(_fn_resid, 2, 1, 1, [F32], tr, "gated_residual")
    gelu_head = _make_rowop(_fn_gelu_head, 2, 0, 1, [F32], tr, "ssm_gelu")
    mixcat = _make_rowop(_fn_mixcat, 6, 0, 1, [BF16], tr, "mix_concat")
    pool = _make_pool(n_ctx, n_lat, pool_width, mix)
    convgate = _make_convgate(n_ctx, n_lat)
    ssm = [_make_ssm(tr, True), _make_ssm(tr, False)]
    lin_f32 = {k: _make_linear(F32, k) for k in ("w_in", "w_pool", "w_glu", "w_out", "w_down", "ada")}
    lin_up = _make_linear(BF16, "w_up")
    gather_big = {k: _make_gather(BF16, "gather_" + k) for k in BIG_SHARDED}
    gather_small = _make_gather(F32, "gather_small")
    gather_mod = _make_gather(F32, "gather_mod")

    whole = {k: gather_big[k](p[k][0]) for k in BIG_SHARDED}
    host = {k: _make_hosting_linear(BF16 if k == 'w_up' else F32, k) for k in BIG_SHARDED}

    def big_linear(k, a, l):
        w = whole[k] if k == 'w_up' else whole[k].reshape(-1, whole[k].shape[2])
        if l + 1 == depth:
            return (lin_up if k == 'w_up' else lin_f32[k])(a, w)
        y, whole[k] = host[k](a, w, p[k][l + 1])
        return y

    small = [p[k] for k in SMALL_SHARDED]
    packed = gather_small(_pack(small, PACK))
    w_pool, w_glu, w_conv = _unpack(packed, [a.shape for a in small], lead=(N_DEV,))
    w_pool = jnp.moveaxis(w_pool, 0, 2).reshape(depth, len(POOL_WINDOWS), -1, w_pool.shape[-1])
    w_glu = jnp.moveaxis(w_glu, 0, 1).reshape(depth, -1, w_glu.shape[-1])
    w_conv = jnp.moveaxis(w_conv, 0, 3).reshape(depth, 9, -1)

    s_rows = jnp.concatenate([s_c, jax.nn.silu(p['c_ctx'])[None, :],
                              jnp.zeros((PACK - N_DEV - 1, d_model), F32)], axis=0)
    cols = p['w_ada'][0].shape[1]
    b_loc = lax.dynamic_slice_in_dim(p['b_ada'], me * cols, cols, axis=1)
    mod_loc = jnp.stack([lin_f32['ada'](s_rows, p['w_ada'][l]) + b_loc[l][None, :] for l in range(depth)])
    mod = gather_mod(mod_loc.reshape(depth * PACK, cols)).reshape(N_DEV, depth, PACK, cols)
    mod = jnp.moveaxis(mod, 0, 2).reshape(depth, PACK, 6, d_model)
    mod_lat = lax.dynamic_index_in_dim(mod, me, axis=1, keepdims=False)
    mod_ctx = mod[:, N_DEV]

    xs = jnp.concatenate([ctx, x], axis=0)
    for l in range(depth):
        def seg(k):
            return jnp.stack([mod_ctx[l, k], mod_lat[l, k]]).reshape(2, 1, d_model)

        def row(name):
            return p[name][l].reshape(1, -1)

        h1, = rmsmod(xs, seg(0), seg(1), row('g_pre_mix'))
        u = big_linear('w_in', h1, l)
        pooled = pool(u)
        yp = [lin_f32['w_pool'](pooled[g], w_pool[l, g]) for g in range(len(POOL_WINDOWS))]
        u_ssm = u[:, pool_width:]
        ys = []
        for d in range(2):
            lam_re, lam_im, bb_re, bb_im = _s5_discretise(
                p['ssm_a_re'][l, d], p['ssm_a_im'][l, d], p['ssm_log_dt'][l, d], p['ssm_b_re'][l, d], p['ssm_b_im'][l, d])
            ys.append(ssm[d](u_ssm, lam_re.reshape(1, -1), lam_im.reshape(1, -1), _blocks_in(bb_re), _blocks_in(bb_im),
                             _blocks_out(p['ssm_c_re'][l, d]), _blocks_out(-p['ssm_c_im'][l, d])))
        gl, = gelu_head(ys[0] + ys[1], u_ssm, row('ssm_d'))
        s = lin_f32['w_glu'](gl, w_glu[l])
        cat, = mixcat(*yp, gl, s, row('pool_scale'))
        mixed = big_linear('w_out', cat, l)
        x1, = resid(xs, mixed, seg(2), row('g_post_mix'))
        h2, = rmsmod(x1, seg(3), seg(4), row('g_pre_ffn'))
        z = big_linear('w_up', h2, l)
        a = convgate(z, w_conv[l])
        f = big_linear('w_down', a, l)
        xs, = resid(x1, f, seg(5), row('g_post_ffn'))
    return xs[n_ctx:]


def _as_rows(a):
    return a.reshape(-1, a.shape[-1])


def kernel(x, c, ctx, c_ctx, w_ada, b_ada, w_in, w_pool, pool_scale, ssm_a_re, ssm_a_im, ssm_log_dt, ssm_b_re, ssm_b_im, ssm_c_re, ssm_c_im, ssm_d, w_glu, w_out, g_pre_mix, g_post_mix, g_pre_ffn, g_post_ffn, w_up, w_conv, w_down, loss_target, m_c_ctx, m_w_ada, m_b_ada, m_w_in, m_w_pool, m_pool_scale, m_ssm_a_re, m_ssm_a_im, m_ssm_log_dt, m_ssm_b_re, m_ssm_b_im, m_ssm_c_re, m_ssm_c_im, m_ssm_d, m_w_glu, m_w_out, m_g_pre_mix, m_g_post_mix, m_g_pre_ffn, m_g_post_ffn, m_w_up, m_w_conv, m_w_down, v_c_ctx, v_w_ada, v_b_ada, v_w_in, v_w_pool, v_pool_scale, v_ssm_a_re, v_ssm_a_im, v_ssm_log_dt, v_ssm_b_re, v_ssm_b_im, v_ssm_c_re, v_ssm_c_im, v_ssm_d, v_w_glu, v_w_out, v_g_pre_mix, v_g_post_mix, v_g_pre_ffn, v_g_post_ffn, v_w_up, v_w_conv, v_w_down):
    weights = dict(zip(WEIGHTS, (c_ctx, w_ada, b_ada, w_in, w_pool, pool_scale, ssm_a_re, ssm_a_im, ssm_log_dt, ssm_b_re,
                                 ssm_b_im, ssm_c_re, ssm_c_im, ssm_d, w_glu, w_out, g_pre_mix, g_post_mix, g_pre_ffn,
                                 g_post_ffn, w_up, w_conv, w_down)))
    m_in = dict(zip(WEIGHTS, (m_c_ctx, m_w_ada, m_b_ada, m_w_in, m_w_pool, m_pool_scale, m_ssm_a_re, m_ssm_a_im,
                              m_ssm_log_dt, m_ssm_b_re, m_ssm_b_im, m_ssm_c_re, m_ssm_c_im, m_ssm_d, m_w_glu, m_w_out,
                              m_g_pre_mix, m_g_post_mix, m_g_pre_ffn, m_g_post_ffn, m_w_up, m_w_conv, m_w_down)))
    v_in = dict(zip(WEIGHTS, (v_c_ctx, v_w_ada, v_b_ada, v_w_in, v_w_pool, v_pool_scale, v_ssm_a_re, v_ssm_a_im,
                              v_ssm_log_dt, v_ssm_b_re, v_ssm_b_im, v_ssm_c_re, v_ssm_c_im, v_ssm_d, v_w_glu, v_w_out,
                              v_g_pre_mix, v_g_post_mix, v_g_pre_ffn, v_g_post_ffn, v_w_up, v_w_conv, v_w_down)))
    depth = w_in.shape[0]
    n_lat, d_model = x.shape[1], x.shape[2]
    n_ctx = ctx.shape[1]
    per_layer = BIG_SHARDED + ['w_ada']

    c_rows = jnp.concatenate([c, jnp.zeros((SUBLANE - 1, d_model), F32)], axis=0)
    s_c = jax.nn.silu(_ag_call(c_rows, "gather_c")[:, 0, :])

    params = {k: ([w[l] for l in range(depth)] if k in per_layer else w) for k, w in weights.items()}

    def run(x2d, prm):
        return _forward(x2d, prm, ctx[0], s_c, n_ctx, n_lat)

    y, vjp = jax.vjp(run, x[0], params)
    sq, dy = _loss_call(y, loss_target[0], n_ctx)
    loss = lax.psum(0.5 * sq[0, 0] / d_model, ("x", "y", "c"))
    gx, grads = vjp(dy)
    grads = {k: (jnp.stack(g) if k in per_layer else g) for k, g in grads.items()}

    rep_shapes = [weights[k].shape for k in REPLICATED]
    contrib = _pack([grads[k] for k in REPLICATED], N_DEV * PACK)
    rows = contrib.shape[0] // N_DEV
    total = _ag_call(_reduce_scatter(contrib.reshape(N_DEV, rows, LANE), "reduce_replicated"), "gather_replicated")
    total = total.reshape(N_DEV * rows, LANE)
    for k, g in zip(REPLICATED, _unpack(total, rep_shapes)):
        grads[k] = g

    delta, new_m, new_v = {}, {}, {}
    rep = [_pack([src[k] for k in REPLICATED], N_DEV * PACK) for src in (weights, m_in, v_in)]
    upd = _adam_call(rep[0], total, rep[1], rep[2], "adamw_replicated")
    for out, buf in zip((delta, new_m, new_v), upd):
        out.update(zip(REPLICATED, _unpack(buf, rep_shapes)))
    for k in WEIGHTS:
        if k in REPLICATED:
            continue
        upd = _adam_call(_as_rows(weights[k]), _as_rows(grads[k]), _as_rows(m_in[k]), _as_rows(v_in[k]), "adamw_" + k)
        for out, buf in zip((delta, new_m, new_v), upd):
            out[k] = buf.reshape(weights[k].shape)

    return (loss, gx[None], *[grads[k] for k in WEIGHTS], *[delta[k] for k in WEIGHTS],
            *[new_m[k] for k in WEIGHTS], *[new_v[k] for k in WEIGHTS])
```

```python
import functools
import math

import jax
import jax.numpy as jnp
from jax import lax
from jax.experimental import pallas as pl
from jax.experimental.pallas import tpu as pltpu

F32 = jnp.float32
BF16 = jnp.bfloat16

N_DEV = 8
GRID_W = 64
POOL_WINDOWS = (2, 4, 8, 16)
SSM_GROUP = 16
SSM_STATE = 64
EPS = 1e-6
ADAM_LR = 0.001
ADAM_B1 = 0.9
ADAM_B2 = 0.999
ADAM_EPS = 1e-08
ADAM_WD = 0.01
ADAM_STEP = 10

SUBLANE = 8
PACK = 16
LANE = 128
VMEM_LIMIT = 56 * 1024 * 1024
MESH = pl.DeviceIdType.MESH

WEIGHTS = ['c_ctx', 'w_ada', 'b_ada', 'w_in', 'w_pool', 'pool_scale', 'ssm_a_re', 'ssm_a_im', 'ssm_log_dt',
           'ssm_b_re', 'ssm_b_im', 'ssm_c_re', 'ssm_c_im', 'ssm_d', 'w_glu', 'w_out', 'g_pre_mix',
           'g_post_mix', 'g_pre_ffn', 'g_post_ffn', 'w_up', 'w_conv', 'w_down']
REPLICATED = ['c_ctx', 'b_ada', 'pool_scale', 'ssm_a_re', 'ssm_a_im', 'ssm_log_dt', 'ssm_b_re', 'ssm_b_im',
              'ssm_c_re', 'ssm_c_im', 'ssm_d', 'g_pre_mix', 'g_post_mix', 'g_pre_ffn', 'g_post_ffn']
BIG_SHARDED = ['w_in', 'w_out', 'w_up', 'w_down']
SMALL_SHARDED = ['w_pool', 'w_glu', 'w_conv']


def _pcall(body, **kw):
    return pl.pallas_call(body, **kw)


def _params(sem=None):
    return pltpu.CompilerParams(dimension_semantics=sem, vmem_limit_bytes=VMEM_LIMIT)


def _tile(n, cap, mult):
    if n <= cap:
        return n
    best = None
    d = mult
    while d <= cap:
        if n % d == 0:
            best = d
        d += mult
    assert best is not None, (n, cap, mult)
    return best


def _sigmoid(x):
    return 1.0 / (1.0 + jnp.exp(-x))


def _mm(a, b, *, tb=False, out_dtype=F32, out_blocked=False, side=None, name):
    b_blocked = b.ndim == 3
    M, K = a.shape
    if b_blocked:
        nb, br, bc = b.shape
        N, Kb = (br, nb * bc) if tb else (nb * bc, br)
    else:
        N, Kb = (b.shape if tb else b.shape[::-1])
    assert K == Kb, (a.shape, b.shape, tb)
    tm = _tile(M, 1408, PACK)
    if out_blocked:
        assert N % N_DEV == 0
        tn = N // N_DEV
    elif b_blocked and not tb:
        tn = bc
    else:
        tn = _tile(N, 512, LANE)
    if b_blocked and not tb:
        assert tn == bc
    if b_blocked and tb:
        tk = bc
    else:
        tk = _tile(K, 2304, LANE)
    nm, nn, nk = M // tm, N // tn, K // tk

    a_spec = pl.BlockSpec((tm, tk), lambda i, j, k: (i, k))
    if b_blocked:
        if tb:
            b_spec = pl.BlockSpec((None, tn, tk), lambda i, j, k: (k, j, 0))
        else:
            b_spec = pl.BlockSpec((None, tk, tn), lambda i, j, k: (j, k, 0))
    else:
        b_spec = pl.BlockSpec((tn, tk), lambda i, j, k: (j, k)) if tb else pl.BlockSpec((tk, tn), lambda i, j, k: (k, j))
    if out_blocked:
        o_spec = pl.BlockSpec((None, tm, tn), lambda i, j, k: (j, i, 0))
        o_shape = jax.ShapeDtypeStruct((N_DEV, M, tn), out_dtype)
    else:
        o_spec = pl.BlockSpec((tm, tn), lambda i, j, k: (i, j))
        o_shape = jax.ShapeDtypeStruct((M, N), out_dtype)
    dims = (((1,), ((1 if tb else 0),)), ((), ()))

    def matmul(a_ref, b_ref, o_ref, acc_ref):
        k = pl.program_id(2)
        part = lax.dot_general(a_ref[...].astype(BF16), b_ref[...].astype(BF16), dims, preferred_element_type=F32)
        if nk == 1:
            o_ref[...] = part.astype(o_ref.dtype)
        else:
            @pl.when(k == 0)
            def _():
                acc_ref[...] = part

            @pl.when(k > 0)
            def _():
                acc_ref[...] += part

            @pl.when(k == nk - 1)
            def _():
                o_ref[...] = acc_ref[...].astype(o_ref.dtype)

    acc = [pltpu.VMEM((tm, tn), F32)]
    if side is None:
        return _pcall(matmul, name=name, grid=(nm, nn, nk), in_specs=[a_spec, b_spec], out_specs=o_spec,
                      out_shape=o_shape, scratch_shapes=acc,
                      compiler_params=_params(("parallel", "parallel", "arbitrary")))(a, b)

    parts, src, side_shape = side

    def body(a_ref, b_ref, s_ref, o_ref, so_ref, acc_ref, *sems):
        step = (pl.program_id(0) * nn + pl.program_id(1)) * nk + pl.program_id(2)
        start, finish = parts(s_ref, so_ref, *sems)
        pl.when(step == 0)(start)
        matmul(a_ref, b_ref, o_ref, acc_ref)
        pl.when(step == nm * nn * nk - 1)(finish)

    return _pcall(body, name=name, grid=(nm, nn, nk), in_specs=[a_spec, b_spec, ANY_SPEC],
                  out_specs=[o_spec, ANY_SPEC], out_shape=[o_shape, side_shape], scratch_shapes=acc + COMM_SCRATCH,
                  compiler_params=_params(("arbitrary", "arbitrary", "arbitrary")))(a, b, src)


def _make_linear(out_dtype, name):
    @jax.custom_vjp
    def lin(a, w):
        return _mm(a, w, out_dtype=out_dtype, name=name)

    def fwd(a, w):
        return lin(a, w), (a, w)

    def bwd(res, dy):
        a, w = res
        da = _mm(dy, w, tb=True, out_dtype=a.dtype, name=name + "_da")
        dw = _mm(a.T, dy, out_dtype=w.dtype, out_blocked=(w.ndim == 3), name=name + "_dw")
        return da, dw

    lin.defvjp(fwd, bwd)
    return lin


def _make_hosting_linear(out_dtype, name):
    @jax.custom_vjp
    def lin(a, w, next_shard):
        nxt = next_shard.astype(BF16)
        side = (_gather_parts, nxt, jax.ShapeDtypeStruct((N_DEV,) + nxt.shape, BF16))
        return tuple(_mm(a, w, out_dtype=out_dtype, side=side, name=name + "_gather"))

    def fwd(a, w, next_shard):
        return lin(a, w, next_shard), (a, w)

    def bwd(res, cts):
        a, w = res
        dy, d_next = cts
        nb, r, c = d_next.shape
        half = r // 2

        def side(first, count):
            return (functools.partial(_scatter_parts, rows=(first, count)), d_next,
                    jax.ShapeDtypeStruct((nb, count, c), d_next.dtype))

        dw, lo = _mm(a.T, dy, out_dtype=w.dtype, out_blocked=(w.ndim == 3), side=side(0, half),
                     name=name + "_dw_scatter")
        da, hi = _mm(dy, w, tb=True, out_dtype=a.dtype, side=side(half, r - half), name=name + "_da_scatter")
        d_shard = jnp.concatenate([_sum8_call(lo, name + "_sum_lo"), _sum8_call(hi, name + "_sum_hi")], axis=0)
        return da, dw, d_shard

    lin.defvjp(fwd, bwd)
    return lin


def _make_rowop(fn, n_row, n_seg, n_bc, out_dtypes, tr, name):
    def specs(args):
        rows, segs, bcs = args[:n_row], args[n_row:n_row + n_seg], args[n_row + n_seg:]
        sp = [pl.BlockSpec((tr, r.shape[1]), lambda i: (i, 0)) for r in rows]
        sp += [pl.BlockSpec((None, 1, s.shape[2]), lambda i: (jnp.minimum(i, 1), 0, 0)) for s in segs]
        sp += [pl.BlockSpec((1, b.shape[1]), lambda i: (0, 0)) for b in bcs]
        return sp

    def out_widths(args):
        tiles = [jax.ShapeDtypeStruct((tr, a.shape[-1]), a.dtype) for a in args[:n_row]]
        tiles += [jax.ShapeDtypeStruct((1, a.shape[-1]), a.dtype) for a in args[n_row:]]
        return [o.shape[1] for o in jax.eval_shape(fn, *tiles)]

    def fwd_call(*args):
        n = args[0].shape[0]
        widths = out_widths(args)
        n_in = len(args)

        def body(*refs):
            vals = [r[...] for r in refs[:n_in]]
            outs = fn(*vals)
            for o_ref, o in zip(refs[n_in:], outs):
                o_ref[...] = o.astype(o_ref.dtype)

        return _pcall(body, name=name, grid=(n // tr,), in_specs=specs(args),
                      out_specs=[pl.BlockSpec((tr, w), lambda i: (i, 0)) for w in widths],
                      out_shape=[jax.ShapeDtypeStruct((n, w), d) for w, d in zip(widths, out_dtypes)],
                      compiler_params=_params(("parallel",)))(*args)

    def bwd_call(args, cots):
        n = args[0].shape[0]
        n_in = len(args)
        n_ct = len(cots)
        rows, segs, bcs = args[:n_row], args[n_row:n_row + n_seg], args[n_row + n_seg:]

        def body(*refs):
            i = pl.program_id(0)
            vals = [r[...] for r in refs[:n_in]]
            cts = [r[...].astype(F32) for r in refs[n_in:n_in + n_ct]]
            outs = refs[n_in + n_ct:]
            _, vjp = jax.vjp(lambda *v: tuple(fn(*v)), *vals)
            grads = vjp(tuple(cts))
            for o_ref, g in zip(outs[:n_row], grads[:n_row]):
                o_ref[...] = g.astype(o_ref.dtype)
            for o_ref, g in zip(outs[n_row:n_row + n_seg], grads[n_row:n_row + n_seg]):
                @pl.when(i <= 1)
                def _():
                    o_ref[...] = jnp.zeros_like(o_ref)
                o_ref[...] += g.astype(F32)
            for o_ref, g in zip(outs[n_row + n_seg:], grads[n_row + n_seg:]):
                @pl.when(i == 0)
                def _():
                    o_ref[...] = jnp.zeros_like(o_ref)
                o_ref[...] += g.astype(F32)

        out_specs = [pl.BlockSpec((tr, r.shape[1]), lambda i: (i, 0)) for r in rows]
        out_specs += [pl.BlockSpec((None, 1, s.shape[2]), lambda i: (jnp.minimum(i, 1), 0, 0)) for s in segs]
        out_specs += [pl.BlockSpec((1, b.shape[1]), lambda i: (0, 0)) for b in bcs]
        out_shape = [jax.ShapeDtypeStruct(r.shape, r.dtype) for r in rows]
        out_shape += [jax.ShapeDtypeStruct(s.shape, F32) for s in segs]
        out_shape += [jax.ShapeDtypeStruct(b.shape, F32) for b in bcs]
        in_specs = specs(args) + [pl.BlockSpec((tr, c.shape[1]), lambda i: (i, 0)) for c in cots]
        return _pcall(body, name=name + "_bwd", grid=(n // tr,), in_specs=in_specs, out_specs=out_specs,
                      out_shape=out_shape, compiler_params=_params(("arbitrary",)))(*args, *cots)

    @jax.custom_vjp
    def op(*args):
        return tuple(fwd_call(*args))

    def op_fwd(*args):
        return op(*args), args

    def op_bwd(args, cots):
        return tuple(bwd_call(args, list(cots)))

    op.defvjp(op_fwd, op_bwd)
    return op


def _rms(x, g):
    return x * lax.rsqrt(jnp.mean(x * x, axis=-1, keepdims=True) + EPS) * g


def _fn_rmsmod(x, shift, scale, g):
    return (_rms(x.astype(F32), g) * (1.0 + scale) + shift,)


def _fn_resid(x, m, gate, g):
    return (x + gate * _rms(m.astype(F32), g),)


def _fn_gelu_head(y, u, d):
    return (jax.nn.gelu(y + d * u.astype(F32)),)


def _fn_mixcat(y0, y1, y2, y3, gl, s, ps):
    pool = jnp.concatenate([y0, y1, y2, y3], axis=1) * ps
    return (jnp.concatenate([pool, gl * _sigmoid(s)], axis=1),)


def _pool_call(x, col_blk0, ncol, w, n_ctx, n_lat, transpose, out_dtype, name):
    n = n_ctx + n_lat
    cw = LANE
    r = n_ctx
    gap = SUBLANE
    half = w // 2
    lat0 = 2 * gap + n_ctx
    nbuf = 3 * gap + n

    def body(x_ref, o_ref, buf):
        def inv_cnt(seg_len, t0):
            t = t0 + lax.broadcasted_iota(jnp.int32, (r, 1), 0)
            cnt = jnp.minimum(t + half, seg_len) - jnp.maximum(t - half, 0)
            return 1.0 / cnt.astype(F32)

        zero = jnp.zeros((gap, cw), F32)
        buf[0:gap, :] = zero
        buf[gap + n_ctx:lat0, :] = zero
        buf[lat0 + n_lat:nbuf, :] = zero

        def fill(src0, dst0, seg_len, t0):
            v = x_ref[pl.ds(src0, r), :].astype(F32)
            if transpose:
                v = v * inv_cnt(seg_len, t0)
            buf[pl.ds(dst0, r), :] = v

        def compute(src0, dst0, seg_len, t0):
            win = buf[pl.ds(dst0 - gap, r + 2 * gap), :]
            nw = r + 2 * gap
            s = win + pltpu.roll(win, (nw - 1) if transpose else 1, 0)
            for sh in (1, 2, 4):
                if w >= 4 * sh:
                    s = pltpu.roll(s, sh, 0) + pltpu.roll(s, nw - sh, 0)
            ws = s[gap:gap + r]
            if transpose:
                out = ws - x_ref[pl.ds(src0, r), :].astype(F32)
            else:
                out = ws * inv_cnt(seg_len, t0) - win[gap:gap + r]
            o_ref[pl.ds(src0, r), :] = out.astype(o_ref.dtype)

        for step in (fill, compute):
            step(0, gap, n_ctx, 0)

            def lat(i, c, step=step):
                off = pl.multiple_of(i * r, r)
                step(n_ctx + off, lat0 + off, n_lat, off)
                return c
            lax.fori_loop(0, n_lat // r, lat, 0)

    return _pcall(body, name=name, grid=(ncol,),
                  in_specs=[pl.BlockSpec((n, cw), lambda j: (0, col_blk0 + j))],
                  out_specs=pl.BlockSpec((n, cw), lambda j: (0, j)),
                  out_shape=jax.ShapeDtypeStruct((n, ncol * cw), out_dtype),
                  scratch_shapes=[pltpu.VMEM((nbuf, cw), F32)],
                  compiler_params=_params(("parallel",)))(x)


def _make_pool(n_ctx, n_lat, pool_width, mix_width):
    ncol = pool_width // len(POOL_WINDOWS) // LANE

    @jax.custom_vjp
    def pool(u):
        return tuple(_pool_call(u, g * ncol, ncol, w, n_ctx, n_lat, False, BF16, "pool_w%d" % w)
                     for g, w in enumerate(POOL_WINDOWS))

    def fwd(u):
        return pool(u), None

    def bwd(_, cots):
        parts = [_pool_call(ct, 0, ncol, w, n_ctx, n_lat, True, F32, "pool_w%d_bwd" % w)
                 for ct, w in zip(cots, POOL_WINDOWS)]
        parts.append(jnp.zeros((n_ctx + n_lat, mix_width - pool_width), F32))
        return (jnp.concatenate(parts, axis=1),)

    pool.defvjp(fwd, bwd)
    return pool


class _ConvGeom:
    def __init__(self, n_ctx, n_lat):
        self.n_ctx, self.n_lat = n_ctx, n_lat
        self.gap = GRID_W + SUBLANE
        self.r = 2 * GRID_W
        self.ctx0 = self.gap
        self.lat0 = 2 * self.gap + n_ctx
        self.nbuf = 3 * self.gap + n_ctx + n_lat
        self.nwin = self.r + 2 * self.gap
        self.n16 = self.r + 2 * SUBLANE
        assert n_lat % self.r == 0 and n_ctx % self.r == 0

    def zero_gaps(self, buf):
        z = jnp.zeros((self.gap, LANE), F32)
        buf[0:self.gap, :] = z
        buf[self.ctx0 + self.n_ctx:self.lat0, :] = z
        buf[self.lat0 + self.n_lat:self.nbuf, :] = z

    def fill(self, src_ref, buf):
        r = self.r

        def seg(src0, dst0, count):
            def one(i, c):
                off = pl.multiple_of(i * r, r)
                buf[pl.ds(dst0 + off, r), :] = src_ref[pl.ds(src0 + off, r), :].astype(F32)
                return c
            lax.fori_loop(0, count, one, 0)
        seg(0, self.ctx0, self.n_ctx // r)
        seg(self.n_ctx, self.lat0, self.n_lat // r)

    def col_masks(self, nrows, first_col):
        col = (lax.broadcasted_iota(jnp.int32, (nrows, 1), 0) + first_col) & (GRID_W - 1)
        return col == GRID_W - 1, col == 0

    def lat_window(self, buf, i):
        ws = pl.multiple_of(self.lat0 - self.gap + i * self.r, SUBLANE)
        return buf[pl.ds(ws, self.nwin), :]

    def ctx_window(self, buf):
        return buf[self.ctx0 - SUBLANE:self.ctx0 + self.n_ctx + SUBLANE, :]

    def lat_sources(self, win):
        last, first = self.col_masks(self.nwin, GRID_W - SUBLANE)
        return jnp.where(last, 0.0, win), win, jnp.where(first, 0.0, win)

    def row_slice(self, x, di, sign):
        st = self.gap - SUBLANE + sign * (di - 1) * GRID_W
        return x[st:st + self.n16]

    def lat_conv(self, win, w):
        srcs = self.lat_sources(win)
        cs = []
        for dj in range(3):
            acc = None
            for di in range(3):
                term = w[di * 3 + dj] * self.row_slice(srcs[dj], di, 1)
                acc = term if acc is None else acc + term
            cs.append(acc)
        out = pltpu.roll(cs[0], 1, 0) + cs[1] + pltpu.roll(cs[2], self.n16 - 1, 0)
        return out[SUBLANE:SUBLANE + self.r]

    def ctx_conv(self, win, w, transpose=False):
        n = win.shape[0]
        lo, hi = (w[5], w[3]) if transpose else (w[3], w[5])
        out = lo * pltpu.roll(win, 1, 0) + w[4] * win + hi * pltpu.roll(win, n - 1, 0)
        return out[SUBLANE:SUBLANE + self.n_ctx]

    def lat_conv_t(self, dwin, w):
        es = []
        for dj in range(3):
            acc = None
            for di in range(3):
                term = w[di * 3 + dj] * self.row_slice(dwin, di, -1)
                acc = term if acc is None else acc + term
            es.append(acc)
        last, first = self.col_masks(self.n16, GRID_W - SUBLANE)
        out = (jnp.where(last, 0.0, pltpu.roll(es[0], self.n16 - 1, 0)) + es[1]
               + jnp.where(first, 0.0, pltpu.roll(es[2], 1, 0)))
        return out[SUBLANE:SUBLANE + self.r]


def _taps(w_ref):
    return [w_ref[k:k + 1, :] for k in range(9)]


def _conv_specs(n, f_tiles):
    zv = pl.BlockSpec((n, LANE), lambda j: (0, j))
    zg = pl.BlockSpec((n, LANE), lambda j: (0, j + f_tiles))
    wv = pl.BlockSpec((9, LANE), lambda j: (0, j))
    wg = pl.BlockSpec((9, LANE), lambda j: (0, j + f_tiles))
    return zv, zg, wv, wg


def _conv_fwd_call(z, wc, n_ctx, n_lat):
    n, f2 = z.shape
    ft = f2 // 2 // LANE
    geo = _ConvGeom(n_ctx, n_lat)

    def body(zv_ref, zg_ref, wv_ref, wg_ref, o_ref, cv_ref, cg_ref, bv, bg):
        wv, wg = _taps(wv_ref), _taps(wg_ref)
        for src, buf in ((zv_ref, bv), (zg_ref, bg)):
            geo.zero_gaps(buf)
            geo.fill(src, buf)

        def emit(rows, cv, cg):
            o_ref[rows, :] = (cv * cg * _sigmoid(cg)).astype(o_ref.dtype)
            cv_ref[rows, :] = cv.astype(cv_ref.dtype)
            cg_ref[rows, :] = cg.astype(cg_ref.dtype)

        emit(slice(0, n_ctx), geo.ctx_conv(geo.ctx_window(bv), wv), geo.ctx_conv(geo.ctx_window(bg), wg))

        def chunk(i, c):
            off = pl.multiple_of(n_ctx + i * geo.r, SUBLANE)
            emit(pl.ds(off, geo.r), geo.lat_conv(geo.lat_window(bv, i), wv), geo.lat_conv(geo.lat_window(bg, i), wg))
            return c
        lax.fori_loop(0, n_lat // geo.r, chunk, 0)

    tile = pl.BlockSpec((n, LANE), lambda j: (0, j))
    return _pcall(body, name="conv_gate", grid=(ft,), in_specs=list(_conv_specs(n, ft)), out_specs=[tile] * 3,
                  out_shape=[jax.ShapeDtypeStruct((n, f2 // 2), BF16)] * 3,
                  scratch_shapes=[pltpu.VMEM((geo.nbuf, LANE), F32)] * 2,
                  compiler_params=_params(("parallel",)))(z, z, wc, wc)


def _conv_bwd_call(z, wc, cv, cg, da, n_ctx, n_lat):
    n, f2 = z.shape
    ft = f2 // 2 // LANE
    geo = _ConvGeom(n_ctx, n_lat)
    r, n16 = geo.r, geo.n16

    def body(zv_ref, zg_ref, wv_ref, wg_ref, cv_ref, cg_ref, da_ref, dzv_ref, dzg_ref, dwv_ref, dwg_ref,
             bv, bg, dv, dg):
        wv, wg = _taps(wv_ref), _taps(wg_ref)
        for buf in (bv, bg, dv, dg):
            geo.zero_gaps(buf)
        geo.fill(zv_ref, bv)
        geo.fill(zg_ref, bg)

        def gate_grads(cv, cg, d):
            sg = _sigmoid(cg)
            return d * cg * sg, d * cv * sg * (1.0 + cg * (1.0 - sg))

        def tap_sums(d_c, srcs, pad):
            zeros = jnp.zeros((SUBLANE, LANE), F32)
            dce = jnp.concatenate([zeros, d_c, zeros], axis=0)
            m = dce.shape[0]
            shifted = (pltpu.roll(dce, m - 1, 0), dce, pltpu.roll(dce, 1, 0))
            out = []
            for di in range(3):
                for dj in range(3):
                    src = srcs[dj] if pad is None else geo.row_slice(srcs[dj], di, 1)
                    out.append(jnp.sum(shifted[dj] * src, axis=0, keepdims=True))
            return out

        winv, wing = geo.ctx_window(bv), geo.ctx_window(bg)
        d_cv, d_cg = gate_grads(cv_ref[0:n_ctx, :].astype(F32), cg_ref[0:n_ctx, :].astype(F32),
                                da_ref[0:n_ctx, :].astype(F32))
        dv[geo.ctx0:geo.ctx0 + n_ctx, :] = d_cv
        dg[geo.ctx0:geo.ctx0 + n_ctx, :] = d_cg
        zero_row = jnp.zeros((1, LANE), F32)
        acc0 = []
        for d_c, win in ((d_cv, winv), (d_cg, wing)):
            sums = tap_sums(d_c, (win, win, win), None)
            acc0 += [zero_row] * 3 + sums[3:6] + [zero_row] * 3

        def chunk(i, acc):
            winv, wing = geo.lat_window(bv, i), geo.lat_window(bg, i)
            off = pl.multiple_of(n_ctx + i * r, SUBLANE)
            d_cv, d_cg = gate_grads(cv_ref[pl.ds(off, r), :].astype(F32), cg_ref[pl.ds(off, r), :].astype(F32),
                                    da_ref[pl.ds(off, r), :].astype(F32))
            dst = pl.multiple_of(geo.lat0 + i * r, SUBLANE)
            dv[pl.ds(dst, r), :] = d_cv
            dg[pl.ds(dst, r), :] = d_cg
            sums = tap_sums(d_cv, geo.lat_sources(winv), True) + tap_sums(d_cg, geo.lat_sources(wing), True)
            return tuple(a + s for a, s in zip(acc, sums))
        acc = lax.fori_loop(0, n_lat // r, chunk, tuple(acc0))
        for k in range(9):
            dwv_ref[k:k + 1, :] = acc[k]
            dwg_ref[k:k + 1, :] = acc[9 + k]

        for dbuf, w, dz_ref in ((dv, wv, dzv_ref), (dg, wg, dzg_ref)):
            dz_ref[0:n_ctx, :] = geo.ctx_conv(geo.ctx_window(dbuf), w, transpose=True).astype(dz_ref.dtype)

            def chunk_t(i, c, dbuf=dbuf, w=w, dz_ref=dz_ref):
                off = pl.multiple_of(n_ctx + i * r, SUBLANE)
                dz_ref[pl.ds(off, r), :] = geo.lat_conv_t(geo.lat_window(dbuf, i), w).astype(dz_ref.dtype)
                return c
            lax.fori_loop(0, n_lat // r, chunk_t, 0)

    tile = pl.BlockSpec((n, LANE), lambda j: (0, j))
    wtile = pl.BlockSpec((9, LANE), lambda j: (0, j))
    dzv, dzg, dwv, dwg = _pcall(
        body, name="conv_gate_bwd", grid=(ft,), in_specs=list(_conv_specs(n, ft)) + [tile] * 3,
        out_specs=[tile, tile, wtile, wtile],
        out_shape=[jax.ShapeDtypeStruct((n, f2 // 2), z.dtype)] * 2 + [jax.ShapeDtypeStruct((9, f2 // 2), F32)] * 2,
        scratch_shapes=[pltpu.VMEM((geo.nbuf, LANE), F32)] * 4,
        compiler_params=_params(("parallel",)))(z, z, wc, wc, cv, cg, da)
    return jnp.concatenate([dzv, dzg], axis=1), jnp.concatenate([dwv, dwg], axis=1)


def _make_convgate(n_ctx, n_lat):
    @jax.custom_vjp
    def conv(z, wc):
        return _conv_fwd_call(z, wc, n_ctx, n_lat)[0]

    def fwd(z, wc):
        a, cv, cg = _conv_fwd_call(z, wc, n_ctx, n_lat)
        return a, (z, wc, cv, cg)

    def bwd(res, da):
        return _conv_bwd_call(*res, da, n_ctx, n_lat)

    conv.defvjp(fwd, bwd)
    return conv


GROUPS_PER_BLOCK = LANE // SSM_GROUP
STATE_BLOCK = GROUPS_PER_BLOCK * SSM_STATE
SCAN_LANES = 2 * LANE


def _cmul(ar, ai, br, bi):
    return ar * br - ai * bi, ar * bi + ai * br


def _lam_tables(lr, li, asc):
    row = lax.broadcasted_iota(jnp.int32, (SUBLANE, lr.shape[1]), 0)
    l1 = (jnp.broadcast_to(lr, row.shape), jnp.broadcast_to(li, row.shape))
    l2 = _cmul(*l1, *l1)
    l4 = _cmul(*l2, *l2)
    pw = l1
    pr = jnp.zeros(row.shape, F32)
    pi = jnp.zeros(row.shape, F32)
    for e in range(1, SUBLANE + 1):
        s = e - 1 if asc else SUBLANE - e
        pr = jnp.where(row == s, pw[0], pr)
        pi = jnp.where(row == s, pw[1], pi)
        pw = _cmul(*pw, *l1)

    def masked(lam_k, k):
        keep = (row >= k) if asc else (row < SUBLANE - k)
        return jnp.where(keep, lam_k[0], 0.0), jnp.where(keep, lam_k[1], 0.0)

    return masked(l1, 1), masked(l2, 2), masked(l4, 4), (pr, pi)


def _tile_scan(br, bi, cr, ci, tables, asc):
    hr, hi = br, bi
    for k, lam_k in zip((1, 2, 4), tables[:3]):
        shift = k if asc else SUBLANE - k
        mr, mi = _cmul(*lam_k, pltpu.roll(hr, shift, 0), pltpu.roll(hi, shift, 0))
        hr, hi = hr + mr, hi + mi
    mr, mi = _cmul(*tables[3], jnp.broadcast_to(cr, br.shape), jnp.broadcast_to(ci, br.shape))
    hr, hi = hr + mr, hi + mi
    last = SUBLANE - 1 if asc else 0
    return hr, hi, hr[last:last + 1, :], hi[last:last + 1, :]


def _chunk_in_time_order(k, n_chunks, asc, adjoint):
    if asc:
        return n_chunks - 1 - k if adjoint else k
    if adjoint:
        return jnp.where(k == n_chunks - 1, 0, k + 1)
    return jnp.where(k == 0, 0, n_chunks - k)


def _dot(a, b):
    return jnp.dot(a, b, preferred_element_type=F32)


def _dot_nt(a, b):
    return lax.dot_general(a, b, (((1,), (1,)), ((), ())), preferred_element_type=F32)


def _scan_chunk(r_buf, i_buf, base, rows, carry, tables, asc, lam_grad=None):
    tiles = rows // SUBLANE
    out_carry, grads = [], []
    for h in range(STATE_BLOCK // SCAN_LANES):
        cols = slice(h * SCAN_LANES, (h + 1) * SCAN_LANES)

        def tile(kt, c, h=h, cols=cols):
            pt = kt if asc else tiles - 1 - kt
            t0 = pl.multiple_of(base + pt * SUBLANE, SUBLANE)
            sr, si, ncr, nci = _tile_scan(r_buf[pl.ds(t0, SUBLANE), cols], i_buf[pl.ds(t0, SUBLANE), cols],
                                          c[0], c[1], tables[h], asc)
            r_buf[pl.ds(t0, SUBLANE), cols] = sr
            i_buf[pl.ds(t0, SUBLANE), cols] = si
            if lam_grad is None:
                return ncr, nci
            h_r, h_i, h_base = lam_grad
            g0 = pl.multiple_of(h_base + pt * SUBLANE, SUBLANE)
            pr, pi = h_r[pl.ds(g0, SUBLANE), cols], h_i[pl.ds(g0, SUBLANE), cols]
            row = lax.broadcasted_iota(jnp.int32, sr.shape, 0)
            if asc:
                nr = jnp.where(row == 0, jnp.broadcast_to(c[0], sr.shape), pltpu.roll(sr, 1, 0))
                ni = jnp.where(row == 0, jnp.broadcast_to(c[1], sr.shape), pltpu.roll(si, 1, 0))
            else:
                nr = jnp.where(row == SUBLANE - 1, jnp.broadcast_to(c[0], sr.shape), pltpu.roll(sr, SUBLANE - 1, 0))
                ni = jnp.where(row == SUBLANE - 1, jnp.broadcast_to(c[1], sr.shape), pltpu.roll(si, SUBLANE - 1, 0))
            return ncr, nci, c[2] + nr * pr + ni * pi, c[3] + ni * pr - nr * pi

        init = (carry[2 * h], carry[2 * h + 1])
        if lam_grad is not None:
            zero = jnp.zeros((SUBLANE, SCAN_LANES), F32)
            init = init + (zero, zero)
        res = lax.fori_loop(0, tiles, tile, init, unroll=2)
        out_carry += [res[0], res[1]]
        grads.append(res[2:])
    return tuple(out_carry), grads


def _ssm_specs(n):
    tok = pl.BlockSpec((n, LANE), lambda q: (0, q))
    lam = pl.BlockSpec((1, STATE_BLOCK), lambda q: (0, q))
    w_in = pl.BlockSpec((None, LANE, STATE_BLOCK), lambda q: (q, 0, 0))
    w_out = pl.BlockSpec((None, STATE_BLOCK, LANE), lambda q: (q, 0, 0))
    return tok, lam, w_in, w_out


def _zero_carry():
    return tuple(jnp.zeros((1, SCAN_LANES), F32) for _ in range(2 * (STATE_BLOCK // SCAN_LANES)))


def _half_tables(lr_ref, li_ref, asc, conj):
    out = []
    for h in range(STATE_BLOCK // SCAN_LANES):
        cols = slice(h * SCAN_LANES, (h + 1) * SCAN_LANES)
        li = li_ref[:, cols]
        out.append(_lam_tables(lr_ref[:, cols], -li if conj else li, asc))
    return out


def _ssm_fwd_call(u, lr, li, b_re, b_im, c_re, c_imn, rc, asc):
    n, cs = u.shape
    nq, nchunks = cs // LANE, n // rc
    tok, lam, w_in, w_out = _ssm_specs(n)

    def body(u_ref, lr_ref, li_ref, br_ref, bi_ref, cr_ref, ci_ref, y_ref, h_r, h_i):
        tables = _half_tables(lr_ref, li_ref, asc, False)
        wbr, wbi = br_ref[...].astype(BF16), bi_ref[...].astype(BF16)
        wcr, wci = cr_ref[...].astype(BF16), ci_ref[...].astype(BF16)

        def chunk(k, carry):
            r0 = pl.multiple_of(_chunk_in_time_order(k, nchunks, asc, False) * rc, rc)
            ub = u_ref[pl.ds(r0, rc), :].astype(BF16)
            h_r[...] = _dot(ub, wbr)
            h_i[...] = _dot(ub, wbi)
            carry, _ = _scan_chunk(h_r, h_i, 0, rc, carry, tables, asc)
            y_ref[pl.ds(r0, rc), :] = _dot(h_r[...].astype(BF16), wcr) + _dot(h_i[...].astype(BF16), wci)
            return carry
        lax.fori_loop(0, nchunks, chunk, _zero_carry())

    return _pcall(body, name="s5_head", grid=(nq,), in_specs=[tok, lam, lam, w_in, w_in, w_out, w_out], out_specs=tok,
                  out_shape=jax.ShapeDtypeStruct((n, cs), F32),
                  scratch_shapes=[pltpu.VMEM((rc, STATE_BLOCK), F32)] * 2,
                  compiler_params=_params(("parallel",)))(u, lr, li, b_re, b_im, c_re, c_imn)


def _ssm_bwd_call(u, dy, lr, li, b_re, b_im, c_re, c_imn, rc, asc):
    n, cs = u.shape
    nq, nchunks = cs // LANE, n // rc
    tok, lam, w_in, w_out = _ssm_specs(n)

    def body(u_ref, dy_ref, lr_ref, li_ref, br_ref, bi_ref, cr_ref, ci_ref,
             du_ref, glr_ref, gli_ref, dbr_ref, dbi_ref, dcr_ref, dci_ref, h_r, h_i, a_r, a_i):
        wbr, wbi = br_ref[...].astype(BF16), bi_ref[...].astype(BF16)
        wcr, wci = cr_ref[...].astype(BF16), ci_ref[...].astype(BF16)

        tables = _half_tables(lr_ref, li_ref, asc, False)

        def chunk(k, carry):
            r0 = pl.multiple_of(_chunk_in_time_order(k, nchunks, asc, False) * rc, rc)
            ub = u_ref[pl.ds(r0, rc), :].astype(BF16)
            h_r[pl.ds(r0, rc), :] = _dot(ub, wbr)
            h_i[pl.ds(r0, rc), :] = _dot(ub, wbi)
            carry, _ = _scan_chunk(h_r, h_i, r0, rc, carry, tables, asc)
            return carry
        lax.fori_loop(0, nchunks, chunk, _zero_carry())

        adj = _half_tables(lr_ref, li_ref, not asc, True)
        for ref in (glr_ref, gli_ref, dbr_ref, dbi_ref, dcr_ref, dci_ref):
            ref[...] = jnp.zeros_like(ref)

        def chunk_adj(k, carry):
            r0 = pl.multiple_of(_chunk_in_time_order(k, nchunks, asc, True) * rc, rc)
            dyc = dy_ref[pl.ds(r0, rc), :]
            dyb = dyc.astype(BF16)
            a_r[...] = _dot_nt(dyb, wcr)
            a_i[...] = _dot_nt(dyb, wci)
            carry, grads = _scan_chunk(a_r, a_i, 0, rc, carry, adj, not asc, lam_grad=(h_r, h_i, r0))
            for h, (gr, gi) in enumerate(grads):
                cols = slice(h * SCAN_LANES, (h + 1) * SCAN_LANES)
                glr_ref[:, cols] += gr
                gli_ref[:, cols] += gi
            ab_r, ab_i = a_r[...].astype(BF16), a_i[...].astype(BF16)
            du_ref[pl.ds(r0, rc), :] = _dot_nt(ab_r, wbr) + _dot_nt(ab_i, wbi)
            ut = u_ref[pl.ds(r0, rc), :].T.astype(BF16)
            dbr_ref[...] += _dot(ut, ab_r)
            dbi_ref[...] += _dot(ut, ab_i)
            dyt = dyc.T.astype(BF16)
            dcr_ref[...] += _dot(dyt, h_r[pl.ds(r0, rc), :].astype(BF16))
            dci_ref[...] += _dot(dyt, h_i[pl.ds(r0, rc), :].astype(BF16))
            return carry
        lax.fori_loop(0, nchunks, chunk_adj, _zero_carry())

    part = pl.BlockSpec((SUBLANE, STATE_BLOCK), lambda q: (0, q))
    w_states = lr.shape[1]
    return _pcall(body, name="s5_head_bwd", grid=(nq,), in_specs=[tok, tok, lam, lam, w_in, w_in, w_out, w_out],
                  out_specs=[tok, part, part, w_in, w_in, w_in, w_in],
                  out_shape=[jax.ShapeDtypeStruct((n, cs), F32)] + [jax.ShapeDtypeStruct((SUBLANE, w_states), F32)] * 2
                  + [jax.ShapeDtypeStruct((nq, LANE, STATE_BLOCK), F32)] * 4,
                  scratch_shapes=[pltpu.VMEM((n, STATE_BLOCK), F32)] * 2 + [pltpu.VMEM((rc, STATE_BLOCK), F32)] * 2,
                  compiler_params=_params(("parallel",)))(u, dy, lr, li, b_re, b_im, c_re, c_imn)


def _make_ssm(rc, asc):
    @jax.custom_vjp
    def ssm(u, lr, li, b_re, b_im, c_re, c_imn):
        return _ssm_fwd_call(u, lr, li, b_re, b_im, c_re, c_imn, rc, asc)

    def fwd(*args):
        return ssm(*args), args

    def bwd(args, dy):
        du, glr, gli, dbr, dbi, dcr, dci = _ssm_bwd_call(args[0], dy, *args[1:], rc, asc)
        return (du, jnp.sum(glr, axis=0, keepdims=True), jnp.sum(gli, axis=0, keepdims=True), dbr, dbi,
                jnp.swapaxes(dcr, 1, 2), jnp.swapaxes(dci, 1, 2))

    ssm.defvjp(fwd, bwd)
    return ssm


def _blocks_in(bb):
    g, p, h = bb.shape
    k = GROUPS_PER_BLOCK
    out = jnp.einsum('qgph,gk->qghkp', bb.reshape(g // k, k, p, h), jnp.eye(k, dtype=F32))
    return out.reshape(g // k, k * h, k * p)


def _blocks_out(cc):
    g, h, p = cc.shape
    k = GROUPS_PER_BLOCK
    out = jnp.einsum('qghp,gk->qgpkh', cc.reshape(g // k, k, h, p), jnp.eye(k, dtype=F32))
    return out.reshape(g // k, k * p, k * h)


def _position():
    return lax.axis_index("x"), lax.axis_index("y"), lax.axis_index("c")


def _linear_index():
    x, y, c = _position()
    return 4 * x + 2 * y + c


COMM_SCRATCH = [pltpu.SemaphoreType.DMA((7,)), pltpu.SemaphoreType.DMA((7,)), pltpu.SemaphoreType.DMA(())]
ANY_SPEC = pl.BlockSpec(memory_space=pl.ANY)


def _gather_parts(x_ref, o_ref, send_sems, recv_sems, local_sem):
    x, y, c = _position()
    me, sibling = (x, y, c), (x, y, 1 - c)
    chips = [(1 - x, y), (x, 1 - y), (1 - x, 1 - y)]

    def block(px, py, pc):
        return o_ref.at[4 * px + 2 * py + pc]

    def copy(k, blk, to, src=None):
        return pltpu.make_async_remote_copy(
            src_ref=block(*blk) if src is None else src, dst_ref=block(*blk),
            send_sem=send_sems.at[k], recv_sem=recv_sems.at[k], device_id=to, device_id_type=MESH)

    mine = pltpu.make_async_copy(x_ref, block(*me), local_sem)
    first = [copy(0, me, sibling, src=x_ref)]
    first += [copy(1 + j, me, (*chip, c), src=x_ref) for j, chip in enumerate(chips)]
    passed = [copy(4 + j, (*chip, c), sibling) for j, chip in enumerate(chips)]

    def start():
        mine.start()
        for cp in first:
            cp.start()

    def finish():
        for j, chip in enumerate(chips):
            copy(1 + j, (*chip, c), me).wait_recv()
            passed[j].start()
        copy(0, sibling, me).wait_recv()
        for j, chip in enumerate(chips):
            copy(4 + j, (*chip, 1 - c), me).wait_recv()
        for cp in first + passed:
            cp.wait_send()
        mine.wait()

    return start, finish


def _scatter_parts(g_ref, o_ref, send_sems, recv_sems, local_sem, rows=None):
    x, y, c = _position()

    def block(p):
        return g_ref.at[p] if rows is None else g_ref.at[p, pl.ds(rows[0], rows[1])]

    mine = pltpu.make_async_copy(block(4 * x + 2 * y + c), o_ref.at[0], local_sem)
    copies = []
    for k in range(1, N_DEV):
        px = 1 - x if k & 4 else x
        py = 1 - y if k & 2 else y
        pc = 1 - c if k & 1 else c
        copies.append(pltpu.make_async_remote_copy(
            src_ref=block(4 * px + 2 * py + pc), dst_ref=o_ref.at[k],
            send_sem=send_sems.at[k - 1], recv_sem=recv_sems.at[k - 1],
            device_id=(px, py, pc), device_id_type=MESH))

    def start():
        mine.start()
        for cp in copies:
            cp.start()

    def finish():
        for cp in copies:
            cp.wait()
        mine.wait()

    return start, finish


def _comm_call(parts, src, out_shape, name):
    def body(s_ref, o_ref, *sems):
        start, finish = parts(s_ref, o_ref, *sems)
        start()
        finish()

    return _pcall(body, name=name, in_specs=[ANY_SPEC], out_specs=ANY_SPEC, out_shape=out_shape,
                  scratch_shapes=COMM_SCRATCH)(src)


def _ag_call(shard, name):
    return _comm_call(_gather_parts, shard, jax.ShapeDtypeStruct((N_DEV,) + shard.shape, shard.dtype), name)


def _rs_call(g, name):
    return _comm_call(_scatter_parts, g, jax.ShapeDtypeStruct(g.shape, g.dtype), name)


def _sum8_call(parts, name):
    _, r, c = parts.shape
    tr = _tile(r, max(PACK, (4 << 20) // (N_DEV * c * parts.dtype.itemsize) // PACK * PACK), PACK)

    def body(p_ref, o_ref):
        acc = p_ref[0].astype(F32)
        for k in range(1, N_DEV):
            acc = acc + p_ref[k].astype(F32)
        o_ref[...] = acc

    return _pcall(body, name=name, grid=(r // tr,),
                  in_specs=[pl.BlockSpec((N_DEV, tr, c), lambda i: (0, i, 0))],
                  out_specs=pl.BlockSpec((tr, c), lambda i: (i, 0)),
                  out_shape=jax.ShapeDtypeStruct((r, c), F32), compiler_params=_params(("parallel",)))(parts)


def _reduce_scatter(g, name):
    return _sum8_call(_rs_call(g, name), name + "_sum")


def _make_gather(dtype, name):
    @jax.custom_vjp
    def gather(shard):
        return _ag_call(shard.astype(dtype), name)

    def fwd(shard):
        return gather(shard), None

    def bwd(_, ct):
        return (_reduce_scatter(ct, name + "_rs"),)

    gather.defvjp(fwd, bwd)
    return gather


def _adam_call(w, g, m, v, name):
    r, c = w.shape
    tr = _tile(r, max(SUBLANE, (1 << 20) // (4 * c) // SUBLANE * SUBLANE), SUBLANE)

    def body(w_ref, g_ref, m_ref, v_ref, d_ref, mo_ref, vo_ref):
        gv = g_ref[...]
        m2 = ADAM_B1 * m_ref[...] + (1.0 - ADAM_B1) * gv
        v2 = ADAM_B2 * v_ref[...] + (1.0 - ADAM_B2) * (gv * gv)
        m_hat = m2 / (1.0 - ADAM_B1 ** ADAM_STEP)
        v_hat = v2 / (1.0 - ADAM_B2 ** ADAM_STEP)
        d_ref[...] = -ADAM_LR * (m_hat / (jnp.sqrt(v_hat) + ADAM_EPS) + ADAM_WD * w_ref[...])
        mo_ref[...] = m2
        vo_ref[...] = v2

    spec = pl.BlockSpec((tr, c), lambda i: (i, 0))
    return _pcall(body, name=name, grid=(r // tr,), in_specs=[spec] * 4, out_specs=[spec] * 3,
                  out_shape=[jax.ShapeDtypeStruct((r, c), F32)] * 3, compiler_params=_params(("parallel",)))(w, g, m, v)


def _loss_call(y, target, tr):
    n, d = y.shape

    def body(y_ref, t_ref, s_ref, dy_ref):
        i = pl.program_id(0)
        err = y_ref[...] - t_ref[...]
        dy_ref[...] = err * (1.0 / d)
        part = jnp.sum(jnp.sum(err * err, axis=1, keepdims=True), axis=0, keepdims=True)

        @pl.when(i == 0)
        def _():
            s_ref[...] = jnp.zeros_like(s_ref)
        s_ref[...] += part

    spec = pl.BlockSpec((tr, d), lambda i: (i, 0))
    return _pcall(body, name="loss_head", grid=(n // tr,), in_specs=[spec, spec],
                  out_specs=[pl.BlockSpec((1, 1), lambda i: (0, 0)), spec],
                  out_shape=[jax.ShapeDtypeStruct((1, 1), F32), jax.ShapeDtypeStruct((n, d), F32)],
                  compiler_params=_params(("arbitrary",)))(y, target)


def _pack(arrays, rows_mult):
    flat = jnp.concatenate([a.reshape(-1).astype(F32) for a in arrays])
    rows = -(-flat.shape[0] // LANE)
    rows = -(-rows // rows_mult) * rows_mult
    return jnp.pad(flat, (0, rows * LANE - flat.shape[0])).reshape(rows, LANE)


def _unpack(buf, shapes, lead=()):
    flat = buf.reshape(lead + (-1,))
    out, pos = [], 0
    for s in shapes:
        size = math.prod(s)
        out.append(flat[..., pos:pos + size].reshape(lead + tuple(s)))
        pos += size
    return out


def _s5_discretise(a_re, a_im, log_dt, b_re, b_im):
    dt = jnp.exp(log_dt)[:, None]
    mag = jnp.exp(a_re * dt)
    lam_re = mag * jnp.cos(a_im * dt)
    lam_im = mag * jnp.sin(a_im * dt)
    denom = a_re * a_re + a_im * a_im
    nr, ni = lam_re - 1.0, lam_im
    f_re = (nr * a_re + ni * a_im) / denom
    f_im = (ni * a_re - nr * a_im) / denom
    bb_re = f_re[..., None] * b_re - f_im[..., None] * b_im
    bb_im = f_re[..., None] * b_im + f_im[..., None] * b_re
    return lam_re, lam_im, bb_re, bb_im


def _forward(x, p, ctx, s_c, n_ctx, n_lat):
    d_model = x.shape[1]
    depth = len(p['w_in'])
    mix = p['w_in'][0].shape[1]
    pool_width = p['pool_scale'].shape[1]
    tr = n_ctx
    me = _linear_index()

    rmsmod = _make_rowop(_fn_rmsmod, 1, 2, 1, [BF16], tr, "rms_modulate")
    resid = _make_rowop(_fn_resid, 2, 1, 1, [F32], tr, "gated_residual")
    gelu_head = _make_rowop(_fn_gelu_head, 2, 0, 1, [F32], tr, "ssm_gelu")
    mixcat = _make_rowop(_fn_mixcat, 6, 0, 1, [BF16], tr, "mix_concat")
    pool = _make_pool(n_ctx, n_lat, pool_width, mix)
    convgate = _make_convgate(n_ctx, n_lat)
    ssm = [_make_ssm(tr, True), _make_ssm(tr, False)]
    lin_f32 = {k: _make_linear(F32, k) for k in ("w_in", "w_pool", "w_glu", "w_out", "w_down", "ada")}
    lin_up = _make_linear(BF16, "w_up")
    gather_big = {k: _make_gather(BF16, "gather_" + k) for k in BIG_SHARDED}
    gather_small = _make_gather(F32, "gather_small")
    gather_mod = _make_gather(F32, "gather_mod")

    whole = {k: gather_big[k](p[k][0]) for k in BIG_SHARDED}
    host = {k: _make_hosting_linear(BF16 if k == 'w_up' else F32, k) for k in BIG_SHARDED}

    def big_linear(k, a, l):
        w = whole[k] if k == 'w_up' else whole[k].reshape(-1, whole[k].shape[2])
        if l + 1 == depth:
            return (lin_up if k == 'w_up' else lin_f32[k])(a, w)
        y, whole[k] = host[k](a, w, p[k][l + 1])
        return y

    small = [p[k] for k in SMALL_SHARDED]
    packed = gather_small(_pack(small, PACK))
    w_pool, w_glu, w_conv = _unpack(packed, [a.shape for a in small], lead=(N_DEV,))
    w_pool = jnp.moveaxis(w_pool, 0, 2).reshape(depth, len(POOL_WINDOWS), -1, w_pool.shape[-1])
    w_glu = jnp.moveaxis(w_glu, 0, 1).reshape(depth, -1, w_glu.shape[-1])
    w_conv = jnp.moveaxis(w_conv, 0, 3).reshape(depth, 9, -1)

    s_rows = jnp.concatenate([s_c, jax.nn.silu(p['c_ctx'])[None, :],
                              jnp.zeros((PACK - N_DEV - 1, d_model), F32)], axis=0)
    cols = p['w_ada'][0].shape[1]
    b_loc = lax.dynamic_slice_in_dim(p['b_ada'], me * cols, cols, axis=1)
    mod_loc = jnp.stack([lin_f32['ada'](s_rows, p['w_ada'][l]) + b_loc[l][None, :] for l in range(depth)])
    mod = gather_mod(mod_loc.reshape(depth * PACK, cols)).reshape(N_DEV, depth, PACK, cols)
    mod = jnp.moveaxis(mod, 0, 2).reshape(depth, PACK, 6, d_model)
    mod_lat = lax.dynamic_index_in_dim(mod, me, axis=1, keepdims=False)
    mod_ctx = mod[:, N_DEV]

    xs = jnp.concatenate([ctx, x], axis=0)
    for l in range(depth):
        def seg(k):
            return jnp.stack([mod_ctx[l, k], mod_lat[l, k]]).reshape(2, 1, d_model)

        def row(name):
            return p[name][l].reshape(1, -1)

        h1, = rmsmod(xs, seg(0), seg(1), row('g_pre_mix'))
        u = big_linear('w_in', h1, l)
        pooled = pool(u)
        yp = [lin_f32['w_pool'](pooled[g], w_pool[l, g]) for g in range(len(POOL_WINDOWS))]
        u_ssm = u[:, pool_width:]
        ys = []
        for d in range(2):
            lam_re, lam_im, bb_re, bb_im = _s5_discretise(
                p['ssm_a_re'][l, d], p['ssm_a_im'][l, d], p['ssm_log_dt'][l, d], p['ssm_b_re'][l, d], p['ssm_b_im'][l, d])
            ys.append(ssm[d](u_ssm, lam_re.reshape(1, -1), lam_im.reshape(1, -1), _blocks_in(bb_re), _blocks_in(bb_im),
                             _blocks_out(p['ssm_c_re'][l, d]), _blocks_out(-p['ssm_c_im'][l, d])))
        gl, = gelu_head(ys[0] + ys[1], u_ssm, row('ssm_d'))
        s = lin_f32['w_glu'](gl, w_glu[l])
        cat, = mixcat(*yp, gl, s, row('pool_scale'))
        mixed = big_linear('w_out', cat, l)
        x1, = resid(xs, mixed, seg(2), row('g_post_mix'))
        h2, = rmsmod(x1, seg(3), seg(4), row('g_pre_ffn'))
        z = big_linear('w_up', h2, l)
        a = convgate(z, w_conv[l])
        f = big_linear('w_down', a, l)
        xs, = resid(x1, f, seg(5), row('g_post_ffn'))
    return xs[n_ctx:]


def _as_rows(a):
    return a.reshape(-1, a.shape[-1])


def kernel(x, c, ctx, c_ctx, w_ada, b_ada, w_in, w_pool, pool_scale, ssm_a_re, ssm_a_im, ssm_log_dt, ssm_b_re, ssm_b_im, ssm_c_re, ssm_c_im, ssm_d, w_glu, w_out, g_pre_mix, g_post_mix, g_pre_ffn, g_post_ffn, w_up, w_conv, w_down, loss_target, m_c_ctx, m_w_ada, m_b_ada, m_w_in, m_w_pool, m_pool_scale, m_ssm_a_re, m_ssm_a_im, m_ssm_log_dt, m_ssm_b_re, m_ssm_b_im, m_ssm_c_re, m_ssm_c_im, m_ssm_d, m_w_glu, m_w_out, m_g_pre_mix, m_g_post_mix, m_g_pre_ffn, m_g_post_ffn, m_w_up, m_w_conv, m_w_down, v_c_ctx, v_w_ada, v_b_ada, v_w_in, v_w_pool, v_pool_scale, v_ssm_a_re, v_ssm_a_im, v_ssm_log_dt, v_ssm_b_re, v_ssm_b_im, v_ssm_c_re, v_ssm_c_im, v_ssm_d, v_w_glu, v_w_out, v_g_pre_mix, v_g_post_mix, v_g_pre_ffn, v_g_post_ffn, v_w_up, v_w_conv, v_w_down):
    weights = dict(zip(WEIGHTS, (c_ctx, w_ada, b_ada, w_in, w_pool, pool_scale, ssm_a_re, ssm_a_im, ssm_log_dt, ssm_b_re,
                                 ssm_b_im, ssm_c_re, ssm_c_im, ssm_d, w_glu, w_out, g_pre_mix, g_post_mix, g_pre_ffn,
                                 g_post_ffn, w_up, w_conv, w_down)))
    m_in = dict(zip(WEIGHTS, (m_c_ctx, m_w_ada, m_b_ada, m_w_in, m_w_pool, m_pool_scale, m_ssm_a_re, m_ssm_a_im,
                              m_ssm_log_dt, m_ssm_b_re, m_ssm_b_im, m_ssm_c_re, m_ssm_c_im, m_ssm_d, m_w_glu, m_w_out,
                              m_g_pre_mix, m_g_post_mix, m_g_pre_ffn, m_g_post_ffn, m_w_up, m_w_conv, m_w_down)))
    v_in = dict(zip(WEIGHTS, (v_c_ctx, v_w_ada, v_b_ada, v_w_in, v_w_pool, v_pool_scale, v_ssm_a_re, v_ssm_a_im,
                              v_ssm_log_dt, v_ssm_b_re, v_ssm_b_im, v_ssm_c_re, v_ssm_c_im, v_ssm_d, v_w_glu, v_w_out,
                              v_g_pre_mix, v_g_post_mix, v_g_pre_ffn, v_g_post_ffn, v_w_up, v_w_conv, v_w_down)))
    depth = w_in.shape[0]
    n_lat, d_model = x.shape[1], x.shape[2]
    n_ctx = ctx.shape[1]
    per_layer = BIG_SHARDED + ['w_ada']

    c_rows = jnp.concatenate([c, jnp.zeros((SUBLANE - 1, d_model), F32)], axis=0)
    s_c = jax.nn.silu(_ag_call(c_rows, "gather_c")[:, 0, :])

    params = {k: ([w[l] for l in range(depth)] if k in per_layer else w) for k, w in weights.items()}

    def run(x2d, prm):
        return _forward(x2d, prm, ctx[0], s_c, n_ctx, n_lat)

    y, vjp = jax.vjp(run, x[0], params)
    sq, dy = _loss_call(y, loss_target[0], n_ctx)
    loss = lax.psum(0.5 * sq[0, 0] / d_model, ("x", "y", "c"))
    gx, grads = vjp(dy)
    grads = {k: (jnp.stack(g) if k in per_layer else g) for k, g in grads.items()}

    rep_shapes = [weights[k].shape for k in REPLICATED]
    contrib = _pack([grads[k] for k in REPLICATED], N_DEV * PACK)
    rows = contrib.shape[0] // N_DEV
    total = _ag_call(_reduce_scatter(contrib.reshape(N_DEV, rows, LANE), "reduce_replicated"), "gather_replicated")
    total = total.reshape(N_DEV * rows, LANE)
    for k, g in zip(REPLICATED, _unpack(total, rep_shapes)):
        grads[k] = g

    delta, new_m, new_v = {}, {}, {}
    rep = [_pack([src[k] for k in REPLICATED], N_DEV * PACK) for src in (weights, m_in, v_in)]
    upd = _adam_call(rep[0], total, rep[1], rep[2], "adamw_replicated")
    for out, buf in zip((delta, new_m, new_v), upd):
        out.update(zip(REPLICATED, _unpack(buf, rep_shapes)))
    for k in WEIGHTS:
        if k in REPLICATED:
            continue
        upd = _adam_call(_as_rows(weights[k]), _as_rows(grads[k]), _as_rows(m_in[k]), _as_rows(v_in[k]), "adamw_" + k)
        for out, buf in zip((delta, new_m, new_v), upd):
            out[k] = buf.reshape(weights[k].shape)

    return (loss, gx[None], *[grads[k] for k in WEIGHTS], *[delta[k] for k in WEIGHTS],
            *[new_m[k] for k in WEIGHTS], *[new_v[k] for k in WEIGHTS])
```

```python
import functools
import math

import jax
import jax.numpy as jnp
from jax import lax
from jax.experimental import pallas as pl
from jax.experimental.pallas import tpu as pltpu

F32 = jnp.float32
BF16 = jnp.bfloat16

N_DEV = 8
GRID_W = 64
POOL_WINDOWS = (2, 4, 8, 16)
SSM_GROUP = 16
SSM_STATE = 64
EPS = 1e-6
ADAM_LR = 0.001
ADAM_B1 = 0.9
ADAM_B2 = 0.999
ADAM_EPS = 1e-08
ADAM_WD = 0.01
ADAM_STEP = 10

SUBLANE = 8
PACK = 16
LANE = 128
VMEM_LIMIT = 56 * 1024 * 1024
MM_VMEM_BUDGET = 40 * 1024 * 1024
MESH = pl.DeviceIdType.MESH

WEIGHTS = ['c_ctx', 'w_ada', 'b_ada', 'w_in', 'w_pool', 'pool_scale', 'ssm_a_re', 'ssm_a_im', 'ssm_log_dt',
           'ssm_b_re', 'ssm_b_im', 'ssm_c_re', 'ssm_c_im', 'ssm_d', 'w_glu', 'w_out', 'g_pre_mix',
           'g_post_mix', 'g_pre_ffn', 'g_post_ffn', 'w_up', 'w_conv', 'w_down']
REPLICATED = ['c_ctx', 'b_ada', 'pool_scale', 'ssm_a_re', 'ssm_a_im', 'ssm_log_dt', 'ssm_b_re', 'ssm_b_im',
              'ssm_c_re', 'ssm_c_im', 'ssm_d', 'g_pre_mix', 'g_post_mix', 'g_pre_ffn', 'g_post_ffn']
BIG_SHARDED = ['w_in', 'w_out', 'w_up', 'w_down']
SMALL_SHARDED = ['w_pool', 'w_glu', 'w_conv']


def _pcall(body, **kw):
    return pl.pallas_call(body, **kw)


def _params(sem=None):
    return pltpu.CompilerParams(dimension_semantics=sem, vmem_limit_bytes=VMEM_LIMIT)


def _tile(n, cap, mult):
    if n <= cap:
        return n
    best = None
    d = mult
    while d <= cap:
        if n % d == 0:
            best = d
        d += mult
    assert best is not None, (n, cap, mult)
    return best


def _sigmoid(x):
    return 1.0 / (1.0 + jnp.exp(-x))


def _mm(a, b, *, tb=False, out_dtype=F32, out_blocked=False, side=None, name):
    b_blocked = b.ndim == 3
    M, K = a.shape
    if b_blocked:
        nb, br, bc = b.shape
        N, Kb = (br, nb * bc) if tb else (nb * bc, br)
    else:
        N, Kb = (b.shape if tb else b.shape[::-1])
    assert K == Kb, (a.shape, b.shape, tb)
    tm = _tile(M, 1408, PACK)
    tk = bc if (b_blocked and tb) else _tile(K, 2304, LANE)

    def vmem_bytes(tn):
        tiles = tm * tk * a.dtype.itemsize + tk * tn * b.dtype.itemsize + tm * tn * jnp.dtype(out_dtype).itemsize
        return 2 * tiles + tm * tn * 4

    if out_blocked:
        assert N % N_DEV == 0
        tn = N // N_DEV
    elif b_blocked and not tb:
        tn = bc
    else:
        tn = _tile(N, 1024, LANE)
        if vmem_bytes(tn) > MM_VMEM_BUDGET:
            tn = _tile(N, 512, LANE)
    if b_blocked and not tb:
        assert tn == bc
    nm, nn, nk = M // tm, N // tn, K // tk

    a_spec = pl.BlockSpec((tm, tk), lambda i, j, k: (i, k))
    if b_blocked:
        if tb:
            b_spec = pl.BlockSpec((None, tn, tk), lambda i, j, k: (k, j, 0))
        else:
            b_spec = pl.BlockSpec((None, tk, tn), lambda i, j, k: (j, k, 0))
    else:
        b_spec = pl.BlockSpec((tn, tk), lambda i, j, k: (j, k)) if tb else pl.BlockSpec((tk, tn), lambda i, j, k: (k, j))
    if out_blocked:
        o_spec = pl.BlockSpec((None, tm, tn), lambda i, j, k: (j, i, 0))
        o_shape = jax.ShapeDtypeStruct((N_DEV, M, tn), out_dtype)
    else:
        o_spec = pl.BlockSpec((tm, tn), lambda i, j, k: (i, j))
        o_shape = jax.ShapeDtypeStruct((M, N), out_dtype)
    dims = (((1,), ((1 if tb else 0),)), ((), ()))

    def matmul(a_ref, b_ref, o_ref, acc_ref):
        k = pl.program_id(2)
        part = lax.dot_general(a_ref[...].astype(BF16), b_ref[...].astype(BF16), dims, preferred_element_type=F32)
        if nk == 1:
            o_ref[...] = part.astype(o_ref.dtype)
        else:
            @pl.when(k == 0)
            def _():
                acc_ref[...] = part

            @pl.when(k > 0)
            def _():
                acc_ref[...] += part

            @pl.when(k == nk - 1)
            def _():
                o_ref[...] = acc_ref[...].astype(o_ref.dtype)

    out = _call_with_side(matmul, name=name, grid=(nm, nn, nk), in_specs=[a_spec, b_spec], out_specs=[o_spec],
                          out_shape=[o_shape], scratch_shapes=[pltpu.VMEM((tm, tn), F32)], args=(a, b), side=side,
                          semantics=("parallel", "parallel", "arbitrary"))
    return out[0] if side is None else out


def _call_with_side(core, *, name, grid, in_specs, out_specs, out_shape, scratch_shapes, args, side, semantics):
    if side is None:
        return _pcall(core, name=name, grid=grid, in_specs=in_specs, out_specs=out_specs, out_shape=out_shape,
                      scratch_shapes=scratch_shapes, compiler_params=_params(semantics))(*args)
    parts, src, side_shape = side
    n_in, n_out, n_scr = len(in_specs), len(out_specs), len(scratch_shapes)

    def body(*refs):
        ins, s_ref = refs[:n_in], refs[n_in]
        outs, so_ref = refs[n_in + 1:n_in + 1 + n_out], refs[n_in + 1 + n_out]
        scr, sems = refs[n_in + 2 + n_out:n_in + 2 + n_out + n_scr], refs[n_in + 2 + n_out + n_scr:]
        step = 0
        for d, g in enumerate(grid):
            step = step * g + pl.program_id(d)
        start, finish = parts(s_ref, so_ref, *sems)
        pl.when(step == 0)(start)
        core(*ins, *outs, *scr)
        pl.when(step == math.prod(grid) - 1)(finish)

    return _pcall(body, name=name, grid=grid, in_specs=list(in_specs) + [ANY_SPEC],
                  out_specs=list(out_specs) + [ANY_SPEC], out_shape=list(out_shape) + [side_shape],
                  scratch_shapes=list(scratch_shapes) + COMM_SCRATCH,
                  compiler_params=_params(("arbitrary",) * len(grid)))(*args, src)


def _make_linear(out_dtype, name):
    @jax.custom_vjp
    def lin(a, w):
        return _mm(a, w, out_dtype=out_dtype, name=name)

    def fwd(a, w):
        return lin(a, w), (a, w)

    def bwd(res, dy):
        a, w = res
        da = _mm(dy, w, tb=True, out_dtype=a.dtype, name=name + "_da")
        dw = _mm(a.T, dy, out_dtype=w.dtype, out_blocked=(w.ndim == 3), name=name + "_dw")
        return da, dw

    lin.defvjp(fwd, bwd)
    return lin


def _make_hosting_linear(out_dtype, name):
    @jax.custom_vjp
    def lin(a, w, next_shard):
        nxt = next_shard.astype(BF16)
        side = (_gather_parts, nxt, jax.ShapeDtypeStruct((N_DEV,) + nxt.shape, BF16))
        return tuple(_mm(a, w, out_dtype=out_dtype, side=side, name=name + "_gather"))

    def fwd(a, w, next_shard):
        return lin(a, w, next_shard), (a, w)

    def bwd(res, cts):
        a, w = res
        dy, d_next = cts
        nb, r, c = d_next.shape
        half = r // 2

        def side(first, count):
            return (functools.partial(_scatter_parts, rows=(first, count)), d_next,
                    jax.ShapeDtypeStruct((nb, count, c), d_next.dtype))

        dw, lo = _mm(a.T, dy, out_dtype=w.dtype, out_blocked=(w.ndim == 3), side=side(0, half),
                     name=name + "_dw_scatter")
        da, hi = _mm(dy, w, tb=True, out_dtype=a.dtype, side=side(half, r - half), name=name + "_da_scatter")
        d_shard = jnp.concatenate([_sum8_call(lo, name + "_sum_lo"), _sum8_call(hi, name + "_sum_hi")], axis=0)
        return da, dw, d_shard

    lin.defvjp(fwd, bwd)
    return lin


def _make_rowop(fn, n_row, n_seg, n_bc, out_dtypes, tr, name):
    def specs(args):
        rows, segs, bcs = args[:n_row], args[n_row:n_row + n_seg], args[n_row + n_seg:]
        sp = [pl.BlockSpec((tr, r.shape[1]), lambda i: (i, 0)) for r in rows]
        sp += [pl.BlockSpec((None, 1, s.shape[2]), lambda i: (jnp.minimum(i, 1), 0, 0)) for s in segs]
        sp += [pl.BlockSpec((1, b.shape[1]), lambda i: (0, 0)) for b in bcs]
        return sp

    def out_widths(args):
        tiles = [jax.ShapeDtypeStruct((tr, a.shape[-1]), a.dtype) for a in args[:n_row]]
        tiles += [jax.ShapeDtypeStruct((1, a.shape[-1]), a.dtype) for a in args[n_row:]]
        return [o.shape[1] for o in jax.eval_shape(fn, *tiles)]

    def fwd_call(*args):
        n = args[0].shape[0]
        widths = out_widths(args)
        n_in = len(args)

        def body(*refs):
            vals = [r[...] for r in refs[:n_in]]
            outs = fn(*vals)
            for o_ref, o in zip(refs[n_in:], outs):
                o_ref[...] = o.astype(o_ref.dtype)

        return _pcall(body, name=name, grid=(n // tr,), in_specs=specs(args),
                      out_specs=[pl.BlockSpec((tr, w), lambda i: (i, 0)) for w in widths],
                      out_shape=[jax.ShapeDtypeStruct((n, w), d) for w, d in zip(widths, out_dtypes)],
                      compiler_params=_params(("parallel",)))(*args)

    def bwd_call(args, cots):
        n = args[0].shape[0]
        n_in = len(args)
        n_ct = len(cots)
        rows, segs, bcs = args[:n_row], args[n_row:n_row + n_seg], args[n_row + n_seg:]

        def body(*refs):
            i = pl.program_id(0)
            vals = [r[...] for r in refs[:n_in]]
            cts = [r[...].astype(F32) for r in refs[n_in:n_in + n_ct]]
            outs = refs[n_in + n_ct:]
            _, vjp = jax.vjp(lambda *v: tuple(fn(*v)), *vals)
            grads = vjp(tuple(cts))
            for o_ref, g in zip(outs[:n_row], grads[:n_row]):
                o_ref[...] = g.astype(o_ref.dtype)
            for o_ref, g in zip(outs[n_row:n_row + n_seg], grads[n_row:n_row + n_seg]):
                @pl.when(i <= 1)
                def _():
                    o_ref[...] = jnp.zeros_like(o_ref)
                o_ref[...] += g.astype(F32)
            for o_ref, g in zip(outs[n_row + n_seg:], grads[n_row + n_seg:]):
                @pl.when(i == 0)
                def _():
                    o_ref[...] = jnp.zeros_like(o_ref)
                o_ref[...] += g.astype(F32)

        out_specs = [pl.BlockSpec((tr, r.shape[1]), lambda i: (i, 0)) for r in rows]
        out_specs += [pl.BlockSpec((None, 1, s.shape[2]), lambda i: (jnp.minimum(i, 1), 0, 0)) for s in segs]
        out_specs += [pl.BlockSpec((1, b.shape[1]), lambda i: (0, 0)) for b in bcs]
        out_shape = [jax.ShapeDtypeStruct(r.shape, r.dtype) for r in rows]
        out_shape += [jax.ShapeDtypeStruct(s.shape, F32) for s in segs]
        out_shape += [jax.ShapeDtypeStruct(b.shape, F32) for b in bcs]
        in_specs = specs(args) + [pl.BlockSpec((tr, c.shape[1]), lambda i: (i, 0)) for c in cots]
        return _pcall(body, name=name + "_bwd", grid=(n // tr,), in_specs=in_specs, out_specs=out_specs,
                      out_shape=out_shape, compiler_params=_params(("arbitrary",)))(*args, *cots)

    @jax.custom_vjp
    def op(*args):
        return tuple(fwd_call(*args))

    def op_fwd(*args):
        return op(*args), args

    def op_bwd(args, cots):
        return tuple(bwd_call(args, list(cots)))

    op.defvjp(op_fwd, op_bwd)
    return op


def _rms(x, g):
    return x * lax.rsqrt(jnp.mean(x * x, axis=-1, keepdims=True) + EPS) * g


def _fn_rmsmod(x, shift, scale, g):
    return (_rms(x.astype(F32), g) * (1.0 + scale) + shift,)


def _fn_resid(x, m, gate, g):
    return (x + gate * _rms(m.astype(F32), g),)


def _fn_gelu_head(y, u, d):
    return (jax.nn.gelu(y + d * u.astype(F32)),)


def _fn_mixcat(y0, y1, y2, y3, gl, s, ps):
    pool = jnp.concatenate([y0, y1, y2, y3], axis=1) * ps
    return (jnp.concatenate([pool, gl * _sigmoid(s)], axis=1),)


def _pool_call(x, col_blk0, ncol, w, n_ctx, n_lat, transpose, out_dtype, name):
    n = n_ctx + n_lat
    cw = LANE
    r = n_ctx
    gap = SUBLANE
    half = w // 2
    lat0 = 2 * gap + n_ctx
    nbuf = 3 * gap + n

    def body(x_ref, o_ref, buf):
        def inv_cnt(seg_len, t0):
            t = t0 + lax.broadcasted_iota(jnp.int32, (r, 1), 0)
            cnt = jnp.minimum(t + half, seg_len) - jnp.maximum(t - half, 0)
            return 1.0 / cnt.astype(F32)

        zero = jnp.zeros((gap, cw), F32)
        buf[0:gap, :] = zero
        buf[gap + n_ctx:lat0, :] = zero
        buf[lat0 + n_lat:nbuf, :] = zero

        def fill(src0, dst0, seg_len, t0):
            v = x_ref[pl.ds(src0, r), :].astype(F32)
            if transpose:
                v = v * inv_cnt(seg_len, t0)
            buf[pl.ds(dst0, r), :] = v

        def compute(src0, dst0, seg_len, t0):
            win = buf[pl.ds(dst0 - gap, r + 2 * gap), :]
            nw = r + 2 * gap
            s = win + pltpu.roll(win, (nw - 1) if transpose else 1, 0)
            for sh in (1, 2, 4):
                if w >= 4 * sh:
                    s = pltpu.roll(s, sh, 0) + pltpu.roll(s, nw - sh, 0)
            ws = s[gap:gap + r]
            if transpose:
                out = ws - x_ref[pl.ds(src0, r), :].astype(F32)
            else:
                out = ws * inv_cnt(seg_len, t0) - win[gap:gap + r]
            o_ref[pl.ds(src0, r), :] = out.astype(o_ref.dtype)

        for step in (fill, compute):
            step(0, gap, n_ctx, 0)

            def lat(i, c, step=step):
                off = pl.multiple_of(i * r, r)
                step(n_ctx + off, lat0 + off, n_lat, off)
                return c
            lax.fori_loop(0, n_lat // r, lat, 0)

    return _pcall(body, name=name, grid=(ncol,),
                  in_specs=[pl.BlockSpec((n, cw), lambda j: (0, col_blk0 + j))],
                  out_specs=pl.BlockSpec((n, cw), lambda j: (0, j)),
                  out_shape=jax.ShapeDtypeStruct((n, ncol * cw), out_dtype),
                  scratch_shapes=[pltpu.VMEM((nbuf, cw), F32)],
                  compiler_params=_params(("parallel",)))(x)


def _make_pool(n_ctx, n_lat, pool_width, mix_width):
    ncol = pool_width // len(POOL_WINDOWS) // LANE

    @jax.custom_vjp
    def pool(u):
        return tuple(_pool_call(u, g * ncol, ncol, w, n_ctx, n_lat, False, BF16, "pool_w%d" % w)
                     for g, w in enumerate(POOL_WINDOWS))

    def fwd(u):
        return pool(u), None

    def bwd(_, cots):
        parts = [_pool_call(ct, 0, ncol, w, n_ctx, n_lat, True, F32, "pool_w%d_bwd" % w)
                 for ct, w in zip(cots, POOL_WINDOWS)]
        parts.append(jnp.zeros((n_ctx + n_lat, mix_width - pool_width), F32))
        return (jnp.concatenate(parts, axis=1),)

    pool.defvjp(fwd, bwd)
    return pool


class _ConvGeom:
    def __init__(self, n_ctx, n_lat):
        self.n_ctx, self.n_lat = n_ctx, n_lat
        self.gap = GRID_W + SUBLANE
        self.r = 2 * GRID_W
        self.ctx0 = self.gap
        self.lat0 = 2 * self.gap + n_ctx
        self.nbuf = 3 * self.gap + n_ctx + n_lat
        self.nwin = self.r + 2 * self.gap
        self.n16 = self.r + 2 * SUBLANE
        assert n_lat % self.r == 0 and n_ctx % self.r == 0

    def zero_gaps(self, buf):
        z = jnp.zeros((self.gap, LANE), F32)
        buf[0:self.gap, :] = z
        buf[self.ctx0 + self.n_ctx:self.lat0, :] = z
        buf[self.lat0 + self.n_lat:self.nbuf, :] = z

    def fill(self, src_ref, buf):
        r = self.r

        def seg(src0, dst0, count):
            def one(i, c):
                off = pl.multiple_of(i * r, r)
                buf[pl.ds(dst0 + off, r), :] = src_ref[pl.ds(src0 + off, r), :].astype(F32)
                return c
            lax.fori_loop(0, count, one, 0)
        seg(0, self.ctx0, self.n_ctx // r)
        seg(self.n_ctx, self.lat0, self.n_lat // r)

    def col_masks(self, nrows, first_col):
        col = (lax.broadcasted_iota(jnp.int32, (nrows, 1), 0) + first_col) & (GRID_W - 1)
        return col == GRID_W - 1, col == 0

    def lat_window(self, buf, i):
        ws = pl.multiple_of(self.lat0 - self.gap + i * self.r, SUBLANE)
        return buf[pl.ds(ws, self.nwin), :]

    def ctx_window(self, buf):
        return buf[self.ctx0 - SUBLANE:self.ctx0 + self.n_ctx + SUBLANE, :]

    def lat_sources(self, win):
        last, first = self.col_masks(self.nwin, GRID_W - SUBLANE)
        return jnp.where(last, 0.0, win), win, jnp.where(first, 0.0, win)

    def row_slice(self, x, di, sign):
        st = self.gap - SUBLANE + sign * (di - 1) * GRID_W
        return x[st:st + self.n16]

    def lat_conv(self, win, w):
        srcs = self.lat_sources(win)
        cs = []
        for dj in range(3):
            acc = None
            for di in range(3):
                term = w[di * 3 + dj] * self.row_slice(srcs[dj], di, 1)
                acc = term if acc is None else acc + term
            cs.append(acc)
        out = pltpu.roll(cs[0], 1, 0) + cs[1] + pltpu.roll(cs[2], self.n16 - 1, 0)
        return out[SUBLANE:SUBLANE + self.r]

    def ctx_conv(self, win, w, transpose=False):
        n = win.shape[0]
        lo, hi = (w[5], w[3]) if transpose else (w[3], w[5])
        out = lo * pltpu.roll(win, 1, 0) + w[4] * win + hi * pltpu.roll(win, n - 1, 0)
        return out[SUBLANE:SUBLANE + self.n_ctx]

    def lat_conv_t(self, dwin, w):
        es = []
        for dj in range(3):
            acc = None
            for di in range(3):
                term = w[di * 3 + dj] * self.row_slice(dwin, di, -1)
                acc = term if acc is None else acc + term
            es.append(acc)
        last, first = self.col_masks(self.n16, GRID_W - SUBLANE)
        out = (jnp.where(last, 0.0, pltpu.roll(es[0], self.n16 - 1, 0)) + es[1]
               + jnp.where(first, 0.0, pltpu.roll(es[2], 1, 0)))
        return out[SUBLANE:SUBLANE + self.r]


def _taps(w_ref):
    return [w_ref[k:k + 1, :] for k in range(9)]


def _conv_specs(n, f_tiles):
    zv = pl.BlockSpec((n, LANE), lambda j: (0, j))
    zg = pl.BlockSpec((n, LANE), lambda j: (0, j + f_tiles))
    wv = pl.BlockSpec((9, LANE), lambda j: (0, j))
    wg = pl.BlockSpec((9, LANE), lambda j: (0, j + f_tiles))
    return zv, zg, wv, wg


def _conv_fwd_call(z, wc, n_ctx, n_lat):
    n, f2 = z.shape
    ft = f2 // 2 // LANE
    geo = _ConvGeom(n_ctx, n_lat)

    def body(zv_ref, zg_ref, wv_ref, wg_ref, o_ref, cv_ref, cg_ref, bv, bg):
        wv, wg = _taps(wv_ref), _taps(wg_ref)
        for src, buf in ((zv_ref, bv), (zg_ref, bg)):
            geo.zero_gaps(buf)
            geo.fill(src, buf)

        def emit(rows, cv, cg):
            o_ref[rows, :] = (cv * cg * _sigmoid(cg)).astype(o_ref.dtype)
            cv_ref[rows, :] = cv.astype(cv_ref.dtype)
            cg_ref[rows, :] = cg.astype(cg_ref.dtype)

        emit(slice(0, n_ctx), geo.ctx_conv(geo.ctx_window(bv), wv), geo.ctx_conv(geo.ctx_window(bg), wg))

        def chunk(i, c):
            off = pl.multiple_of(n_ctx + i * geo.r, SUBLANE)
            emit(pl.ds(off, geo.r), geo.lat_conv(geo.lat_window(bv, i), wv), geo.lat_conv(geo.lat_window(bg, i), wg))
            return c
        lax.fori_loop(0, n_lat // geo.r, chunk, 0)

    tile = pl.BlockSpec((n, LANE), lambda j: (0, j))
    return _pcall(body, name="conv_gate", grid=(ft,), in_specs=list(_conv_specs(n, ft)), out_specs=[tile] * 3,
                  out_shape=[jax.ShapeDtypeStruct((n, f2 // 2), BF16)] * 3,
                  scratch_shapes=[pltpu.VMEM((geo.nbuf, LANE), F32)] * 2,
                  compiler_params=_params(("parallel",)))(z, z, wc, wc)


def _conv_bwd_call(z, wc, cv, cg, da, n_ctx, n_lat):
    n, f2 = z.shape
    ft = f2 // 2 // LANE
    geo = _ConvGeom(n_ctx, n_lat)
    r, n16 = geo.r, geo.n16

    def body(zv_ref, zg_ref, wv_ref, wg_ref, cv_ref, cg_ref, da_ref, dzv_ref, dzg_ref, dwv_ref, dwg_ref,
             bv, bg, dv, dg):
        wv, wg = _taps(wv_ref), _taps(wg_ref)
        for buf in (bv, bg, dv, dg):
            geo.zero_gaps(buf)
        geo.fill(zv_ref, bv)
        geo.fill(zg_ref, bg)

        def gate_grads(cv, cg, d):
            sg = _sigmoid(cg)
            return d * cg * sg, d * cv * sg * (1.0 + cg * (1.0 - sg))

        def tap_sums(d_c, srcs, pad):
            zeros = jnp.zeros((SUBLANE, LANE), F32)
            dce = jnp.concatenate([zeros, d_c, zeros], axis=0)
            m = dce.shape[0]
            shifted = (pltpu.roll(dce, m - 1, 0), dce, pltpu.roll(dce, 1, 0))
            out = []
            for di in range(3):
                for dj in range(3):
                    src = srcs[dj] if pad is None else geo.row_slice(srcs[dj], di, 1)
                    out.append(jnp.sum(shifted[dj] * src, axis=0, keepdims=True))
            return out

        winv, wing = geo.ctx_window(bv), geo.ctx_window(bg)
        d_cv, d_cg = gate_grads(cv_ref[0:n_ctx, :].astype(F32), cg_ref[0:n_ctx, :].astype(F32),
                                da_ref[0:n_ctx, :].astype(F32))
        dv[geo.ctx0:geo.ctx0 + n_ctx, :] = d_cv
        dg[geo.ctx0:geo.ctx0 + n_ctx, :] = d_cg
        zero_row = jnp.zeros((1, LANE), F32)
        acc0 = []
        for d_c, win in ((d_cv, winv), (d_cg, wing)):
            sums = tap_sums(d_c, (win, win, win), None)
            acc0 += [zero_row] * 3 + sums[3:6] + [zero_row] * 3

        def chunk(i, acc):
            winv, wing = geo.lat_window(bv, i), geo.lat_window(bg, i)
            off = pl.multiple_of(n_ctx + i * r, SUBLANE)
            d_cv, d_cg = gate_grads(cv_ref[pl.ds(off, r), :].astype(F32), cg_ref[pl.ds(off, r), :].astype(F32),
                                    da_ref[pl.ds(off, r), :].astype(F32))
            dst = pl.multiple_of(geo.lat0 + i * r, SUBLANE)
            dv[pl.ds(dst, r), :] = d_cv
            dg[pl.ds(dst, r), :] = d_cg
            sums = tap_sums(d_cv, geo.lat_sources(winv), True) + tap_sums(d_cg, geo.lat_sources(wing), True)
            return tuple(a + s for a, s in zip(acc, sums))
        acc = lax.fori_loop(0, n_lat // r, chunk, tuple(acc0))
        for k in range(9):
            dwv_ref[k:k + 1, :] = acc[k]
            dwg_ref[k:k + 1, :] = acc[9 + k]

        for dbuf, w, dz_ref in ((dv, wv, dzv_ref), (dg, wg, dzg_ref)):
            dz_ref[0:n_ctx, :] = geo.ctx_conv(geo.ctx_window(dbuf), w, transpose=True).astype(dz_ref.dtype)

            def chunk_t(i, c, dbuf=dbuf, w=w, dz_ref=dz_ref):
                off = pl.multiple_of(n_ctx + i * r, SUBLANE)
                dz_ref[pl.ds(off, r), :] = geo.lat_conv_t(geo.lat_window(dbuf, i), w).astype(dz_ref.dtype)
                return c
            lax.fori_loop(0, n_lat // r, chunk_t, 0)

    tile = pl.BlockSpec((n, LANE), lambda j: (0, j))
    wtile = pl.BlockSpec((9, LANE), lambda j: (0, j))
    dzv, dzg, dwv, dwg = _pcall(
        body, name="conv_gate_bwd", grid=(ft,), in_specs=list(_conv_specs(n, ft)) + [tile] * 3,
        out_specs=[tile, tile, wtile, wtile],
        out_shape=[jax.ShapeDtypeStruct((n, f2 // 2), z.dtype)] * 2 + [jax.ShapeDtypeStruct((9, f2 // 2), F32)] * 2,
        scratch_shapes=[pltpu.VMEM((geo.nbuf, LANE), F32)] * 4,
        compiler_params=_params(("parallel",)))(z, z, wc, wc, cv, cg, da)
    return jnp.concatenate([dzv, dzg], axis=1), jnp.concatenate([dwv, dwg], axis=1)


def _make_convgate(n_ctx, n_lat):
    @jax.custom_vjp
    def conv(z, wc):
        return _conv_fwd_call(z, wc, n_ctx, n_lat)[0]

    def fwd(z, wc):
        a, cv, cg = _conv_fwd_call(z, wc, n_ctx, n_lat)
        return a, (z, wc, cv, cg)

    def bwd(res, da):
        return _conv_bwd_call(*res, da, n_ctx, n_lat)

    conv.defvjp(fwd, bwd)
    return conv


GROUPS_PER_BLOCK = LANE // SSM_GROUP
STATE_BLOCK = GROUPS_PER_BLOCK * SSM_STATE
SCAN_LANES = STATE_BLOCK


def _cmul(ar, ai, br, bi):
    return ar * br - ai * bi, ar * bi + ai * br


def _lam_tables(lr, li, asc):
    row = lax.broadcasted_iota(jnp.int32, (SUBLANE, lr.shape[1]), 0)
    l1 = (jnp.broadcast_to(lr, row.shape), jnp.broadcast_to(li, row.shape))
    l2 = _cmul(*l1, *l1)
    l4 = _cmul(*l2, *l2)
    pw = l1
    pr = jnp.zeros(row.shape, F32)
    pi = jnp.zeros(row.shape, F32)
    for e in range(1, SUBLANE + 1):
        s = e - 1 if asc else SUBLANE - e
        pr = jnp.where(row == s, pw[0], pr)
        pi = jnp.where(row == s, pw[1], pi)
        pw = _cmul(*pw, *l1)

    def masked(lam_k, k):
        keep = (row >= k) if asc else (row < SUBLANE - k)
        return jnp.where(keep, lam_k[0], 0.0), jnp.where(keep, lam_k[1], 0.0)

    return masked(l1, 1), masked(l2, 2), masked(l4, 4), (pr, pi)


def _tile_scan(br, bi, cr, ci, tables, asc):
    hr, hi = br, bi
    for k, lam_k in zip((1, 2, 4), tables[:3]):
        shift = k if asc else SUBLANE - k
        mr, mi = _cmul(*lam_k, pltpu.roll(hr, shift, 0), pltpu.roll(hi, shift, 0))
        hr, hi = hr + mr, hi + mi
    mr, mi = _cmul(*tables[3], jnp.broadcast_to(cr, br.shape), jnp.broadcast_to(ci, br.shape))
    hr, hi = hr + mr, hi + mi
    last = SUBLANE - 1 if asc else 0
    return hr, hi, hr[last:last + 1, :], hi[last:last + 1, :]


def _chunk_in_time_order(k, n_chunks, asc, adjoint):
    if asc:
        return n_chunks - 1 - k if adjoint else k
    if adjoint:
        return jnp.where(k == n_chunks - 1, 0, k + 1)
    return jnp.where(k == 0, 0, n_chunks - k)


def _dot(a, b):
    return jnp.dot(a, b, preferred_element_type=F32)


def _dot_nt(a, b):
    return lax.dot_general(a, b, (((1,), (1,)), ((), ())), preferred_element_type=F32)


def _scan_chunk(r_buf, i_buf, base, rows, carry, tables, asc, lam_grad=None):
    tiles = rows // SUBLANE
    out_carry, grads = [], []
    for h in range(STATE_BLOCK // SCAN_LANES):
        cols = slice(h * SCAN_LANES, (h + 1) * SCAN_LANES)

        def tile(kt, c, h=h, cols=cols):
            pt = kt if asc else tiles - 1 - kt
            t0 = pl.multiple_of(base + pt * SUBLANE, SUBLANE)
            sr, si, ncr, nci = _tile_scan(r_buf[pl.ds(t0, SUBLANE), cols], i_buf[pl.ds(t0, SUBLANE), cols],
                                          c[0], c[1], tables[h], asc)
            r_buf[pl.ds(t0, SUBLANE), cols] = sr
            i_buf[pl.ds(t0, SUBLANE), cols] = si
            if lam_grad is None:
                return ncr, nci
            h_r, h_i, h_base = lam_grad
            g0 = pl.multiple_of(h_base + pt * SUBLANE, SUBLANE)
            pr, pi = h_r[pl.ds(g0, SUBLANE), cols], h_i[pl.ds(g0, SUBLANE), cols]
            row = lax.broadcasted_iota(jnp.int32, sr.shape, 0)
            if asc:
                nr = jnp.where(row == 0, jnp.broadcast_to(c[0], sr.shape), pltpu.roll(sr, 1, 0))
                ni = jnp.where(row == 0, jnp.broadcast_to(c[1], sr.shape), pltpu.roll(si, 1, 0))
            else:
                nr = jnp.where(row == SUBLANE - 1, jnp.broadcast_to(c[0], sr.shape), pltpu.roll(sr, SUBLANE - 1, 0))
                ni = jnp.where(row == SUBLANE - 1, jnp.broadcast_to(c[1], sr.shape), pltpu.roll(si, SUBLANE - 1, 0))
            return ncr, nci, c[2] + nr * pr + ni * pi, c[3] + ni * pr - nr * pi

        init = (carry[2 * h], carry[2 * h + 1])
        if lam_grad is not None:
            zero = jnp.zeros((SUBLANE, SCAN_LANES), F32)
            init = init + (zero, zero)
        res = lax.fori_loop(0, tiles, tile, init, unroll=2)
        out_carry += [res[0], res[1]]
        grads.append(res[2:])
    return tuple(out_carry), grads


def _ssm_specs(n):
    tok = pl.BlockSpec((n, LANE), lambda q: (0, q))
    lam = pl.BlockSpec((1, STATE_BLOCK), lambda q: (0, q))
    w_in = pl.BlockSpec((None, LANE, STATE_BLOCK), lambda q: (q, 0, 0))
    w_out = pl.BlockSpec((None, STATE_BLOCK, LANE), lambda q: (q, 0, 0))
    return tok, lam, w_in, w_out


def _zero_carry():
    return tuple(jnp.zeros((1, SCAN_LANES), F32) for _ in range(2 * (STATE_BLOCK // SCAN_LANES)))


def _half_tables(lr_ref, li_ref, asc, conj):
    out = []
    for h in range(STATE_BLOCK // SCAN_LANES):
        cols = slice(h * SCAN_LANES, (h + 1) * SCAN_LANES)
        li = li_ref[:, cols]
        out.append(_lam_tables(lr_ref[:, cols], -li if conj else li, asc))
    return out


def _ssm_fwd_call(u, lr, li, b_re, b_im, c_re, c_imn, rc, asc, side=None):
    n, cs = u.shape
    nq, nchunks = cs // LANE, n // rc
    tok, lam, w_in, w_out = _ssm_specs(n)

    def body(u_ref, lr_ref, li_ref, br_ref, bi_ref, cr_ref, ci_ref, y_ref, h_r, h_i):
        tables = _half_tables(lr_ref, li_ref, asc, False)
        wbr, wbi = br_ref[...].astype(BF16), bi_ref[...].astype(BF16)
        wcr, wci = cr_ref[...].astype(BF16), ci_ref[...].astype(BF16)

        def chunk(k, carry):
            r0 = pl.multiple_of(_chunk_in_time_order(k, nchunks, asc, False) * rc, rc)
            ub = u_ref[pl.ds(r0, rc), :].astype(BF16)
            h_r[...] = _dot(ub, wbr)
            h_i[...] = _dot(ub, wbi)
            carry, _ = _scan_chunk(h_r, h_i, 0, rc, carry, tables, asc)
            y_ref[pl.ds(r0, rc), :] = _dot(h_r[...].astype(BF16), wcr) + _dot(h_i[...].astype(BF16), wci)
            return carry
        lax.fori_loop(0, nchunks, chunk, _zero_carry())

    return _call_with_side(body, name="s5_head", grid=(nq,), in_specs=[tok, lam, lam, w_in, w_in, w_out, w_out],
                           out_specs=[tok], out_shape=[jax.ShapeDtypeStruct((n, cs), F32)],
                           scratch_shapes=[pltpu.VMEM((rc, STATE_BLOCK), F32)] * 2,
                           args=(u, lr, li, b_re, b_im, c_re, c_imn), side=side, semantics=("parallel",))


def _ssm_bwd_call(u, dy, lr, li, b_re, b_im, c_re, c_imn, rc, asc, side=None):
    n, cs = u.shape
    nq, nchunks = cs // LANE, n // rc
    tok, lam, w_in, w_out = _ssm_specs(n)

    def body(u_ref, dy_ref, lr_ref, li_ref, br_ref, bi_ref, cr_ref, ci_ref,
             du_ref, glr_ref, gli_ref, dbr_ref, dbi_ref, dcr_ref, dci_ref, h_r, h_i, a_r, a_i):
        wbr, wbi = br_ref[...].astype(BF16), bi_ref[...].astype(BF16)
        wcr, wci = cr_ref[...].astype(BF16), ci_ref[...].astype(BF16)

        tables = _half_tables(lr_ref, li_ref, asc, False)

        def chunk(k, carry):
            r0 = pl.multiple_of(_chunk_in_time_order(k, nchunks, asc, False) * rc, rc)
            ub = u_ref[pl.ds(r0, rc), :].astype(BF16)
            h_r[pl.ds(r0, rc), :] = _dot(ub, wbr)
            h_i[pl.ds(r0, rc), :] = _dot(ub, wbi)
            carry, _ = _scan_chunk(h_r, h_i, r0, rc, carry, tables, asc)
            return carry
        lax.fori_loop(0, nchunks, chunk, _zero_carry())

        adj = _half_tables(lr_ref, li_ref, not asc, True)
        for ref in (glr_ref, gli_ref, dbr_ref, dbi_ref, dcr_ref, dci_ref):
            ref[...] = jnp.zeros_like(ref)

        def chunk_adj(k, carry):
            r0 = pl.multiple_of(_chunk_in_time_order(k, nchunks, asc, True) * rc, rc)
            dyc = dy_ref[pl.ds(r0, rc), :]
            dyb = dyc.astype(BF16)
            a_r[...] = _dot_nt(dyb, wcr)
            a_i[...] = _dot_nt(dyb, wci)
            carry, grads = _scan_chunk(a_r, a_i, 0, rc, carry, adj, not asc, lam_grad=(h_r, h_i, r0))
            for h, (gr, gi) in enumerate(grads):
                cols = slice(h * SCAN_LANES, (h + 1) * SCAN_LANES)
                glr_ref[:, cols] += gr
                gli_ref[:, cols] += gi
            ab_r, ab_i = a_r[...].astype(BF16), a_i[...].astype(BF16)
            du_ref[pl.ds(r0, rc), :] = _dot_nt(ab_r, wbr) + _dot_nt(ab_i, wbi)
            ut = u_ref[pl.ds(r0, rc), :].T.astype(BF16)
            dbr_ref[...] += _dot(ut, ab_r)
            dbi_ref[...] += _dot(ut, ab_i)
            dyt = dyc.T.astype(BF16)
            dcr_ref[...] += _dot(dyt, h_r[pl.ds(r0, rc), :].astype(BF16))
            dci_ref[...] += _dot(dyt, h_i[pl.ds(r0, rc), :].astype(BF16))
            return carry
        lax.fori_loop(0, nchunks, chunk_adj, _zero_carry())

    part = pl.BlockSpec((SUBLANE, STATE_BLOCK), lambda q: (0, q))
    w_states = lr.shape[1]
    return _call_with_side(
        body, name="s5_head_bwd", grid=(nq,), in_specs=[tok, tok, lam, lam, w_in, w_in, w_out, w_out],
        out_specs=[tok, part, part, w_in, w_in, w_in, w_in],
        out_shape=[jax.ShapeDtypeStruct((n, cs), F32)] + [jax.ShapeDtypeStruct((SUBLANE, w_states), F32)] * 2
        + [jax.ShapeDtypeStruct((nq, LANE, STATE_BLOCK), F32)] * 4,
        scratch_shapes=[pltpu.VMEM((n, STATE_BLOCK), F32)] * 2 + [pltpu.VMEM((rc, STATE_BLOCK), F32)] * 2,
        args=(u, dy, lr, li, b_re, b_im, c_re, c_imn), side=side, semantics=("parallel",))


def _ssm_cotangents(outs):
    du, glr, gli, dbr, dbi, dcr, dci = outs
    return (du, jnp.sum(glr, axis=0, keepdims=True), jnp.sum(gli, axis=0, keepdims=True), dbr, dbi,
            jnp.swapaxes(dcr, 1, 2), jnp.swapaxes(dci, 1, 2))


def _make_ssm(rc, asc):
    @jax.custom_vjp
    def ssm(u, lr, li, b_re, b_im, c_re, c_imn):
        return _ssm_fwd_call(u, lr, li, b_re, b_im, c_re, c_imn, rc, asc)[0]

    def fwd(*args):
        return ssm(*args), args

    def bwd(args, dy):
        return _ssm_cotangents(_ssm_bwd_call(args[0], dy, *args[1:], rc, asc))

    ssm.defvjp(fwd, bwd)
    return ssm


def _make_hosting_ssm(rc, asc, name):
    @jax.custom_vjp
    def ssm(u, lr, li, b_re, b_im, c_re, c_imn, shard):
        sh = shard.astype(BF16)
        side = (_gather_parts, sh, jax.ShapeDtypeStruct((N_DEV,) + sh.shape, BF16))
        return tuple(_ssm_fwd_call(u, lr, li, b_re, b_im, c_re, c_imn, rc, asc, side=side))

    def fwd(*args):
        return ssm(*args), args[:-1]

    def bwd(args, cts):
        dy, d_whole = cts
        side = (_scatter_parts, d_whole, jax.ShapeDtypeStruct(d_whole.shape, d_whole.dtype))
        outs = _ssm_bwd_call(args[0], dy, *args[1:], rc, asc, side=side)
        return _ssm_cotangents(outs[:-1]) + (_sum8_call(outs[-1], name + "_sum"),)

    ssm.defvjp(fwd, bwd)
    return ssm


def _blocks_in(bb):
    g, p, h = bb.shape
    k = GROUPS_PER_BLOCK
    out = jnp.einsum('qgph,gk->qghkp', bb.reshape(g // k, k, p, h), jnp.eye(k, dtype=F32))
    return out.reshape(g // k, k * h, k * p)


def _blocks_out(cc):
    g, h, p = cc.shape
    k = GROUPS_PER_BLOCK
    out = jnp.einsum('qghp,gk->qgpkh', cc.reshape(g // k, k, h, p), jnp.eye(k, dtype=F32))
    return out.reshape(g // k, k * p, k * h)


def _position():
    return lax.axis_index("x"), lax.axis_index("y"), lax.axis_index("c")


def _linear_index():
    x, y, c = _position()
    return 4 * x + 2 * y + c


COMM_SCRATCH = [pltpu.SemaphoreType.DMA((7,)), pltpu.SemaphoreType.DMA((7,)), pltpu.SemaphoreType.DMA(())]
ANY_SPEC = pl.BlockSpec(memory_space=pl.ANY)


def _gather_parts(x_ref, o_ref, send_sems, recv_sems, local_sem):
    x, y, c = _position()
    me, sibling = (x, y, c), (x, y, 1 - c)
    chips = [(1 - x, y), (x, 1 - y), (1 - x, 1 - y)]

    def block(px, py, pc):
        return o_ref.at[4 * px + 2 * py + pc]

    def copy(k, blk, to, src=None):
        return pltpu.make_async_remote_copy(
            src_ref=block(*blk) if src is None else src, dst_ref=block(*blk),
            send_sem=send_sems.at[k], recv_sem=recv_sems.at[k], device_id=to, device_id_type=MESH)

    mine = pltpu.make_async_copy(x_ref, block(*me), local_sem)
    first = [copy(0, me, sibling, src=x_ref)]
    first += [copy(1 + j, me, (*chip, c), src=x_ref) for j, chip in enumerate(chips)]
    passed = [copy(4 + j, (*chip, c), sibling) for j, chip in enumerate(chips)]

    def start():
        mine.start()
        for cp in first:
            cp.start()

    def finish():
        for j, chip in enumerate(chips):
            copy(1 + j, (*chip, c), me).wait_recv()
            passed[j].start()
        copy(0, sibling, me).wait_recv()
        for j, chip in enumerate(chips):
            copy(4 + j, (*chip, 1 - c), me).wait_recv()
        for cp in first + passed:
            cp.wait_send()
        mine.wait()

    return start, finish


def _scatter_parts(g_ref, o_ref, send_sems, recv_sems, local_sem, rows=None):
    x, y, c = _position()

    def block(p):
        return g_ref.at[p] if rows is None else g_ref.at[p, pl.ds(rows[0], rows[1])]

    mine = pltpu.make_async_copy(block(4 * x + 2 * y + c), o_ref.at[0], local_sem)
    copies = []
    for k in range(1, N_DEV):
        px = 1 - x if k & 4 else x
        py = 1 - y if k & 2 else y
        pc = 1 - c if k & 1 else c
        copies.append(pltpu.make_async_remote_copy(
            src_ref=block(4 * px + 2 * py + pc), dst_ref=o_ref.at[k],
            send_sem=send_sems.at[k - 1], recv_sem=recv_sems.at[k - 1],
            device_id=(px, py, pc), device_id_type=MESH))

    def start():
        mine.start()
        for cp in copies:
            cp.start()

    def finish():
        for cp in copies:
            cp.wait()
        mine.wait()

    return start, finish


def _comm_call(parts, src, out_shape, name):
    def body(s_ref, o_ref, *sems):
        start, finish = parts(s_ref, o_ref, *sems)
        start()
        finish()

    return _pcall(body, name=name, in_specs=[ANY_SPEC], out_specs=ANY_SPEC, out_shape=out_shape,
                  scratch_shapes=COMM_SCRATCH)(src)


def _ag_call(shard, name):
    return _comm_call(_gather_parts, shard, jax.ShapeDtypeStruct((N_DEV,) + shard.shape, shard.dtype), name)


def _rs_call(g, name):
    return _comm_call(_scatter_parts, g, jax.ShapeDtypeStruct(g.shape, g.dtype), name)


def _sum8_call(parts, name):
    _, r, c = parts.shape
    tr = _tile(r, max(PACK, (4 << 20) // (N_DEV * c * parts.dtype.itemsize) // PACK * PACK), PACK)

    def body(p_ref, o_ref):
        acc = p_ref[0].astype(F32)
        for k in range(1, N_DEV):
            acc = acc + p_ref[k].astype(F32)
        o_ref[...] = acc

    return _pcall(body, name=name, grid=(r // tr,),
                  in_specs=[pl.BlockSpec((N_DEV, tr, c), lambda i: (0, i, 0))],
                  out_specs=pl.BlockSpec((tr, c), lambda i: (i, 0)),
                  out_shape=jax.ShapeDtypeStruct((r, c), F32), compiler_params=_params(("parallel",)))(parts)


def _reduce_scatter(g, name):
    return _sum8_call(_rs_call(g, name), name + "_sum")


def _make_gather(dtype, name):
    @jax.custom_vjp
    def gather(shard):
        return _ag_call(shard.astype(dtype), name)

    def fwd(shard):
        return gather(shard), None

    def bwd(_, ct):
        return (_reduce_scatter(ct, name + "_rs"),)

    gather.defvjp(fwd, bwd)
    return gather


def _adam_call(w, g, m, v, name):
    r, c = w.shape
    tr = _tile(r, max(SUBLANE, (1 << 20) // (4 * c) // SUBLANE * SUBLANE), SUBLANE)

    def body(w_ref, g_ref, m_ref, v_ref, d_ref, mo_ref, vo_ref):
        gv = g_ref[...]
        m2 = ADAM_B1 * m_ref[...] + (1.0 - ADAM_B1) * gv
        v2 = ADAM_B2 * v_ref[...] + (1.0 - ADAM_B2) * (gv * gv)
        m_hat = m2 / (1.0 - ADAM_B1 ** ADAM_STEP)
        v_hat = v2 / (1.0 - ADAM_B2 ** ADAM_STEP)
        d_ref[...] = -ADAM_LR * (m_hat / (jnp.sqrt(v_hat) + ADAM_EPS) + ADAM_WD * w_ref[...])
        mo_ref[...] = m2
        vo_ref[...] = v2

    spec = pl.BlockSpec((tr, c), lambda i: (i, 0))
    return _pcall(body, name=name, grid=(r // tr,), in_specs=[spec] * 4, out_specs=[spec] * 3,
                  out_shape=[jax.ShapeDtypeStruct((r, c), F32)] * 3, compiler_params=_params(("parallel",)))(w, g, m, v)


def _loss_call(y, target, tr):
    n, d = y.shape

    def body(y_ref, t_ref, s_ref, dy_ref):
        i = pl.program_id(0)
        err = y_ref[...] - t_ref[...]
        dy_ref[...] = err * (1.0 / d)
        part = jnp.sum(jnp.sum(err * err, axis=1, keepdims=True), axis=0, keepdims=True)

        @pl.when(i == 0)
        def _():
            s_ref[...] = jnp.zeros_like(s_ref)
        s_ref[...] += part

    spec = pl.BlockSpec((tr, d), lambda i: (i, 0))
    return _pcall(body, name="loss_head", grid=(n // tr,), in_specs=[spec, spec],
                  out_specs=[pl.BlockSpec((1, 1), lambda i: (0, 0)), spec],
                  out_shape=[jax.ShapeDtypeStruct((1, 1), F32), jax.ShapeDtypeStruct((n, d), F32)],
                  compiler_params=_params(("arbitrary",)))(y, target)


def _pack(arrays, rows_mult):
    flat = jnp.concatenate([a.reshape(-1).astype(F32) for a in arrays])
    rows = -(-flat.shape[0] // LANE)
    rows = -(-rows // rows_mult) * rows_mult
    return jnp.pad(flat, (0, rows * LANE - flat.shape[0])).reshape(rows, LANE)


def _unpack(buf, shapes, lead=()):
    flat = buf.reshape(lead + (-1,))
    out, pos = [], 0
    for s in shapes:
        size = math.prod(s)
        out.append(flat[..., pos:pos + size].reshape(lead + tuple(s)))
        pos += size
    return out


def _s5_discretise(a_re, a_im, log_dt, b_re, b_im):
    dt = jnp.exp(log_dt)[:, None]
    mag = jnp.exp(a_re * dt)
    lam_re = mag * jnp.cos(a_im * dt)
    lam_im = mag * jnp.sin(a_im * dt)
    denom = a_re * a_re + a_im * a_im
    nr, ni = lam_re - 1.0, lam_im
    f_re = (nr * a_re + ni * a_im) / denom
    f_im = (ni * a_re - nr * a_im) / denom
    bb_re = f_re[..., None] * b_re - f_im[..., None] * b_im
    bb_im = f_re[..., None] * b_im + f_im[..., None] * b_re
    return lam_re, lam_im, bb_re, bb_im


def _forward(x, p, ctx, s_c, n_ctx, n_lat):
    d_model = x.shape[1]
    depth = len(p['w_in'])
    mix = p['w_in'][0].shape[1]
    pool_width = p['pool_scale'].shape[1]
    tr = n_ctx
    me = _linear_index()

    rmsmod = _make_rowop(_fn_rmsmod, 1, 2, 1, [BF16], tr, "rms_modulate")
    resid = _make_rowop(_fn_resid, 2, 1, 1, [F32], tr, "gated_residual")
    gelu_head = _make_rowop(_fn_gelu_head, 2, 0, 1, [F32], tr, "ssm_gelu")
    mixcat = _make_rowop(_fn_mixcat, 6, 0, 1, [BF16], tr, "mix_concat")
    pool = _make_pool(n_ctx, n_lat, pool_width, mix)
    convgate = _make_convgate(n_ctx, n_lat)
    ssm = [_make_ssm(tr, True), _make_ssm(tr, False)]
    lin_f32 = {k: _make_linear(F32, k) for k in ("w_in", "w_pool", "w_glu", "w_out", "w_down", "ada")}
    lin_up = _make_linear(BF16, "w_up")
    gather_big = {k: _make_gather(BF16, "gather_" + k) for k in BIG_SHARDED}
    gather_small = _make_gather(F32, "gather_small")
    gather_mod = _make_gather(F32, "gather_mod")

    whole = {k: gather_big[k](p[k][0]) for k in ('w_in', 'w_out')}
    host = {k: _make_hosting_linear(BF16 if k == 'w_up' else F32, k) for k in BIG_SHARDED}
    ssm_host = [_make_hosting_ssm(tr, True, "s5_w_up"), _make_hosting_ssm(tr, False, "s5_w_down")]

    def big_linear(k, a, l):
        w = whole[k] if k == 'w_up' else whole[k].reshape(-1, whole[k].shape[2])
        if l + 1 == depth:
            return (lin_up if k == 'w_up' else lin_f32[k])(a, w)
        y, whole[k] = host[k](a, w, p[k][l + 1])
        return y

    small = [p[k] for k in SMALL_SHARDED]
    packed = gather_small(_pack(small, PACK))
    w_pool, w_glu, w_conv = _unpack(packed, [a.shape for a in small], lead=(N_DEV,))
    w_pool = jnp.moveaxis(w_pool, 0, 2).reshape(depth, len(POOL_WINDOWS), -1, w_pool.shape[-1])
    w_glu = jnp.moveaxis(w_glu, 0, 1).reshape(depth, -1, w_glu.shape[-1])
    w_conv = jnp.moveaxis(w_conv, 0, 3).reshape(depth, 9, -1)

    s_rows = jnp.concatenate([s_c, jax.nn.silu(p['c_ctx'])[None, :],
                              jnp.zeros((PACK - N_DEV - 1, d_model), F32)], axis=0)
    cols = p['w_ada'][0].shape[1]
    b_loc = lax.dynamic_slice_in_dim(p['b_ada'], me * cols, cols, axis=1)
    mod_loc = jnp.stack([lin_f32['ada'](s_rows, p['w_ada'][l]) + b_loc[l][None, :] for l in range(depth)])
    mod = gather_mod(mod_loc.reshape(depth * PACK, cols)).reshape(N_DEV, depth, PACK, cols)
    mod = jnp.moveaxis(mod, 0, 2).reshape(depth, PACK, 6, d_model)
    mod_lat = lax.dynamic_index_in_dim(mod, me, axis=1, keepdims=False)
    mod_ctx = mod[:, N_DEV]

    xs = jnp.concatenate([ctx, x], axis=0)
    for l in range(depth):
        def seg(k):
            return jnp.stack([mod_ctx[l, k], mod_lat[l, k]]).reshape(2, 1, d_model)

        def row(name):
            return p[name][l].reshape(1, -1)

        h1, = rmsmod(xs, seg(0), seg(1), row('g_pre_mix'))
        u = big_linear('w_in', h1, l)
        pooled = pool(u)
        yp = [lin_f32['w_pool'](pooled[g], w_pool[l, g]) for g in range(len(POOL_WINDOWS))]
        u_ssm = u[:, pool_width:]
        ys = []
        for d in range(2):
            lam_re, lam_im, bb_re, bb_im = _s5_discretise(
                p['ssm_a_re'][l, d], p['ssm_a_im'][l, d], p['ssm_log_dt'][l, d], p['ssm_b_re'][l, d], p['ssm_b_im'][l, d])
            args = (u_ssm, lam_re.reshape(1, -1), lam_im.reshape(1, -1), _blocks_in(bb_re), _blocks_in(bb_im),
                    _blocks_out(p['ssm_c_re'][l, d]), _blocks_out(-p['ssm_c_im'][l, d]))
            if l == 0:
                k = ('w_up', 'w_down')[d]
                y_dir, whole[k] = ssm_host[d](*args, p[k][0])
                ys.append(y_dir)
            else:
                ys.append(ssm[d](*args))
        gl, = gelu_head(ys[0] + ys[1], u_ssm, row('ssm_d'))
        s = lin_f32['w_glu'](gl, w_glu[l])
        cat, = mixcat(*yp, gl, s, row('pool_scale'))
        mixed = big_linear('w_out', cat, l)
        x1, = resid(xs, mixed, seg(2), row('g_post_mix'))
        h2, = rmsmod(x1, seg(3), seg(4), row('g_pre_ffn'))
        z = big_linear('w_up', h2, l)
        a = convgate(z, w_conv[l])
        f = big_linear('w_down', a, l)
        xs, = resid(x1, f, seg(5), row('g_post_ffn'))
    return xs[n_ctx:]


def _as_rows(a):
    return a.reshape(-1, a.shape[-1])


def kernel(x, c, ctx, c_ctx, w_ada, b_ada, w_in, w_pool, pool_scale, ssm_a_re, ssm_a_im, ssm_log_dt, ssm_b_re, ssm_b_im, ssm_c_re, ssm_c_im, ssm_d, w_glu, w_out, g_pre_mix, g_post_mix, g_pre_ffn, g_post_ffn, w_up, w_conv, w_down, loss_target, m_c_ctx, m_w_ada, m_b_ada, m_w_in, m_w_pool, m_pool_scale, m_ssm_a_re, m_ssm_a_im, m_ssm_log_dt, m_ssm_b_re, m_ssm_b_im, m_ssm_c_re, m_ssm_c_im, m_ssm_d, m_w_glu, m_w_out, m_g_pre_mix, m_g_post_mix, m_g_pre_ffn, m_g_post_ffn, m_w_up, m_w_conv, m_w_down, v_c_ctx, v_w_ada, v_b_ada, v_w_in, v_w_pool, v_pool_scale, v_ssm_a_re, v_ssm_a_im, v_ssm_log_dt, v_ssm_b_re, v_ssm_b_im, v_ssm_c_re, v_ssm_c_im, v_ssm_d, v_w_glu, v_w_out, v_g_pre_mix, v_g_post_mix, v_g_pre_ffn, v_g_post_ffn, v_w_up, v_w_conv, v_w_down):
    weights = dict(zip(WEIGHTS, (c_ctx, w_ada, b_ada, w_in, w_pool, pool_scale, ssm_a_re, ssm_a_im, ssm_log_dt, ssm_b_re,
                                 ssm_b_im, ssm_c_re, ssm_c_im, ssm_d, w_glu, w_out, g_pre_mix, g_post_mix, g_pre_ffn,
                                 g_post_ffn, w_up, w_conv, w_down)))
    m_in = dict(zip(WEIGHTS, (m_c_ctx, m_w_ada, m_b_ada, m_w_in, m_w_pool, m_pool_scale, m_ssm_a_re, m_ssm_a_im,
                              m_ssm_log_dt, m_ssm_b_re, m_ssm_b_im, m_ssm_c_re, m_ssm_c_im, m_ssm_d, m_w_glu, m_w_out,
                              m_g_pre_mix, m_g_post_mix, m_g_pre_ffn, m_g_post_ffn, m_w_up, m_w_conv, m_w_down)))
    v_in = dict(zip(WEIGHTS, (v_c_ctx, v_w_ada, v_b_ada, v_w_in, v_w_pool, v_pool_scale, v_ssm_a_re, v_ssm_a_im,
                              v_ssm_log_dt, v_ssm_b_re, v_ssm_b_im, v_ssm_c_re, v_ssm_c_im, v_ssm_d, v_w_glu, v_w_out,
                              v_g_pre_mix, v_g_post_mix, v_g_pre_ffn, v_g_post_ffn, v_w_up, v_w_conv, v_w_down)))
    depth = w_in.shape[0]
    n_lat, d_model = x.shape[1], x.shape[2]
    n_ctx = ctx.shape[1]
    per_layer = BIG_SHARDED + ['w_ada']

    c_rows = jnp.concatenate([c, jnp.zeros((SUBLANE - 1, d_model), F32)], axis=0)
    s_c = jax.nn.silu(_ag_call(c_rows, "gather_c")[:, 0, :])

    params = {k: ([w[l] for l in range(depth)] if k in per_layer else w) for k, w in weights.items()}

    def run(x2d, prm):
        return _forward(x2d, prm, ctx[0], s_c, n_ctx, n_lat)

    y, vjp = jax.vjp(run, x[0], params)
    sq, dy = _loss_call(y, loss_target[0], n_ctx)
    loss = lax.psum(0.5 * sq[0, 0] / d_model, ("x", "y", "c"))
    gx, grads = vjp(dy)
    grads = {k: (jnp.stack(g) if k in per_layer else g) for k, g in grads.items()}

    rep_shapes = [weights[k].shape for k in REPLICATED]
    contrib = _pack([grads[k] for k in REPLICATED], N_DEV * PACK)
    rows = contrib.shape[0] // N_DEV
    total = _ag_call(_reduce_scatter(contrib.reshape(N_DEV, rows, LANE), "reduce_replicated"), "gather_replicated")
    total = total.reshape(N_DEV * rows, LANE)
    for k, g in zip(REPLICATED, _unpack(total, rep_shapes)):
        grads[k] = g

    delta, new_m, new_v = {}, {}, {}
    rep = [_pack([src[k] for k in REPLICATED], N_DEV * PACK) for src in (weights, m_in, v_in)]
    upd = _adam_call(rep[0], total, rep[1], rep[2], "adamw_replicated")
    for out, buf in zip((delta, new_m, new_v), upd):
        out.update(zip(REPLICATED, _unpack(buf, rep_shapes)))
    for k in WEIGHTS:
        if k in REPLICATED:
            continue
        upd = _adam_call(_as_rows(weights[k]), _as_rows(grads[k]), _as_rows(m_in[k]), _as_rows(v_in[k]), "adamw_" + k)
        for out, buf in zip((delta, new_m, new_v), upd):
            out[k] = buf.reshape(weights[k].shape)

    return (loss, gx[None], *[grads[k] for k in WEIGHTS], *[delta[k] for k in WEIGHTS],
            *[new_m[k] for k in WEIGHTS], *[new_v[k] for k in WEIGHTS])
```

```python
import functools
import math

import jax
import jax.numpy as jnp
from jax import lax
from jax.experimental import pallas as pl
from jax.experimental.pallas import tpu as pltpu

F32 = jnp.float32
BF16 = jnp.bfloat16

N_DEV = 8
GRID_W = 64
POOL_WINDOWS = (2, 4, 8, 16)
SSM_GROUP = 16
SSM_STATE = 64
EPS = 1e-6
ADAM_LR = 0.001
ADAM_B1 = 0.9
ADAM_B2 = 0.999
ADAM_EPS = 1e-08
ADAM_WD = 0.01
ADAM_STEP = 10

SUBLANE = 8
PACK = 16
LANE = 128
VMEM_LIMIT = 56 * 1024 * 1024
MM_VMEM_BUDGET = 40 * 1024 * 1024
MESH = pl.DeviceIdType.MESH

WEIGHTS = ['c_ctx', 'w_ada', 'b_ada', 'w_in', 'w_pool', 'pool_scale', 'ssm_a_re', 'ssm_a_im', 'ssm_log_dt',
           'ssm_b_re', 'ssm_b_im', 'ssm_c_re', 'ssm_c_im', 'ssm_d', 'w_glu', 'w_out', 'g_pre_mix',
           'g_post_mix', 'g_pre_ffn', 'g_post_ffn', 'w_up', 'w_conv', 'w_down']
REPLICATED = ['c_ctx', 'b_ada', 'pool_scale', 'ssm_a_re', 'ssm_a_im', 'ssm_log_dt', 'ssm_b_re', 'ssm_b_im',
              'ssm_c_re', 'ssm_c_im', 'ssm_d', 'g_pre_mix', 'g_post_mix', 'g_pre_ffn', 'g_post_ffn']
BIG_SHARDED = ['w_in', 'w_out', 'w_up', 'w_down']
SMALL_SHARDED = ['w_pool', 'w_glu', 'w_conv']


def _pcall(body, **kw):
    return pl.pallas_call(body, **kw)


def _params(sem=None):
    return pltpu.CompilerParams(dimension_semantics=sem, vmem_limit_bytes=VMEM_LIMIT)


def _tile(n, cap, mult):
    if n <= cap:
        return n
    best = None
    d = mult
    while d <= cap:
        if n % d == 0:
            best = d
        d += mult
    assert best is not None, (n, cap, mult)
    return best


def _sigmoid(x):
    return 1.0 / (1.0 + jnp.exp(-x))


def _mm(a, b, *, tb=False, out_dtype=F32, out_blocked=False, side=None, name):
    b_blocked = b.ndim == 3
    M, K = a.shape
    if b_blocked:
        nb, br, bc = b.shape
        N, Kb = (br, nb * bc) if tb else (nb * bc, br)
    else:
        N, Kb = (b.shape if tb else b.shape[::-1])
    assert K == Kb, (a.shape, b.shape, tb)
    tm = _tile(M, 1408, PACK)
    tk = bc if (b_blocked and tb) else _tile(K, 2304, LANE)

    def vmem_bytes(tn):
        tiles = tm * tk * a.dtype.itemsize + tk * tn * b.dtype.itemsize + tm * tn * jnp.dtype(out_dtype).itemsize
        return 2 * tiles + tm * tn * 4

    if out_blocked:
        assert N % N_DEV == 0
        tn = N // N_DEV
    elif b_blocked and not tb:
        tn = bc
    else:
        tn = _tile(N, 1024, LANE)
        if vmem_bytes(tn) > MM_VMEM_BUDGET:
            tn = _tile(N, 512, LANE)
    if b_blocked and not tb:
        assert tn == bc
    nm, nn, nk = M // tm, N // tn, K // tk

    a_spec = pl.BlockSpec((tm, tk), lambda i, j, k: (i, k))
    if b_blocked:
        if tb:
            b_spec = pl.BlockSpec((None, tn, tk), lambda i, j, k: (k, j, 0))
        else:
            b_spec = pl.BlockSpec((None, tk, tn), lambda i, j, k: (j, k, 0))
    else:
        b_spec = pl.BlockSpec((tn, tk), lambda i, j, k: (j, k)) if tb else pl.BlockSpec((tk, tn), lambda i, j, k: (k, j))
    if out_blocked:
        o_spec = pl.BlockSpec((None, tm, tn), lambda i, j, k: (j, i, 0))
        o_shape = jax.ShapeDtypeStruct((N_DEV, M, tn), out_dtype)
    else:
        o_spec = pl.BlockSpec((tm, tn), lambda i, j, k: (i, j))
        o_shape = jax.ShapeDtypeStruct((M, N), out_dtype)
    dims = (((1,), ((1 if tb else 0),)), ((), ()))

    def matmul(a_ref, b_ref, o_ref, acc_ref):
        k = pl.program_id(2)
        part = lax.dot_general(a_ref[...].astype(BF16), b_ref[...].astype(BF16), dims, preferred_element_type=F32)
        if nk == 1:
            o_ref[...] = part.astype(o_ref.dtype)
        else:
            @pl.when(k == 0)
            def _():
                acc_ref[...] = part

            @pl.when(k > 0)
            def _():
                acc_ref[...] += part

            @pl.when(k == nk - 1)
            def _():
                o_ref[...] = acc_ref[...].astype(o_ref.dtype)

    out = _call_with_side(matmul, name=name, grid=(nm, nn, nk), in_specs=[a_spec, b_spec], out_specs=[o_spec],
                          out_shape=[o_shape], scratch_shapes=[pltpu.VMEM((tm, tn), F32)], args=(a, b), side=side,
                          semantics=("parallel", "parallel", "arbitrary"))
    return out[0] if side is None else out


def _call_with_side(core, *, name, grid, in_specs, out_specs, out_shape, scratch_shapes, args, side, semantics):
    if side is None:
        return _pcall(core, name=name, grid=grid, in_specs=in_specs, out_specs=out_specs, out_shape=out_shape,
                      scratch_shapes=scratch_shapes, compiler_params=_params(semantics))(*args)
    parts, src, side_shape = side
    n_in, n_out, n_scr = len(in_specs), len(out_specs), len(scratch_shapes)

    def body(*refs):
        ins, s_ref = refs[:n_in], refs[n_in]
        outs, so_ref = refs[n_in + 1:n_in + 1 + n_out], refs[n_in + 1 + n_out]
        scr, sems = refs[n_in + 2 + n_out:n_in + 2 + n_out + n_scr], refs[n_in + 2 + n_out + n_scr:]
        step = 0
        for d, g in enumerate(grid):
            step = step * g + pl.program_id(d)
        start, finish = parts(s_ref, so_ref, *sems)
        pl.when(step == 0)(start)
        core(*ins, *outs, *scr)
        pl.when(step == math.prod(grid) - 1)(finish)

    return _pcall(body, name=name, grid=grid, in_specs=list(in_specs) + [ANY_SPEC],
                  out_specs=list(out_specs) + [ANY_SPEC], out_shape=list(out_shape) + [side_shape],
                  scratch_shapes=list(scratch_shapes) + COMM_SCRATCH,
                  compiler_params=_params(("arbitrary",) * len(grid)))(*args, src)


def _make_linear(out_dtype, name):
    @jax.custom_vjp
    def lin(a, w):
        return _mm(a, w, out_dtype=out_dtype, name=name)

    def fwd(a, w):
        return lin(a, w), (a, w)

    def bwd(res, dy):
        a, w = res
        da = _mm(dy, w, tb=True, out_dtype=a.dtype, name=name + "_da")
        dw = _mm(a.T, dy, out_dtype=w.dtype, out_blocked=(w.ndim == 3), name=name + "_dw")
        return da, dw

    lin.defvjp(fwd, bwd)
    return lin


def _make_hosting_linear(out_dtype, name):
    @jax.custom_vjp
    def lin(a, w, next_shard):
        nxt = next_shard.astype(BF16)
        side = (_gather_parts, nxt, jax.ShapeDtypeStruct((N_DEV,) + nxt.shape, BF16))
        return tuple(_mm(a, w, out_dtype=out_dtype, side=side, name=name + "_gather"))

    def fwd(a, w, next_shard):
        return lin(a, w, next_shard), (a, w)

    def bwd(res, cts):
        a, w = res
        dy, d_next = cts
        nb, r, c = d_next.shape
        half = r // 2

        def side(first, count):
            return (functools.partial(_scatter_parts, rows=(first, count)), d_next,
                    jax.ShapeDtypeStruct((nb, count, c), d_next.dtype))

        dw, lo = _mm(a.T, dy, out_dtype=w.dtype, out_blocked=(w.ndim == 3), side=side(0, half),
                     name=name + "_dw_scatter")
        da, hi = _mm(dy, w, tb=True, out_dtype=a.dtype, side=side(half, r - half), name=name + "_da_scatter")
        d_shard = jnp.concatenate([_sum8_call(lo, name + "_sum_lo"), _sum8_call(hi, name + "_sum_hi")], axis=0)
        return da, dw, d_shard

    lin.defvjp(fwd, bwd)
    return lin


def _make_rowop(fn, n_row, n_seg, n_bc, out_dtypes, tr, name):
    def specs(args):
        rows, segs, bcs = args[:n_row], args[n_row:n_row + n_seg], args[n_row + n_seg:]
        sp = [pl.BlockSpec((tr, r.shape[1]), lambda i: (i, 0)) for r in rows]
        sp += [pl.BlockSpec((None, 1, s.shape[2]), lambda i: (jnp.minimum(i, 1), 0, 0)) for s in segs]
        sp += [pl.BlockSpec((1, b.shape[1]), lambda i: (0, 0)) for b in bcs]
        return sp

    def out_widths(args):
        tiles = [jax.ShapeDtypeStruct((tr, a.shape[-1]), a.dtype) for a in args[:n_row]]
        tiles += [jax.ShapeDtypeStruct((1, a.shape[-1]), a.dtype) for a in args[n_row:]]
        return [o.shape[1] for o in jax.eval_shape(fn, *tiles)]

    def fwd_call(*args):
        n = args[0].shape[0]
        widths = out_widths(args)
        n_in = len(args)

        def body(*refs):
            vals = [r[...] for r in refs[:n_in]]
            outs = fn(*vals)
            for o_ref, o in zip(refs[n_in:], outs):
                o_ref[...] = o.astype(o_ref.dtype)

        return _pcall(body, name=name, grid=(n // tr,), in_specs=specs(args),
                      out_specs=[pl.BlockSpec((tr, w), lambda i: (i, 0)) for w in widths],
                      out_shape=[jax.ShapeDtypeStruct((n, w), d) for w, d in zip(widths, out_dtypes)],
                      compiler_params=_params(("parallel",)))(*args)

    def bwd_call(args, cots):
        n = args[0].shape[0]
        n_in = len(args)
        n_ct = len(cots)
        rows, segs, bcs = args[:n_row], args[n_row:n_row + n_seg], args[n_row + n_seg:]

        def body(*refs):
            i = pl.program_id(0)
            vals = [r[...] for r in refs[:n_in]]
            cts = [r[...].astype(F32) for r in refs[n_in:n_in + n_ct]]
            outs = refs[n_in + n_ct:]
            _, vjp = jax.vjp(lambda *v: tuple(fn(*v)), *vals)
            grads = vjp(tuple(cts))
            for o_ref, g in zip(outs[:n_row], grads[:n_row]):
                o_ref[...] = g.astype(o_ref.dtype)
            for o_ref, g in zip(outs[n_row:n_row + n_seg], grads[n_row:n_row + n_seg]):
                @pl.when(i <= 1)
                def _():
                    o_ref[...] = jnp.zeros_like(o_ref)
                o_ref[...] += g.astype(F32)
            for o_ref, g in zip(outs[n_row + n_seg:], grads[n_row + n_seg:]):
                @pl.when(i == 0)
                def _():
                    o_ref[...] = jnp.zeros_like(o_ref)
                o_ref[...] += g.astype(F32)

        out_specs = [pl.BlockSpec((tr, r.shape[1]), lambda i: (i, 0)) for r in rows]
        out_specs += [pl.BlockSpec((None, 1, s.shape[2]), lambda i: (jnp.minimum(i, 1), 0, 0)) for s in segs]
        out_specs += [pl.BlockSpec((1, b.shape[1]), lambda i: (0, 0)) for b in bcs]
        out_shape = [jax.ShapeDtypeStruct(r.shape, r.dtype) for r in rows]
        out_shape += [jax.ShapeDtypeStruct(s.shape, F32) for s in segs]
        out_shape += [jax.ShapeDtypeStruct(b.shape, F32) for b in bcs]
        in_specs = specs(args) + [pl.BlockSpec((tr, c.shape[1]), lambda i: (i, 0)) for c in cots]
        return _pcall(body, name=name + "_bwd", grid=(n // tr,), in_specs=in_specs, out_specs=out_specs,
                      out_shape=out_shape, compiler_params=_params(("arbitrary",)))(*args, *cots)

    @jax.custom_vjp
    def op(*args):
        return tuple(fwd_call(*args))

    def op_fwd(*args):
        return op(*args), args

    def op_bwd(args, cots):
        return tuple(bwd_call(args, list(cots)))

    op.defvjp(op_fwd, op_bwd)
    return op


def _rms(x, g):
    return x * lax.rsqrt(jnp.mean(x * x, axis=-1, keepdims=True) + EPS) * g


def _fn_rmsmod(x, shift, scale, g):
    return (_rms(x.astype(F32), g) * (1.0 + scale) + shift,)


def _fn_resid(x, m, gate, g):
    return (x + gate * _rms(m.astype(F32), g),)


def _fn_gelu_head(y, u, d):
    return (jax.nn.gelu(y + d * u.astype(F32)),)


def _fn_mixcat(y0, y1, y2, y3, gl, s, ps):
    pool = jnp.concatenate([y0, y1, y2, y3], axis=1) * ps
    return (jnp.concatenate([pool, gl * _sigmoid(s)], axis=1),)


def _pool_call(x, col_blk0, ncol, w, n_ctx, n_lat, transpose, out_dtype, name):
    n = n_ctx + n_lat
    cw = LANE
    r = n_ctx
    gap = SUBLANE
    half = w // 2
    lat0 = 2 * gap + n_ctx
    nbuf = 3 * gap + n

    def body(x_ref, o_ref, buf):
        def inv_cnt(seg_len, t0):
            t = t0 + lax.broadcasted_iota(jnp.int32, (r, 1), 0)
            cnt = jnp.minimum(t + half, seg_len) - jnp.maximum(t - half, 0)
            return 1.0 / cnt.astype(F32)

        zero = jnp.zeros((gap, cw), F32)
        buf[0:gap, :] = zero
        buf[gap + n_ctx:lat0, :] = zero
        buf[lat0 + n_lat:nbuf, :] = zero

        def fill(src0, dst0, seg_len, t0):
            v = x_ref[pl.ds(src0, r), :].astype(F32)
            if transpose:
                v = v * inv_cnt(seg_len, t0)
            buf[pl.ds(dst0, r), :] = v

        def compute(src0, dst0, seg_len, t0):
            win = buf[pl.ds(dst0 - gap, r + 2 * gap), :]
            nw = r + 2 * gap
            s = win + pltpu.roll(win, (nw - 1) if transpose else 1, 0)
            for sh in (1, 2, 4):
                if w >= 4 * sh:
                    s = pltpu.roll(s, sh, 0) + pltpu.roll(s, nw - sh, 0)
            ws = s[gap:gap + r]
            if transpose:
                out = ws - x_ref[pl.ds(src0, r), :].astype(F32)
            else:
                out = ws * inv_cnt(seg_len, t0) - win[gap:gap + r]
            o_ref[pl.ds(src0, r), :] = out.astype(o_ref.dtype)

        for step in (fill, compute):
            step(0, gap, n_ctx, 0)

            def lat(i, c, step=step):
                off = pl.multiple_of(i * r, r)
                step(n_ctx + off, lat0 + off, n_lat, off)
                return c
            lax.fori_loop(0, n_lat // r, lat, 0)

    return _pcall(body, name=name, grid=(ncol,),
                  in_specs=[pl.BlockSpec((n, cw), lambda j: (0, col_blk0 + j))],
                  out_specs=pl.BlockSpec((n, cw), lambda j: (0, j)),
                  out_shape=jax.ShapeDtypeStruct((n, ncol * cw), out_dtype),
                  scratch_shapes=[pltpu.VMEM((nbuf, cw), F32)],
                  compiler_params=_params(("parallel",)))(x)


def _make_pool(n_ctx, n_lat, pool_width, mix_width):
    ncol = pool_width // len(POOL_WINDOWS) // LANE

    @jax.custom_vjp
    def pool(u):
        return tuple(_pool_call(u, g * ncol, ncol, w, n_ctx, n_lat, False, BF16, "pool_w%d" % w)
                     for g, w in enumerate(POOL_WINDOWS))

    def fwd(u):
        return pool(u), None

    def bwd(_, cots):
        parts = [_pool_call(ct, 0, ncol, w, n_ctx, n_lat, True, F32, "pool_w%d_bwd" % w)
                 for ct, w in zip(cots, POOL_WINDOWS)]
        parts.append(jnp.zeros((n_ctx + n_lat, mix_width - pool_width), F32))
        return (jnp.concatenate(parts, axis=1),)

    pool.defvjp(fwd, bwd)
    return pool


class _ConvGeom:
    def __init__(self, n_ctx, n_lat):
        self.n_ctx, self.n_lat = n_ctx, n_lat
        self.gap = GRID_W + SUBLANE
        self.r = 2 * GRID_W
        self.ctx0 = self.gap
        self.lat0 = 2 * self.gap + n_ctx
        self.nbuf = 3 * self.gap + n_ctx + n_lat
        self.nwin = self.r + 2 * self.gap
        self.n16 = self.r + 2 * SUBLANE
        assert n_lat % self.r == 0 and n_ctx % self.r == 0

    def zero_gaps(self, buf):
        z = jnp.zeros((self.gap, LANE), F32)
        buf[0:self.gap, :] = z
        buf[self.ctx0 + self.n_ctx:self.lat0, :] = z
        buf[self.lat0 + self.n_lat:self.nbuf, :] = z

    def fill(self, src_ref, buf):
        r = self.r

        def seg(src0, dst0, count):
            def one(i, c):
                off = pl.multiple_of(i * r, r)
                buf[pl.ds(dst0 + off, r), :] = src_ref[pl.ds(src0 + off, r), :].astype(F32)
                return c
            lax.fori_loop(0, count, one, 0)
        seg(0, self.ctx0, self.n_ctx // r)
        seg(self.n_ctx, self.lat0, self.n_lat // r)

    def col_masks(self, nrows, first_col):
        col = (lax.broadcasted_iota(jnp.int32, (nrows, 1), 0) + first_col) & (GRID_W - 1)
        return col == GRID_W - 1, col == 0

    def lat_window(self, buf, i):
        ws = pl.multiple_of(self.lat0 - self.gap + i * self.r, SUBLANE)
        return buf[pl.ds(ws, self.nwin), :]

    def ctx_window(self, buf):
        return buf[self.ctx0 - SUBLANE:self.ctx0 + self.n_ctx + SUBLANE, :]

    def lat_sources(self, win):
        last, first = self.col_masks(self.nwin, GRID_W - SUBLANE)
        return jnp.where(last, 0.0, win), win, jnp.where(first, 0.0, win)

    def row_slice(self, x, di, sign):
        st = self.gap - SUBLANE + sign * (di - 1) * GRID_W
        return x[st:st + self.n16]

    def lat_conv(self, win, w):
        srcs = self.lat_sources(win)
        cs = []
        for dj in range(3):
            acc = None
            for di in range(3):
                term = w[di * 3 + dj] * self.row_slice(srcs[dj], di, 1)
                acc = term if acc is None else acc + term
            cs.append(acc)
        out = pltpu.roll(cs[0], 1, 0) + cs[1] + pltpu.roll(cs[2], self.n16 - 1, 0)
        return out[SUBLANE:SUBLANE + self.r]

    def ctx_conv(self, win, w, transpose=False):
        n = win.shape[0]
        lo, hi = (w[5], w[3]) if transpose else (w[3], w[5])
        out = lo * pltpu.roll(win, 1, 0) + w[4] * win + hi * pltpu.roll(win, n - 1, 0)
        return out[SUBLANE:SUBLANE + self.n_ctx]

    def lat_conv_t(self, dwin, w):
        es = []
        for dj in range(3):
            acc = None
            for di in range(3):
                term = w[di * 3 + dj] * self.row_slice(dwin, di, -1)
                acc = term if acc is None else acc + term
            es.append(acc)
        last, first = self.col_masks(self.n16, GRID_W - SUBLANE)
        out = (jnp.where(last, 0.0, pltpu.roll(es[0], self.n16 - 1, 0)) + es[1]
               + jnp.where(first, 0.0, pltpu.roll(es[2], 1, 0)))
        return out[SUBLANE:SUBLANE + self.r]


def _taps(w_ref):
    return [w_ref[k:k + 1, :] for k in range(9)]


def _conv_specs(n, f_tiles):
    zv = pl.BlockSpec((n, LANE), lambda j: (0, j))
    zg = pl.BlockSpec((n, LANE), lambda j: (0, j + f_tiles))
    wv = pl.BlockSpec((9, LANE), lambda j: (0, j))
    wg = pl.BlockSpec((9, LANE), lambda j: (0, j + f_tiles))
    return zv, zg, wv, wg


def _conv_fwd_call(z, wc, n_ctx, n_lat):
    n, f2 = z.shape
    ft = f2 // 2 // LANE
    geo = _ConvGeom(n_ctx, n_lat)

    def body(zv_ref, zg_ref, wv_ref, wg_ref, o_ref, cv_ref, cg_ref, bv, bg):
        wv, wg = _taps(wv_ref), _taps(wg_ref)
        for src, buf in ((zv_ref, bv), (zg_ref, bg)):
            geo.zero_gaps(buf)
            geo.fill(src, buf)

        def emit(rows, cv, cg):
            o_ref[rows, :] = (cv * cg * _sigmoid(cg)).astype(o_ref.dtype)
            cv_ref[rows, :] = cv.astype(cv_ref.dtype)
            cg_ref[rows, :] = cg.astype(cg_ref.dtype)

        emit(slice(0, n_ctx), geo.ctx_conv(geo.ctx_window(bv), wv), geo.ctx_conv(geo.ctx_window(bg), wg))

        def chunk(i, c):
            off = pl.multiple_of(n_ctx + i * geo.r, SUBLANE)
            emit(pl.ds(off, geo.r), geo.lat_conv(geo.lat_window(bv, i), wv), geo.lat_conv(geo.lat_window(bg, i), wg))
            return c
        lax.fori_loop(0, n_lat // geo.r, chunk, 0)

    tile = pl.BlockSpec((n, LANE), lambda j: (0, j))
    return _pcall(body, name="conv_gate", grid=(ft,), in_specs=list(_conv_specs(n, ft)), out_specs=[tile] * 3,
                  out_shape=[jax.ShapeDtypeStruct((n, f2 // 2), BF16)] * 3,
                  scratch_shapes=[pltpu.VMEM((geo.nbuf, LANE), F32)] * 2,
                  compiler_params=_params(("parallel",)))(z, z, wc, wc)


def _conv_bwd_call(z, wc, cv, cg, da, n_ctx, n_lat):
    n, f2 = z.shape
    ft = f2 // 2 // LANE
    geo = _ConvGeom(n_ctx, n_lat)
    r, n16 = geo.r, geo.n16

    def body(zv_ref, zg_ref, wv_ref, wg_ref, cv_ref, cg_ref, da_ref, dzv_ref, dzg_ref, dwv_ref, dwg_ref,
             bv, bg, dv, dg):
        wv, wg = _taps(wv_ref), _taps(wg_ref)
        for buf in (bv, bg, dv, dg):
            geo.zero_gaps(buf)
        geo.fill(zv_ref, bv)
        geo.fill(zg_ref, bg)

        def gate_grads(cv, cg, d):
            sg = _sigmoid(cg)
            return d * cg * sg, d * cv * sg * (1.0 + cg * (1.0 - sg))

        def tap_sums(d_c, srcs, pad):
            zeros = jnp.zeros((SUBLANE, LANE), F32)
            dce = jnp.concatenate([zeros, d_c, zeros], axis=0)
            m = dce.shape[0]
            shifted = (pltpu.roll(dce, m - 1, 0), dce, pltpu.roll(dce, 1, 0))
            out = []
            for di in range(3):
                for dj in range(3):
                    src = srcs[dj] if pad is None else geo.row_slice(srcs[dj], di, 1)
                    out.append(jnp.sum(shifted[dj] * src, axis=0, keepdims=True))
            return out

        winv, wing = geo.ctx_window(bv), geo.ctx_window(bg)
        d_cv, d_cg = gate_grads(cv_ref[0:n_ctx, :].astype(F32), cg_ref[0:n_ctx, :].astype(F32),
                                da_ref[0:n_ctx, :].astype(F32))
        dv[geo.ctx0:geo.ctx0 + n_ctx, :] = d_cv
        dg[geo.ctx0:geo.ctx0 + n_ctx, :] = d_cg
        zero_row = jnp.zeros((1, LANE), F32)
        acc0 = []
        for d_c, win in ((d_cv, winv), (d_cg, wing)):
            sums = tap_sums(d_c, (win, win, win), None)
            acc0 += [zero_row] * 3 + sums[3:6] + [zero_row] * 3

        def chunk(i, acc):
            winv, wing = geo.lat_window(bv, i), geo.lat_window(bg, i)
            off = pl.multiple_of(n_ctx + i * r, SUBLANE)
            d_cv, d_cg = gate_grads(cv_ref[pl.ds(off, r), :].astype(F32), cg_ref[pl.ds(off, r), :].astype(F32),
                                    da_ref[pl.ds(off, r), :].astype(F32))
            dst = pl.multiple_of(geo.lat0 + i * r, SUBLANE)
            dv[pl.ds(dst, r), :] = d_cv
            dg[pl.ds(dst, r), :] = d_cg
            sums = tap_sums(d_cv, geo.lat_sources(winv), True) + tap_sums(d_cg, geo.lat_sources(wing), True)
            return tuple(a + s for a, s in zip(acc, sums))
        acc = lax.fori_loop(0, n_lat // r, chunk, tuple(acc0))
        for k in range(9):
            dwv_ref[k:k + 1, :] = acc[k]
            dwg_ref[k:k + 1, :] = acc[9 + k]

        for dbuf, w, dz_ref in ((dv, wv, dzv_ref), (dg, wg, dzg_ref)):
            dz_ref[0:n_ctx, :] = geo.ctx_conv(geo.ctx_window(dbuf), w, transpose=True).astype(dz_ref.dtype)

            def chunk_t(i, c, dbuf=dbuf, w=w, dz_ref=dz_ref):
                off = pl.multiple_of(n_ctx + i * r, SUBLANE)
                dz_ref[pl.ds(off, r), :] = geo.lat_conv_t(geo.lat_window(dbuf, i), w).astype(dz_ref.dtype)
                return c
            lax.fori_loop(0, n_lat // r, chunk_t, 0)

    tile = pl.BlockSpec((n, LANE), lambda j: (0, j))
    wtile = pl.BlockSpec((9, LANE), lambda j: (0, j))
    dzv, dzg, dwv, dwg = _pcall(
        body, name="conv_gate_bwd", grid=(ft,), in_specs=list(_conv_specs(n, ft)) + [tile] * 3,
        out_specs=[tile, tile, wtile, wtile],
        out_shape=[jax.ShapeDtypeStruct((n, f2 // 2), z.dtype)] * 2 + [jax.ShapeDtypeStruct((9, f2 // 2), F32)] * 2,
        scratch_shapes=[pltpu.VMEM((geo.nbuf, LANE), F32)] * 4,
        compiler_params=_params(("parallel",)))(z, z, wc, wc, cv, cg, da)
    return jnp.concatenate([dzv, dzg], axis=1), jnp.concatenate([dwv, dwg], axis=1)


def _make_convgate(n_ctx, n_lat):
    @jax.custom_vjp
    def conv(z, wc):
        return _conv_fwd_call(z, wc, n_ctx, n_lat)[0]

    def fwd(z, wc):
        a, cv, cg = _conv_fwd_call(z, wc, n_ctx, n_lat)
        return a, (z, wc, cv, cg)

    def bwd(res, da):
        return _conv_bwd_call(*res, da, n_ctx, n_lat)

    conv.defvjp(fwd, bwd)
    return conv


GROUPS_PER_BLOCK = LANE // SSM_GROUP
STATE_BLOCK = GROUPS_PER_BLOCK * SSM_STATE
SCAN_LANES = STATE_BLOCK


def _cmul(ar, ai, br, bi):
    return ar * br - ai * bi, ar * bi + ai * br


def _lam_tables(lr, li, asc):
    row = lax.broadcasted_iota(jnp.int32, (SUBLANE, lr.shape[1]), 0)
    l1 = (jnp.broadcast_to(lr, row.shape), jnp.broadcast_to(li, row.shape))
    l2 = _cmul(*l1, *l1)
    l4 = _cmul(*l2, *l2)
    pw = l1
    pr = jnp.zeros(row.shape, F32)
    pi = jnp.zeros(row.shape, F32)
    for e in range(1, SUBLANE + 1):
        s = e - 1 if asc else SUBLANE - e
        pr = jnp.where(row == s, pw[0], pr)
        pi = jnp.where(row == s, pw[1], pi)
        pw = _cmul(*pw, *l1)

    def masked(lam_k, k):
        keep = (row >= k) if asc else (row < SUBLANE - k)
        return jnp.where(keep, lam_k[0], 0.0), jnp.where(keep, lam_k[1], 0.0)

    return masked(l1, 1), masked(l2, 2), masked(l4, 4), (pr, pi)


def _tile_scan(br, bi, cr, ci, tables, asc):
    hr, hi = br, bi
    for k, lam_k in zip((1, 2, 4), tables[:3]):
        shift = k if asc else SUBLANE - k
        mr, mi = _cmul(*lam_k, pltpu.roll(hr, shift, 0), pltpu.roll(hi, shift, 0))
        hr, hi = hr + mr, hi + mi
    mr, mi = _cmul(*tables[3], jnp.broadcast_to(cr, br.shape), jnp.broadcast_to(ci, br.shape))
    hr, hi = hr + mr, hi + mi
    last = SUBLANE - 1 if asc else 0
    return hr, hi, hr[last:last + 1, :], hi[last:last + 1, :]


def _chunk_in_time_order(k, n_chunks, asc, adjoint):
    if asc:
        return n_chunks - 1 - k if adjoint else k
    if adjoint:
        return jnp.where(k == n_chunks - 1, 0, k + 1)
    return jnp.where(k == 0, 0, n_chunks - k)


def _dot(a, b):
    return jnp.dot(a, b, preferred_element_type=F32)


def _dot_nt(a, b):
    return lax.dot_general(a, b, (((1,), (1,)), ((), ())), preferred_element_type=F32)


def _scan_chunk(r_buf, i_buf, base, rows, carry, tables, asc, lam_grad=None):
    tiles = rows // SUBLANE
    out_carry, grads = [], []
    for h in range(STATE_BLOCK // SCAN_LANES):
        cols = slice(h * SCAN_LANES, (h + 1) * SCAN_LANES)

        def tile(kt, c, h=h, cols=cols):
            pt = kt if asc else tiles - 1 - kt
            t0 = pl.multiple_of(base + pt * SUBLANE, SUBLANE)
            sr, si, ncr, nci = _tile_scan(r_buf[pl.ds(t0, SUBLANE), cols], i_buf[pl.ds(t0, SUBLANE), cols],
                                          c[0], c[1], tables[h], asc)
            r_buf[pl.ds(t0, SUBLANE), cols] = sr
            i_buf[pl.ds(t0, SUBLANE), cols] = si
            if lam_grad is None:
                return ncr, nci
            h_r, h_i, h_base = lam_grad
            g0 = pl.multiple_of(h_base + pt * SUBLANE, SUBLANE)
            pr, pi = h_r[pl.ds(g0, SUBLANE), cols], h_i[pl.ds(g0, SUBLANE), cols]
            row = lax.broadcasted_iota(jnp.int32, sr.shape, 0)
            if asc:
                nr = jnp.where(row == 0, jnp.broadcast_to(c[0], sr.shape), pltpu.roll(sr, 1, 0))
                ni = jnp.where(row == 0, jnp.broadcast_to(c[1], sr.shape), pltpu.roll(si, 1, 0))
            else:
                nr = jnp.where(row == SUBLANE - 1, jnp.broadcast_to(c[0], sr.shape), pltpu.roll(sr, SUBLANE - 1, 0))
                ni = jnp.where(row == SUBLANE - 1, jnp.broadcast_to(c[1], sr.shape), pltpu.roll(si, SUBLANE - 1, 0))
            return ncr, nci, c[2] + nr * pr + ni * pi, c[3] + ni * pr - nr * pi

        init = (carry[2 * h], carry[2 * h + 1])
        if lam_grad is not None:
            zero = jnp.zeros((SUBLANE, SCAN_LANES), F32)
            init = init + (zero, zero)
        res = lax.fori_loop(0, tiles, tile, init, unroll=2)
        out_carry += [res[0], res[1]]
        grads.append(res[2:])
    return tuple(out_carry), grads


def _ssm_specs(n):
    tok = pl.BlockSpec((n, LANE), lambda q: (0, q))
    lam = pl.BlockSpec((1, STATE_BLOCK), lambda q: (0, q))
    w_in = pl.BlockSpec((None, LANE, STATE_BLOCK), lambda q: (q, 0, 0))
    w_out = pl.BlockSpec((None, STATE_BLOCK, LANE), lambda q: (q, 0, 0))
    return tok, lam, w_in, w_out


def _zero_carry():
    return tuple(jnp.zeros((1, SCAN_LANES), F32) for _ in range(2 * (STATE_BLOCK // SCAN_LANES)))


def _half_tables(lr_ref, li_ref, asc, conj):
    out = []
    for h in range(STATE_BLOCK // SCAN_LANES):
        cols = slice(h * SCAN_LANES, (h + 1) * SCAN_LANES)
        li = li_ref[:, cols]
        out.append(_lam_tables(lr_ref[:, cols], -li if conj else li, asc))
    return out


def _ssm_fwd_call(u, lr, li, b_re, b_im, c_re, c_imn, rc, asc, side=None):
    n, cs = u.shape
    nq, nchunks = cs // LANE, n // rc
    w_states = lr.shape[1]
    tok, lam, w_in, w_out = _ssm_specs(n)
    states = pl.BlockSpec((n, STATE_BLOCK), lambda q: (0, q))

    def body(u_ref, lr_ref, li_ref, br_ref, bi_ref, cr_ref, ci_ref, y_ref, sr_ref, si_ref, h_r, h_i):
        tables = _half_tables(lr_ref, li_ref, asc, False)
        wbr, wbi = br_ref[...].astype(BF16), bi_ref[...].astype(BF16)
        wcr, wci = cr_ref[...].astype(BF16), ci_ref[...].astype(BF16)

        def chunk(k, carry):
            r0 = pl.multiple_of(_chunk_in_time_order(k, nchunks, asc, False) * rc, rc)
            ub = u_ref[pl.ds(r0, rc), :].astype(BF16)
            h_r[...] = _dot(ub, wbr)
            h_i[...] = _dot(ub, wbi)
            carry, _ = _scan_chunk(h_r, h_i, 0, rc, carry, tables, asc)
            hb_r, hb_i = h_r[...].astype(BF16), h_i[...].astype(BF16)
            sr_ref[pl.ds(r0, rc), :] = hb_r
            si_ref[pl.ds(r0, rc), :] = hb_i
            y_ref[pl.ds(r0, rc), :] = _dot(hb_r, wcr) + _dot(hb_i, wci)
            return carry
        lax.fori_loop(0, nchunks, chunk, _zero_carry())

    return _call_with_side(body, name="s5_head", grid=(nq,), in_specs=[tok, lam, lam, w_in, w_in, w_out, w_out],
                           out_specs=[tok, states, states],
                           out_shape=[jax.ShapeDtypeStruct((n, cs), F32)] + [jax.ShapeDtypeStruct((n, w_states), BF16)] * 2,
                           scratch_shapes=[pltpu.VMEM((rc, STATE_BLOCK), F32)] * 2,
                           args=(u, lr, li, b_re, b_im, c_re, c_imn), side=side, semantics=("parallel",))


def _ssm_bwd_call(u, dy, s_re, s_im, lr, li, b_re, b_im, c_re, c_imn, rc, asc, side=None):
    n, cs = u.shape
    nq, nchunks = cs // LANE, n // rc
    tok, lam, w_in, w_out = _ssm_specs(n)
    states = pl.BlockSpec((n, STATE_BLOCK), lambda q: (0, q))

    def body(u_ref, dy_ref, sr_ref, si_ref, lr_ref, li_ref, br_ref, bi_ref, cr_ref, ci_ref,
             du_ref, glr_ref, gli_ref, dbr_ref, dbi_ref, dcr_ref, dci_ref, h_r, h_i, a_r, a_i):
        wbr, wbi = br_ref[...].astype(BF16), bi_ref[...].astype(BF16)
        wcr, wci = cr_ref[...].astype(BF16), ci_ref[...].astype(BF16)

        adj = _half_tables(lr_ref, li_ref, not asc, True)
        for ref in (glr_ref, gli_ref, dbr_ref, dbi_ref, dcr_ref, dci_ref):
            ref[...] = jnp.zeros_like(ref)

        def chunk_adj(k, carry):
            r0 = pl.multiple_of(_chunk_in_time_order(k, nchunks, asc, True) * rc, rc)
            dyc = dy_ref[pl.ds(r0, rc), :]
            dyb = dyc.astype(BF16)
            a_r[...] = _dot_nt(dyb, wcr)
            a_i[...] = _dot_nt(dyb, wci)
            sb_r, sb_i = sr_ref[pl.ds(r0, rc), :], si_ref[pl.ds(r0, rc), :]
            h_r[...] = sb_r.astype(F32)
            h_i[...] = sb_i.astype(F32)
            carry, grads = _scan_chunk(a_r, a_i, 0, rc, carry, adj, not asc, lam_grad=(h_r, h_i, 0))
            for h, (gr, gi) in enumerate(grads):
                cols = slice(h * SCAN_LANES, (h + 1) * SCAN_LANES)
                glr_ref[:, cols] += gr
                gli_ref[:, cols] += gi
            ab_r, ab_i = a_r[...].astype(BF16), a_i[...].astype(BF16)
            du_ref[pl.ds(r0, rc), :] = _dot_nt(ab_r, wbr) + _dot_nt(ab_i, wbi)
            ut = u_ref[pl.ds(r0, rc), :].T.astype(BF16)
            dbr_ref[...] += _dot(ut, ab_r)
            dbi_ref[...] += _dot(ut, ab_i)
            dyt = dyc.T.astype(BF16)
            dcr_ref[...] += _dot(dyt, sb_r)
            dci_ref[...] += _dot(dyt, sb_i)
            return carry
        lax.fori_loop(0, nchunks, chunk_adj, _zero_carry())

    part = pl.BlockSpec((SUBLANE, STATE_BLOCK), lambda q: (0, q))
    w_states = lr.shape[1]
    return _call_with_side(
        body, name="s5_head_bwd", grid=(nq,),
        in_specs=[tok, tok, states, states, lam, lam, w_in, w_in, w_out, w_out],
        out_specs=[tok, part, part, w_in, w_in, w_in, w_in],
        out_shape=[jax.ShapeDtypeStruct((n, cs), F32)] + [jax.ShapeDtypeStruct((SUBLANE, w_states), F32)] * 2
        + [jax.ShapeDtypeStruct((nq, LANE, STATE_BLOCK), F32)] * 4,
        scratch_shapes=[pltpu.VMEM((rc, STATE_BLOCK), F32)] * 4,
        args=(u, dy, s_re, s_im, lr, li, b_re, b_im, c_re, c_imn), side=side, semantics=("parallel",))


def _ssm_cotangents(outs):
    du, glr, gli, dbr, dbi, dcr, dci = outs
    return (du, jnp.sum(glr, axis=0, keepdims=True), jnp.sum(gli, axis=0, keepdims=True), dbr, dbi,
            jnp.swapaxes(dcr, 1, 2), jnp.swapaxes(dci, 1, 2))


def _make_ssm(rc, asc):
    @jax.custom_vjp
    def ssm(u, lr, li, b_re, b_im, c_re, c_imn):
        return _ssm_fwd_call(u, lr, li, b_re, b_im, c_re, c_imn, rc, asc)[0]

    def fwd(*args):
        y, s_re, s_im = _ssm_fwd_call(*args, rc, asc)
        return y, (args, s_re, s_im)

    def bwd(res, dy):
        args, s_re, s_im = res
        return _ssm_cotangents(_ssm_bwd_call(args[0], dy, s_re, s_im, *args[1:], rc, asc))

    ssm.defvjp(fwd, bwd)
    return ssm


def _make_hosting_ssm(rc, asc, name):
    def run(u, lr, li, b_re, b_im, c_re, c_imn, shard):
        sh = shard.astype(BF16)
        side = (_gather_parts, sh, jax.ShapeDtypeStruct((N_DEV,) + sh.shape, BF16))
        return _ssm_fwd_call(u, lr, li, b_re, b_im, c_re, c_imn, rc, asc, side=side)

    @jax.custom_vjp
    def ssm(*args):
        y, _, _, whole = run(*args)
        return y, whole

    def fwd(*args):
        y, s_re, s_im, whole = run(*args)
        return (y, whole), (args[:-1], s_re, s_im)

    def bwd(res, cts):
        args, s_re, s_im = res
        dy, d_whole = cts
        side = (_scatter_parts, d_whole, jax.ShapeDtypeStruct(d_whole.shape, d_whole.dtype))
        outs = _ssm_bwd_call(args[0], dy, s_re, s_im, *args[1:], rc, asc, side=side)
        return _ssm_cotangents(outs[:-1]) + (_sum8_call(outs[-1], name + "_sum"),)

    ssm.defvjp(fwd, bwd)
    return ssm


def _blocks_in(bb):
    g, p, h = bb.shape
    k = GROUPS_PER_BLOCK
    out = jnp.einsum('qgph,gk->qghkp', bb.reshape(g // k, k, p, h), jnp.eye(k, dtype=F32))
    return out.reshape(g // k, k * h, k * p)


def _blocks_out(cc):
    g, h, p = cc.shape
    k = GROUPS_PER_BLOCK
    out = jnp.einsum('qghp,gk->qgpkh', cc.reshape(g // k, k, h, p), jnp.eye(k, dtype=F32))
    return out.reshape(g // k, k * p, k * h)


def _position():
    return lax.axis_index("x"), lax.axis_index("y"), lax.axis_index("c")


def _linear_index():
    x, y, c = _position()
    return 4 * x + 2 * y + c


COMM_SCRATCH = [pltpu.SemaphoreType.DMA((7,)), pltpu.SemaphoreType.DMA((7,)), pltpu.SemaphoreType.DMA(())]
ANY_SPEC = pl.BlockSpec(memory_space=pl.ANY)


def _gather_parts(x_ref, o_ref, send_sems, recv_sems, local_sem):
    x, y, c = _position()
    me, sibling = (x, y, c), (x, y, 1 - c)
    chips = [(1 - x, y), (x, 1 - y), (1 - x, 1 - y)]

    def block(px, py, pc):
        return o_ref.at[4 * px + 2 * py + pc]

    def copy(k, blk, to, src=None):
        return pltpu.make_async_remote_copy(
            src_ref=block(*blk) if src is None else src, dst_ref=block(*blk),
            send_sem=send_sems.at[k], recv_sem=recv_sems.at[k], device_id=to, device_id_type=MESH)

    mine = pltpu.make_async_copy(x_ref, block(*me), local_sem)
    first = [copy(0, me, sibling, src=x_ref)]
    first += [copy(1 + j, me, (*chip, c), src=x_ref) for j, chip in enumerate(chips)]
    passed = [copy(4 + j, (*chip, c), sibling) for j, chip in enumerate(chips)]

    def start():
        mine.start()
        for cp in first:
            cp.start()

    def finish():
        for j, chip in enumerate(chips):
            copy(1 + j, (*chip, c), me).wait_recv()
            passed[j].start()
        copy(0, sibling, me).wait_recv()
        for j, chip in enumerate(chips):
            copy(4 + j, (*chip, 1 - c), me).wait_recv()
        for cp in first + passed:
            cp.wait_send()
        mine.wait()

    return start, finish


def _scatter_parts(g_ref, o_ref, send_sems, recv_sems, local_sem, rows=None):
    x, y, c = _position()

    def block(p):
        return g_ref.at[p] if rows is None else g_ref.at[p, pl.ds(rows[0], rows[1])]

    mine = pltpu.make_async_copy(block(4 * x + 2 * y + c), o_ref.at[0], local_sem)
    copies = []
    for k in range(1, N_DEV):
        px = 1 - x if k & 4 else x
        py = 1 - y if k & 2 else y
        pc = 1 - c if k & 1 else c
        copies.append(pltpu.make_async_remote_copy(
            src_ref=block(4 * px + 2 * py + pc), dst_ref=o_ref.at[k],
            send_sem=send_sems.at[k - 1], recv_sem=recv_sems.at[k - 1],
            device_id=(px, py, pc), device_id_type=MESH))

    def start():
        mine.start()
        for cp in copies:
            cp.start()

    def finish():
        for cp in copies:
            cp.wait()
        mine.wait()

    return start, finish


def _comm_call(parts, src, out_shape, name):
    def body(s_ref, o_ref, *sems):
        start, finish = parts(s_ref, o_ref, *sems)
        start()
        finish()

    return _pcall(body, name=name, in_specs=[ANY_SPEC], out_specs=ANY_SPEC, out_shape=out_shape,
                  scratch_shapes=COMM_SCRATCH)(src)


def _ag_call(shard, name):
    return _comm_call(_gather_parts, shard, jax.ShapeDtypeStruct((N_DEV,) + shard.shape, shard.dtype), name)


def _rs_call(g, name):
    return _comm_call(_scatter_parts, g, jax.ShapeDtypeStruct(g.shape, g.dtype), name)


def _sum8_call(parts, name):
    _, r, c = parts.shape
    tr = _tile(r, max(PACK, (4 << 20) // (N_DEV * c * parts.dtype.itemsize) // PACK * PACK), PACK)

    def body(p_ref, o_ref):
        acc = p_ref[0].astype(F32)
        for k in range(1, N_DEV):
            acc = acc + p_ref[k].astype(F32)
        o_ref[...] = acc

    return _pcall(body, name=name, grid=(r // tr,),
                  in_specs=[pl.BlockSpec((N_DEV, tr, c), lambda i: (0, i, 0))],
                  out_specs=pl.BlockSpec((tr, c), lambda i: (i, 0)),
                  out_shape=jax.ShapeDtypeStruct((r, c), F32), compiler_params=_params(("parallel",)))(parts)


def _reduce_scatter(g, name):
    return _sum8_call(_rs_call(g, name), name + "_sum")


def _make_gather(dtype, name):
    @jax.custom_vjp
    def gather(shard):
        return _ag_call(shard.astype(dtype), name)

    def fwd(shard):
        return gather(shard), None

    def bwd(_, ct):
        return (_reduce_scatter(ct, name + "_rs"),)

    gather.defvjp(fwd, bwd)
    return gather


def _adam_call(w, g, m, v, name):
    r, c = w.shape
    tr = _tile(r, max(SUBLANE, (1 << 20) // (4 * c) // SUBLANE * SUBLANE), SUBLANE)

    def body(w_ref, g_ref, m_ref, v_ref, d_ref, mo_ref, vo_ref):
        gv = g_ref[...]
        m2 = ADAM_B1 * m_ref[...] + (1.0 - ADAM_B1) * gv
        v2 = ADAM_B2 * v_ref[...] + (1.0 - ADAM_B2) * (gv * gv)
        m_hat = m2 / (1.0 - ADAM_B1 ** ADAM_STEP)
        v_hat = v2 / (1.0 - ADAM_B2 ** ADAM_STEP)
        d_ref[...] = -ADAM_LR * (m_hat / (jnp.sqrt(v_hat) + ADAM_EPS) + ADAM_WD * w_ref[...])
        mo_ref[...] = m2
        vo_ref[...] = v2

    spec = pl.BlockSpec((tr, c), lambda i: (i, 0))
    return _pcall(body, name=name, grid=(r // tr,), in_specs=[spec] * 4, out_specs=[spec] * 3,
                  out_shape=[jax.ShapeDtypeStruct((r, c), F32)] * 3, compiler_params=_params(("parallel",)))(w, g, m, v)


def _loss_call(y, target, tr):
    n, d = y.shape

    def body(y_ref, t_ref, s_ref, dy_ref):
        i = pl.program_id(0)
        err = y_ref[...] - t_ref[...]
        dy_ref[...] = err * (1.0 / d)
        part = jnp.sum(jnp.sum(err * err, axis=1, keepdims=True), axis=0, keepdims=True)

        @pl.when(i == 0)
        def _():
            s_ref[...] = jnp.zeros_like(s_ref)
        s_ref[...] += part

    spec = pl.BlockSpec((tr, d), lambda i: (i, 0))
    return _pcall(body, name="loss_head", grid=(n // tr,), in_specs=[spec, spec],
                  out_specs=[pl.BlockSpec((1, 1), lambda i: (0, 0)), spec],
                  out_shape=[jax.ShapeDtypeStruct((1, 1), F32), jax.ShapeDtypeStruct((n, d), F32)],
                  compiler_params=_params(("arbitrary",)))(y, target)


def _pack(arrays, rows_mult):
    flat = jnp.concatenate([a.reshape(-1).astype(F32) for a in arrays])
    rows = -(-flat.shape[0] // LANE)
    rows = -(-rows // rows_mult) * rows_mult
    return jnp.pad(flat, (0, rows * LANE - flat.shape[0])).reshape(rows, LANE)


def _unpack(buf, shapes, lead=()):
    flat = buf.reshape(lead + (-1,))
    out, pos = [], 0
    for s in shapes:
        size = math.prod(s)
        out.append(flat[..., pos:pos + size].reshape(lead + tuple(s)))
        pos += size
    return out


def _s5_discretise(a_re, a_im, log_dt, b_re, b_im):
    dt = jnp.exp(log_dt)[:, None]
    mag = jnp.exp(a_re * dt)
    lam_re = mag * jnp.cos(a_im * dt)
    lam_im = mag * jnp.sin(a_im * dt)
    denom = a_re * a_re + a_im * a_im
    nr, ni = lam_re - 1.0, lam_im
    f_re = (nr * a_re + ni * a_im) / denom
    f_im = (ni * a_re - nr * a_im) / denom
    bb_re = f_re[..., None] * b_re - f_im[..., None] * b_im
    bb_im = f_re[..., None] * b_im + f_im[..., None] * b_re
    return lam_re, lam_im, bb_re, bb_im


def _forward(x, p, ctx, s_c, n_ctx, n_lat):
    d_model = x.shape[1]
    depth = len(p['w_in'])
    mix = p['w_in'][0].shape[1]
    pool_width = p['pool_scale'].shape[1]
    tr = n_ctx
    me = _linear_index()

    rmsmod = _make_rowop(_fn_rmsmod, 1, 2, 1, [BF16], tr, "rms_modulate")
    resid = _make_rowop(_fn_resid, 2, 1, 1, [F32], tr, "gated_residual")
    gelu_head = _make_rowop(_fn_gelu_head, 2, 0, 1, [F32], tr, "ssm_gelu")
    mixcat = _make_rowop(_fn_mixcat, 6, 0, 1, [BF16], tr, "mix_concat")
    pool = _make_pool(n_ctx, n_lat, pool_width, mix)
    convgate = _make_convgate(n_ctx, n_lat)
    ssm = [_make_ssm(tr, True), _make_ssm(tr, False)]
    lin_f32 = {k: _make_linear(F32, k) for k in ("w_in", "w_pool", "w_glu", "w_out", "w_down", "ada")}
    lin_up = _make_linear(BF16, "w_up")
    gather_big = {k: _make_gather(BF16, "gather_" + k) for k in BIG_SHARDED}
    gather_small = _make_gather(F32, "gather_small")
    gather_mod = _make_gather(F32, "gather_mod")

    whole = {k: gather_big[k](p[k][0]) for k in ('w_in', 'w_out')}
    host = {k: _make_hosting_linear(BF16 if k == 'w_up' else F32, k) for k in BIG_SHARDED}
    ssm_host = [_make_hosting_ssm(tr, True, "s5_w_up"), _make_hosting_ssm(tr, False, "s5_w_down")]

    def big_linear(k, a, l):
        w = whole[k] if k == 'w_up' else whole[k].reshape(-1, whole[k].shape[2])
        if l + 1 == depth:
            return (lin_up if k == 'w_up' else lin_f32[k])(a, w)
        y, whole[k] = host[k](a, w, p[k][l + 1])
        return y

    small = [p[k] for k in SMALL_SHARDED]
    packed = gather_small(_pack(small, PACK))
    w_pool, w_glu, w_conv = _unpack(packed, [a.shape for a in small], lead=(N_DEV,))
    w_pool = jnp.moveaxis(w_pool, 0, 2).reshape(depth, len(POOL_WINDOWS), -1, w_pool.shape[-1])
    w_glu = jnp.moveaxis(w_glu, 0, 1).reshape(depth, -1, w_glu.shape[-1])
    w_conv = jnp.moveaxis(w_conv, 0, 3).reshape(depth, 9, -1)

    s_rows = jnp.concatenate([s_c, jax.nn.silu(p['c_ctx'])[None, :],
                              jnp.zeros((PACK - N_DEV - 1, d_model), F32)], axis=0)
    cols = p['w_ada'][0].shape[1]
    b_loc = lax.dynamic_slice_in_dim(p['b_ada'], me * cols, cols, axis=1)
    mod_loc = jnp.stack([lin_f32['ada'](s_rows, p['w_ada'][l]) + b_loc[l][None, :] for l in range(depth)])
    mod = gather_mod(mod_loc.reshape(depth * PACK, cols)).reshape(N_DEV, depth, PACK, cols)
    mod = jnp.moveaxis(mod, 0, 2).reshape(depth, PACK, 6, d_model)
    mod_lat = lax.dynamic_index_in_dim(mod, me, axis=1, keepdims=False)
    mod_ctx = mod[:, N_DEV]

    xs = jnp.concatenate([ctx, x], axis=0)
    for l in range(depth):
        def seg(k):
            return jnp.stack([mod_ctx[l, k], mod_lat[l, k]]).reshape(2, 1, d_model)

        def row(name):
            return p[name][l].reshape(1, -1)

        h1, = rmsmod(xs, seg(0), seg(1), row('g_pre_mix'))
        u = big_linear('w_in', h1, l)
        pooled = pool(u)
        yp = [lin_f32['w_pool'](pooled[g], w_pool[l, g]) for g in range(len(POOL_WINDOWS))]
        u_ssm = u[:, pool_width:]
        ys = []
        for d in range(2):
            lam_re, lam_im, bb_re, bb_im = _s5_discretise(
                p['ssm_a_re'][l, d], p['ssm_a_im'][l, d], p['ssm_log_dt'][l, d], p['ssm_b_re'][l, d], p['ssm_b_im'][l, d])
            args = (u_ssm, lam_re.reshape(1, -1), lam_im.reshape(1, -1), _blocks_in(bb_re), _blocks_in(bb_im),
                    _blocks_out(p['ssm_c_re'][l, d]), _blocks_out(-p['ssm_c_im'][l, d]))
            if l == 0:
                k = ('w_up', 'w_down')[d]
                y_dir, whole[k] = ssm_host[d](*args, p[k][0])
                ys.append(y_dir)
            else:
                ys.append(ssm[d](*args))
        gl, = gelu_head(ys[0] + ys[1], u_ssm, row('ssm_d'))
        s = lin_f32['w_glu'](gl, w_glu[l])
        cat, = mixcat(*yp, gl, s, row('pool_scale'))
        mixed = big_linear('w_out', cat, l)
        x1, = resid(xs, mixed, seg(2), row('g_post_mix'))
        h2, = rmsmod(x1, seg(3), seg(4), row('g_pre_ffn'))
        z = big_linear('w_up', h2, l)
        a = convgate(z, w_conv[l])
        f = big_linear('w_down', a, l)
        xs, = resid(x1, f, seg(5), row('g_post_ffn'))
    return xs[n_ctx:]


def _as_rows(a):
    return a.reshape(-1, a.shape[-1])


def kernel(x, c, ctx, c_ctx, w_ada, b_ada, w_in, w_pool, pool_scale, ssm_a_re, ssm_a_im, ssm_log_dt, ssm_b_re, ssm_b_im, ssm_c_re, ssm_c_im, ssm_d, w_glu, w_out, g_pre_mix, g_post_mix, g_pre_ffn, g_post_ffn, w_up, w_conv, w_down, loss_target, m_c_ctx, m_w_ada, m_b_ada, m_w_in, m_w_pool, m_pool_scale, m_ssm_a_re, m_ssm_a_im, m_ssm_log_dt, m_ssm_b_re, m_ssm_b_im, m_ssm_c_re, m_ssm_c_im, m_ssm_d, m_w_glu, m_w_out, m_g_pre_mix, m_g_post_mix, m_g_pre_ffn, m_g_post_ffn, m_w_up, m_w_conv, m_w_down, v_c_ctx, v_w_ada, v_b_ada, v_w_in, v_w_pool, v_pool_scale, v_ssm_a_re, v_ssm_a_im, v_ssm_log_dt, v_ssm_b_re, v_ssm_b_im, v_ssm_c_re, v_ssm_c_im, v_ssm_d, v_w_glu, v_w_out, v_g_pre_mix, v_g_post_mix, v_g_pre_ffn, v_g_post_ffn, v_w_up, v_w_conv, v_w_down):
    weights = dict(zip(WEIGHTS, (c_ctx, w_ada, b_ada, w_in, w_pool, pool_scale, ssm_a_re, ssm_a_im, ssm_log_dt, ssm_b_re,
                                 ssm_b_im, ssm_c_re, ssm_c_im, ssm_d, w_glu, w_out, g_pre_mix, g_post_mix, g_pre_ffn,
                                 g_post_ffn, w_up, w_conv, w_down)))
    m_in = dict(zip(WEIGHTS, (m_c_ctx, m_w_ada, m_b_ada, m_w_in, m_w_pool, m_pool_scale, m_ssm_a_re, m_ssm_a_im,
                              m_ssm_log_dt, m_ssm_b_re, m_ssm_b_im, m_ssm_c_re, m_ssm_c_im, m_ssm_d, m_w_glu, m_w_out,
                              m_g_pre_mix, m_g_post_mix, m_g_pre_ffn, m_g_post_ffn, m_w_up, m_w_conv, m_w_down)))
    v_in = dict(zip(WEIGHTS, (v_c_ctx, v_w_ada, v_b_ada, v_w_in, v_w_pool, v_pool_scale, v_ssm_a_re, v_ssm_a_im,
                              v_ssm_log_dt, v_ssm_b_re, v_ssm_b_im, v_ssm_c_re, v_ssm_c_im, v_ssm_d, v_w_glu, v_w_out,
                              v_g_pre_mix, v_g_post_mix, v_g_pre_ffn, v_g_post_ffn, v_w_up, v_w_conv, v_w_down)))
    depth = w_in.shape[0]
    n_lat, d_model = x.shape[1], x.shape[2]
    n_ctx = ctx.shape[1]
    per_layer = BIG_SHARDED + ['w_ada']

    c_rows = jnp.concatenate([c, jnp.zeros((SUBLANE - 1, d_model), F32)], axis=0)
    s_c = jax.nn.silu(_ag_call(c_rows, "gather_c")[:, 0, :])

    params = {k: ([w[l] for l in range(depth)] if k in per_layer else w) for k, w in weights.items()}

    def run(x2d, prm):
        return _forward(x2d, prm, ctx[0], s_c, n_ctx, n_lat)

    y, vjp = jax.vjp(run, x[0], params)
    sq, dy = _loss_call(y, loss_target[0], n_ctx)
    loss = lax.psum(0.5 * sq[0, 0] / d_model, ("x", "y", "c"))
    gx, grads = vjp(dy)
    grads = {k: (jnp.stack(g) if k in per_layer else g) for k, g in grads.items()}

    rep_shapes = [weights[k].shape for k in REPLICATED]
    contrib = _pack([grads[k] for k in REPLICATED], N_DEV * PACK)
    rows = contrib.shape[0] // N_DEV
    total = _ag_call(_reduce_scatter(contrib.reshape(N_DEV, rows, LANE), "reduce_replicated"), "gather_replicated")
    total = total.reshape(N_DEV * rows, LANE)
    for k, g in zip(REPLICATED, _unpack(total, rep_shapes)):
        grads[k] = g

    delta, new_m, new_v = {}, {}, {}
    rep = [_pack([src[k] for k in REPLICATED], N_DEV * PACK) for src in (weights, m_in, v_in)]
    upd = _adam_call(rep[0], total, rep[1], rep[2], "adamw_replicated")
    for out, buf in zip((delta, new_m, new_v), upd):
        out.update(zip(REPLICATED, _unpack(buf, rep_shapes)))
    for k in WEIGHTS:
        if k in REPLICATED:
            continue
        upd = _adam_call(_as_rows(weights[k]), _as_rows(grads[k]), _as_rows(m_in[k]), _as_rows(v_in[k]), "adamw_" + k)
        for out, buf in zip((delta, new_m, new_v), upd):
            out[k] = buf.reshape(weights[k].shape)

    return (loss, gx[None], *[grads[k] for k in WEIGHTS], *[delta[k] for k in WEIGHTS],
            *[new_m[k] for k in WEIGHTS], *[new_v[k] for k in WEIGHTS])
```

```python
import functools
import math

import jax
import jax.numpy as jnp
from jax import lax
from jax.experimental import pallas as pl
from jax.experimental.pallas import tpu as pltpu

F32 = jnp.float32
BF16 = jnp.bfloat16

N_DEV = 8
GRID_W = 64
POOL_WINDOWS = (2, 4, 8, 16)
SSM_GROUP = 16
SSM_STATE = 64
EPS = 1e-6
ADAM_LR = 0.001
ADAM_B1 = 0.9
ADAM_B2 = 0.999
ADAM_EPS = 1e-08
ADAM_WD = 0.01
ADAM_STEP = 10

SUBLANE = 8
PACK = 16
LANE = 128
VMEM_LIMIT = 56 * 1024 * 1024
MM_VMEM_BUDGET = 40 * 1024 * 1024
MESH = pl.DeviceIdType.MESH

WEIGHTS = ['c_ctx', 'w_ada', 'b_ada', 'w_in', 'w_pool', 'pool_scale', 'ssm_a_re', 'ssm_a_im', 'ssm_log_dt',
           'ssm_b_re', 'ssm_b_im', 'ssm_c_re', 'ssm_c_im', 'ssm_d', 'w_glu', 'w_out', 'g_pre_mix',
           'g_post_mix', 'g_pre_ffn', 'g_post_ffn', 'w_up', 'w_conv', 'w_down']
REPLICATED = ['c_ctx', 'b_ada', 'pool_scale', 'ssm_a_re', 'ssm_a_im', 'ssm_log_dt', 'ssm_b_re', 'ssm_b_im',
              'ssm_c_re', 'ssm_c_im', 'ssm_d', 'g_pre_mix', 'g_post_mix', 'g_pre_ffn', 'g_post_ffn']
BIG_SHARDED = ['w_in', 'w_out', 'w_up', 'w_down']
SMALL_SHARDED = ['w_pool', 'w_glu', 'w_conv']


def _pcall(body, **kw):
    return pl.pallas_call(body, **kw)


def _params(sem=None):
    return pltpu.CompilerParams(dimension_semantics=sem, vmem_limit_bytes=VMEM_LIMIT)


def _tile(n, cap, mult):
    if n <= cap:
        return n
    best = None
    d = mult
    while d <= cap:
        if n % d == 0:
            best = d
        d += mult
    assert best is not None, (n, cap, mult)
    return best


def _sigmoid(x):
    return 1.0 / (1.0 + jnp.exp(-x))


def _mm(a, b, *, tb=False, out_dtype=F32, out_blocked=False, side=None, name):
    b_blocked = b.ndim == 3
    M, K = a.shape
    if b_blocked:
        nb, br, bc = b.shape
        N, Kb = (br, nb * bc) if tb else (nb * bc, br)
    else:
        N, Kb = (b.shape if tb else b.shape[::-1])
    assert K == Kb, (a.shape, b.shape, tb)
    tm = _tile(M, 1408, PACK)
    tk = bc if (b_blocked and tb) else _tile(K, 2304, LANE)

    def vmem_bytes(tn):
        tiles = tm * tk * a.dtype.itemsize + tk * tn * b.dtype.itemsize + tm * tn * jnp.dtype(out_dtype).itemsize
        return 2 * tiles + tm * tn * 4

    if out_blocked:
        assert N % N_DEV == 0
        tn = N // N_DEV
    elif b_blocked and not tb:
        tn = bc
    else:
        tn = _tile(N, 1024, LANE)
        if vmem_bytes(tn) > MM_VMEM_BUDGET:
            tn = _tile(N, 512, LANE)
    if b_blocked and not tb:
        assert tn == bc
    nm, nn, nk = M // tm, N // tn, K // tk

    a_spec = pl.BlockSpec((tm, tk), lambda i, j, k: (i, k))
    if b_blocked:
        if tb:
            b_spec = pl.BlockSpec((None, tn, tk), lambda i, j, k: (k, j, 0))
        else:
            b_spec = pl.BlockSpec((None, tk, tn), lambda i, j, k: (j, k, 0))
    else:
        b_spec = pl.BlockSpec((tn, tk), lambda i, j, k: (j, k)) if tb else pl.BlockSpec((tk, tn), lambda i, j, k: (k, j))
    if out_blocked:
        o_spec = pl.BlockSpec((None, tm, tn), lambda i, j, k: (j, i, 0))
        o_shape = jax.ShapeDtypeStruct((N_DEV, M, tn), out_dtype)
    else:
        o_spec = pl.BlockSpec((tm, tn), lambda i, j, k: (i, j))
        o_shape = jax.ShapeDtypeStruct((M, N), out_dtype)
    dims = (((1,), ((1 if tb else 0),)), ((), ()))

    def matmul(a_ref, b_ref, o_ref, acc_ref):
        k = pl.program_id(2)
        part = lax.dot_general(a_ref[...].astype(BF16), b_ref[...].astype(BF16), dims, preferred_element_type=F32)
        if nk == 1:
            o_ref[...] = part.astype(o_ref.dtype)
        else:
            @pl.when(k == 0)
            def _():
                acc_ref[...] = part

            @pl.when(k > 0)
            def _():
                acc_ref[...] += part

            @pl.when(k == nk - 1)
            def _():
                o_ref[...] = acc_ref[...].astype(o_ref.dtype)

    out = _call_with_side(matmul, name=name, grid=(nm, nn, nk), in_specs=[a_spec, b_spec], out_specs=[o_spec],
                          out_shape=[o_shape], scratch_shapes=[pltpu.VMEM((tm, tn), F32)], args=(a, b), side=side,
                          semantics=("parallel", "parallel", "arbitrary"))
    return out[0] if side is None else out


def _call_with_side(core, *, name, grid, in_specs, out_specs, out_shape, scratch_shapes, args, side, semantics):
    if side is None:
        return _pcall(core, name=name, grid=grid, in_specs=in_specs, out_specs=out_specs, out_shape=out_shape,
                      scratch_shapes=scratch_shapes, compiler_params=_params(semantics))(*args)
    parts, src, side_shape = side
    n_in, n_out, n_scr = len(in_specs), len(out_specs), len(scratch_shapes)

    def body(*refs):
        ins, s_ref = refs[:n_in], refs[n_in]
        outs, so_ref = refs[n_in + 1:n_in + 1 + n_out], refs[n_in + 1 + n_out]
        scr, sems = refs[n_in + 2 + n_out:n_in + 2 + n_out + n_scr], refs[n_in + 2 + n_out + n_scr:]
        step = 0
        for d, g in enumerate(grid):
            step = step * g + pl.program_id(d)
        start, finish = parts(s_ref, so_ref, *sems)
        pl.when(step == 0)(start)
        core(*ins, *outs, *scr)
        pl.when(step == math.prod(grid) - 1)(finish)

    return _pcall(body, name=name, grid=grid, in_specs=list(in_specs) + [ANY_SPEC],
                  out_specs=list(out_specs) + [ANY_SPEC], out_shape=list(out_shape) + [side_shape],
                  scratch_shapes=list(scratch_shapes) + COMM_SCRATCH,
                  compiler_params=_params(("arbitrary",) * len(grid)))(*args, src)


def _make_linear(out_dtype, name):
    @jax.custom_vjp
    def lin(a, a_t, w):
        return _mm(a, w, out_dtype=out_dtype, name=name)

    def fwd(a, a_t, w):
        return lin(a, a_t, w), (a, a_t, w)

    def bwd(res, dy):
        a, a_t, w = res
        da = _mm(dy, w, tb=True, out_dtype=a.dtype, name=name + "_da")
        dw = _mm(a.T if a_t is None else a_t, dy, out_dtype=w.dtype, out_blocked=(w.ndim == 3), name=name + "_dw")
        return da, None if a_t is None else jnp.zeros_like(a_t), dw

    lin.defvjp(fwd, bwd)
    return lin


def _make_hosting_linear(out_dtype, name):
    @jax.custom_vjp
    def lin(a, a_t, w, next_shard):
        nxt = next_shard.astype(BF16)
        side = (_gather_parts, nxt, jax.ShapeDtypeStruct((N_DEV,) + nxt.shape, BF16))
        return tuple(_mm(a, w, out_dtype=out_dtype, side=side, name=name + "_gather"))

    def fwd(a, a_t, w, next_shard):
        return lin(a, a_t, w, next_shard), (a, a_t, w)

    def bwd(res, cts):
        a, a_t, w = res
        dy, d_next = cts
        nb, r, c = d_next.shape
        half = r // 2

        def side(first, count):
            return (functools.partial(_scatter_parts, rows=(first, count)), d_next,
                    jax.ShapeDtypeStruct((nb, count, c), d_next.dtype))

        dw, lo = _mm(a.T if a_t is None else a_t, dy, out_dtype=w.dtype, out_blocked=(w.ndim == 3),
                     side=side(0, half), name=name + "_dw_scatter")
        da, hi = _mm(dy, w, tb=True, out_dtype=a.dtype, side=side(half, r - half), name=name + "_da_scatter")
        d_shard = jnp.concatenate([_sum8_call(lo, name + "_sum_lo"), _sum8_call(hi, name + "_sum_hi")], axis=0)
        return da, None if a_t is None else jnp.zeros_like(a_t), dw, d_shard

    lin.defvjp(fwd, bwd)
    return lin


def _make_rowop(fn, n_row, n_seg, n_bc, out_dtypes, tr, name, transposed=False):
    def specs(args):
        rows, segs, bcs = args[:n_row], args[n_row:n_row + n_seg], args[n_row + n_seg:]
        sp = [pl.BlockSpec((tr, r.shape[1]), lambda i: (i, 0)) for r in rows]
        sp += [pl.BlockSpec((None, 1, s.shape[2]), lambda i: (jnp.minimum(i, 1), 0, 0)) for s in segs]
        sp += [pl.BlockSpec((1, b.shape[1]), lambda i: (0, 0)) for b in bcs]
        return sp

    def out_widths(args):
        tiles = [jax.ShapeDtypeStruct((tr, a.shape[-1]), a.dtype) for a in args[:n_row]]
        tiles += [jax.ShapeDtypeStruct((1, a.shape[-1]), a.dtype) for a in args[n_row:]]
        return [o.shape[1] for o in jax.eval_shape(fn, *tiles)]

    def fwd_call(*args):
        n = args[0].shape[0]
        widths = out_widths(args)
        n_in = len(args)

        def body(*refs):
            vals = [r[...] for r in refs[:n_in]]
            outs = fn(*vals)
            for o_ref, o in zip(refs[n_in:], outs):
                o_ref[...] = o.astype(o_ref.dtype)
            if transposed:
                refs[-1][...] = outs[0].T.astype(refs[-1].dtype)

        out_specs = [pl.BlockSpec((tr, w), lambda i: (i, 0)) for w in widths]
        out_shape = [jax.ShapeDtypeStruct((n, w), d) for w, d in zip(widths, out_dtypes)]
        if transposed:
            out_specs.append(pl.BlockSpec((widths[0], tr), lambda i: (0, i)))
            out_shape.append(jax.ShapeDtypeStruct((widths[0], n), out_dtypes[0]))
        return _pcall(body, name=name, grid=(n // tr,), in_specs=specs(args), out_specs=out_specs,
                      out_shape=out_shape, compiler_params=_params(("parallel",)))(*args)

    def bwd_call(args, cots):
        n = args[0].shape[0]
        n_in = len(args)
        n_ct = len(cots)
        rows, segs, bcs = args[:n_row], args[n_row:n_row + n_seg], args[n_row + n_seg:]

        def body(*refs):
            i = pl.program_id(0)
            vals = [r[...] for r in refs[:n_in]]
            cts = [r[...].astype(F32) for r in refs[n_in:n_in + n_ct]]
            outs = refs[n_in + n_ct:]
            _, vjp = jax.vjp(lambda *v: tuple(fn(*v)), *vals)
            grads = vjp(tuple(cts))
            for o_ref, g in zip(outs[:n_row], grads[:n_row]):
                o_ref[...] = g.astype(o_ref.dtype)
            for o_ref, g in zip(outs[n_row:n_row + n_seg], grads[n_row:n_row + n_seg]):
                @pl.when(i <= 1)
                def _():
                    o_ref[...] = jnp.zeros_like(o_ref)
                o_ref[...] += g.astype(F32)
            for o_ref, g in zip(outs[n_row + n_seg:], grads[n_row + n_seg:]):
                @pl.when(i == 0)
                def _():
                    o_ref[...] = jnp.zeros_like(o_ref)
                o_ref[...] += g.astype(F32)

        out_specs = [pl.BlockSpec((tr, r.shape[1]), lambda i: (i, 0)) for r in rows]
        out_specs += [pl.BlockSpec((None, 1, s.shape[2]), lambda i: (jnp.minimum(i, 1), 0, 0)) for s in segs]
        out_specs += [pl.BlockSpec((1, b.shape[1]), lambda i: (0, 0)) for b in bcs]
        out_shape = [jax.ShapeDtypeStruct(r.shape, r.dtype) for r in rows]
        out_shape += [jax.ShapeDtypeStruct(s.shape, F32) for s in segs]
        out_shape += [jax.ShapeDtypeStruct(b.shape, F32) for b in bcs]
        in_specs = specs(args) + [pl.BlockSpec((tr, c.shape[1]), lambda i: (i, 0)) for c in cots]
        return _pcall(body, name=name + "_bwd", grid=(n // tr,), in_specs=in_specs, out_specs=out_specs,
                      out_shape=out_shape, compiler_params=_params(("arbitrary",)))(*args, *cots)

    @jax.custom_vjp
    def op(*args):
        return tuple(fwd_call(*args))

    def op_fwd(*args):
        return op(*args), args

    def op_bwd(args, cots):
        return tuple(bwd_call(args, list(cots)[:len(out_dtypes)]))

    op.defvjp(op_fwd, op_bwd)
    return op


def _rms(x, g):
    return x * lax.rsqrt(jnp.mean(x * x, axis=-1, keepdims=True) + EPS) * g


def _fn_rmsmod(x, shift, scale, g):
    return (_rms(x.astype(F32), g) * (1.0 + scale) + shift,)


def _fn_resid(x, m, gate, g):
    return (x + gate * _rms(m.astype(F32), g),)


def _fn_gelu_head(y, u, d):
    return (jax.nn.gelu(y + d * u.astype(F32)),)


def _fn_mixcat(y0, y1, y2, y3, gl, s, ps):
    pool = jnp.concatenate([y0, y1, y2, y3], axis=1) * ps
    return (jnp.concatenate([pool, gl * _sigmoid(s)], axis=1),)


def _pool_call(x, col_blk0, ncol, w, n_ctx, n_lat, transpose, out_dtype, name):
    n = n_ctx + n_lat
    cw = LANE
    r = n_ctx
    gap = SUBLANE
    half = w // 2
    lat0 = 2 * gap + n_ctx
    nbuf = 3 * gap + n

    def body(x_ref, o_ref, buf):
        def inv_cnt(seg_len, t0):
            t = t0 + lax.broadcasted_iota(jnp.int32, (r, 1), 0)
            cnt = jnp.minimum(t + half, seg_len) - jnp.maximum(t - half, 0)
            return 1.0 / cnt.astype(F32)

        zero = jnp.zeros((gap, cw), F32)
        buf[0:gap, :] = zero
        buf[gap + n_ctx:lat0, :] = zero
        buf[lat0 + n_lat:nbuf, :] = zero

        def fill(src0, dst0, seg_len, t0):
            v = x_ref[pl.ds(src0, r), :].astype(F32)
            if transpose:
                v = v * inv_cnt(seg_len, t0)
            buf[pl.ds(dst0, r), :] = v

        def compute(src0, dst0, seg_len, t0):
            win = buf[pl.ds(dst0 - gap, r + 2 * gap), :]
            nw = r + 2 * gap
            s = win + pltpu.roll(win, (nw - 1) if transpose else 1, 0)
            for sh in (1, 2, 4):
                if w >= 4 * sh:
                    s = pltpu.roll(s, sh, 0) + pltpu.roll(s, nw - sh, 0)
            ws = s[gap:gap + r]
            if transpose:
                out = ws - x_ref[pl.ds(src0, r), :].astype(F32)
            else:
                out = ws * inv_cnt(seg_len, t0) - win[gap:gap + r]
            o_ref[pl.ds(src0, r), :] = out.astype(o_ref.dtype)

        for step in (fill, compute):
            step(0, gap, n_ctx, 0)

            def lat(i, c, step=step):
                off = pl.multiple_of(i * r, r)
                step(n_ctx + off, lat0 + off, n_lat, off)
                return c
            lax.fori_loop(0, n_lat // r, lat, 0)

    return _pcall(body, name=name, grid=(ncol,),
                  in_specs=[pl.BlockSpec((n, cw), lambda j: (0, col_blk0 + j))],
                  out_specs=pl.BlockSpec((n, cw), lambda j: (0, j)),
                  out_shape=jax.ShapeDtypeStruct((n, ncol * cw), out_dtype),
                  scratch_shapes=[pltpu.VMEM((nbuf, cw), F32)],
                  compiler_params=_params(("parallel",)))(x)


def _make_pool(n_ctx, n_lat, pool_width, mix_width):
    ncol = pool_width // len(POOL_WINDOWS) // LANE

    @jax.custom_vjp
    def pool(u):
        return tuple(_pool_call(u, g * ncol, ncol, w, n_ctx, n_lat, False, BF16, "pool_w%d" % w)
                     for g, w in enumerate(POOL_WINDOWS))

    def fwd(u):
        return pool(u), None

    def bwd(_, cots):
        parts = [_pool_call(ct, 0, ncol, w, n_ctx, n_lat, True, F32, "pool_w%d_bwd" % w)
                 for ct, w in zip(cots, POOL_WINDOWS)]
        parts.append(jnp.zeros((n_ctx + n_lat, mix_width - pool_width), F32))
        return (jnp.concatenate(parts, axis=1),)

    pool.defvjp(fwd, bwd)
    return pool


class _ConvGeom:
    def __init__(self, n_ctx, n_lat):
        self.n_ctx, self.n_lat = n_ctx, n_lat
        self.gap = GRID_W + SUBLANE
        self.r = 2 * GRID_W
        self.ctx0 = self.gap
        self.lat0 = 2 * self.gap + n_ctx
        self.nbuf = 3 * self.gap + n_ctx + n_lat
        self.nwin = self.r + 2 * self.gap
        self.n16 = self.r + 2 * SUBLANE
        assert n_lat % self.r == 0 and n_ctx % self.r == 0

    def zero_gaps(self, buf):
        z = jnp.zeros((self.gap, LANE), F32)
        buf[0:self.gap, :] = z
        buf[self.ctx0 + self.n_ctx:self.lat0, :] = z
        buf[self.lat0 + self.n_lat:self.nbuf, :] = z

    def fill(self, src_ref, buf):
        r = self.r

        def seg(src0, dst0, count):
            def one(i, c):
                off = pl.multiple_of(i * r, r)
                buf[pl.ds(dst0 + off, r), :] = src_ref[pl.ds(src0 + off, r), :].astype(F32)
                return c
            lax.fori_loop(0, count, one, 0)
        seg(0, self.ctx0, self.n_ctx // r)
        seg(self.n_ctx, self.lat0, self.n_lat // r)

    def col_masks(self, nrows, first_col):
        col = (lax.broadcasted_iota(jnp.int32, (nrows, 1), 0) + first_col) & (GRID_W - 1)
        return col == GRID_W - 1, col == 0

    def lat_window(self, buf, i):
        ws = pl.multiple_of(self.lat0 - self.gap + i * self.r, SUBLANE)
        return buf[pl.ds(ws, self.nwin), :]

    def ctx_window(self, buf):
        return buf[self.ctx0 - SUBLANE:self.ctx0 + self.n_ctx + SUBLANE, :]

    def lat_sources(self, win):
        last, first = self.col_masks(self.nwin, GRID_W - SUBLANE)
        return jnp.where(last, 0.0, win), win, jnp.where(first, 0.0, win)

    def row_slice(self, x, di, sign):
        st = self.gap - SUBLANE + sign * (di - 1) * GRID_W
        return x[st:st + self.n16]

    def lat_conv(self, win, w):
        srcs = self.lat_sources(win)
        cs = []
        for dj in range(3):
            acc = None
            for di in range(3):
                term = w[di * 3 + dj] * self.row_slice(srcs[dj], di, 1)
                acc = term if acc is None else acc + term
            cs.append(acc)
        out = pltpu.roll(cs[0], 1, 0) + cs[1] + pltpu.roll(cs[2], self.n16 - 1, 0)
        return out[SUBLANE:SUBLANE + self.r]

    def ctx_conv(self, win, w, transpose=False):
        n = win.shape[0]
        lo, hi = (w[5], w[3]) if transpose else (w[3], w[5])
        out = lo * pltpu.roll(win, 1, 0) + w[4] * win + hi * pltpu.roll(win, n - 1, 0)
        return out[SUBLANE:SUBLANE + self.n_ctx]

    def lat_conv_t(self, dwin, w):
        es = []
        for dj in range(3):
            acc = None
            for di in range(3):
                term = w[di * 3 + dj] * self.row_slice(dwin, di, -1)
                acc = term if acc is None else acc + term
            es.append(acc)
        last, first = self.col_masks(self.n16, GRID_W - SUBLANE)
        out = (jnp.where(last, 0.0, pltpu.roll(es[0], self.n16 - 1, 0)) + es[1]
               + jnp.where(first, 0.0, pltpu.roll(es[2], 1, 0)))
        return out[SUBLANE:SUBLANE + self.r]


def _taps(w_ref):
    return [w_ref[k:k + 1, :] for k in range(9)]


def _conv_specs(n, f_tiles):
    zv = pl.BlockSpec((n, LANE), lambda j: (0, j))
    zg = pl.BlockSpec((n, LANE), lambda j: (0, j + f_tiles))
    wv = pl.BlockSpec((9, LANE), lambda j: (0, j))
    wg = pl.BlockSpec((9, LANE), lambda j: (0, j + f_tiles))
    return zv, zg, wv, wg


def _conv_fwd_call(z, wc, n_ctx, n_lat):
    n, f2 = z.shape
    ft = f2 // 2 // LANE
    geo = _ConvGeom(n_ctx, n_lat)

    def body(zv_ref, zg_ref, wv_ref, wg_ref, o_ref, cv_ref, cg_ref, bv, bg):
        wv, wg = _taps(wv_ref), _taps(wg_ref)
        for src, buf in ((zv_ref, bv), (zg_ref, bg)):
            geo.zero_gaps(buf)
            geo.fill(src, buf)

        def emit(rows, cv, cg):
            o_ref[rows, :] = (cv * cg * _sigmoid(cg)).astype(o_ref.dtype)
            cv_ref[rows, :] = cv.astype(cv_ref.dtype)
            cg_ref[rows, :] = cg.astype(cg_ref.dtype)

        emit(slice(0, n_ctx), geo.ctx_conv(geo.ctx_window(bv), wv), geo.ctx_conv(geo.ctx_window(bg), wg))

        def chunk(i, c):
            off = pl.multiple_of(n_ctx + i * geo.r, SUBLANE)
            emit(pl.ds(off, geo.r), geo.lat_conv(geo.lat_window(bv, i), wv), geo.lat_conv(geo.lat_window(bg, i), wg))
            return c
        lax.fori_loop(0, n_lat // geo.r, chunk, 0)

    tile = pl.BlockSpec((n, LANE), lambda j: (0, j))
    return _pcall(body, name="conv_gate", grid=(ft,), in_specs=list(_conv_specs(n, ft)), out_specs=[tile] * 3,
                  out_shape=[jax.ShapeDtypeStruct((n, f2 // 2), BF16)] * 3,
                  scratch_shapes=[pltpu.VMEM((geo.nbuf, LANE), F32)] * 2,
                  compiler_params=_params(("parallel",)))(z, z, wc, wc)


def _conv_bwd_call(z, wc, cv, cg, da, n_ctx, n_lat):
    n, f2 = z.shape
    ft = f2 // 2 // LANE
    geo = _ConvGeom(n_ctx, n_lat)
    r, n16 = geo.r, geo.n16

    def body(zv_ref, zg_ref, wv_ref, wg_ref, cv_ref, cg_ref, da_ref, dzv_ref, dzg_ref, dwv_ref, dwg_ref,
             bv, bg, dv, dg):
        wv, wg = _taps(wv_ref), _taps(wg_ref)
        for buf in (bv, bg, dv, dg):
            geo.zero_gaps(buf)
        geo.fill(zv_ref, bv)
        geo.fill(zg_ref, bg)

        def gate_grads(cv, cg, d):
            sg = _sigmoid(cg)
            return d * cg * sg, d * cv * sg * (1.0 + cg * (1.0 - sg))

        def tap_sums(d_c, srcs, pad):
            zeros = jnp.zeros((SUBLANE, LANE), F32)
            dce = jnp.concatenate([zeros, d_c, zeros], axis=0)
            m = dce.shape[0]
            shifted = (pltpu.roll(dce, m - 1, 0), dce, pltpu.roll(dce, 1, 0))
            out = []
            for di in range(3):
                for dj in range(3):
                    src = srcs[dj] if pad is None else geo.row_slice(srcs[dj], di, 1)
                    out.append(jnp.sum(shifted[dj] * src, axis=0, keepdims=True))
            return out

        winv, wing = geo.ctx_window(bv), geo.ctx_window(bg)
        d_cv, d_cg = gate_grads(cv_ref[0:n_ctx, :].astype(F32), cg_ref[0:n_ctx, :].astype(F32),
                                da_ref[0:n_ctx, :].astype(F32))
        dv[geo.ctx0:geo.ctx0 + n_ctx, :] = d_cv
        dg[geo.ctx0:geo.ctx0 + n_ctx, :] = d_cg
        zero_row = jnp.zeros((1, LANE), F32)
        acc0 = []
        for d_c, win in ((d_cv, winv), (d_cg, wing)):
            sums = tap_sums(d_c, (win, win, win), None)
            acc0 += [zero_row] * 3 + sums[3:6] + [zero_row] * 3

        def chunk(i, acc):
            winv, wing = geo.lat_window(bv, i), geo.lat_window(bg, i)
            off = pl.multiple_of(n_ctx + i * r, SUBLANE)
            d_cv, d_cg = gate_grads(cv_ref[pl.ds(off, r), :].astype(F32), cg_ref[pl.ds(off, r), :].astype(F32),
                                    da_ref[pl.ds(off, r), :].astype(F32))
            dst = pl.multiple_of(geo.lat0 + i * r, SUBLANE)
            dv[pl.ds(dst, r), :] = d_cv
            dg[pl.ds(dst, r), :] = d_cg
            sums = tap_sums(d_cv, geo.lat_sources(winv), True) + tap_sums(d_cg, geo.lat_sources(wing), True)
            return tuple(a + s for a, s in zip(acc, sums))
        acc = lax.fori_loop(0, n_lat // r, chunk, tuple(acc0))
        for k in range(9):
            dwv_ref[k:k + 1, :] = acc[k]
            dwg_ref[k:k + 1, :] = acc[9 + k]

        for dbuf, w, dz_ref in ((dv, wv, dzv_ref), (dg, wg, dzg_ref)):
            dz_ref[0:n_ctx, :] = geo.ctx_conv(geo.ctx_window(dbuf), w, transpose=True).astype(dz_ref.dtype)

            def chunk_t(i, c, dbuf=dbuf, w=w, dz_ref=dz_ref):
                off = pl.multiple_of(n_ctx + i * r, SUBLANE)
                dz_ref[pl.ds(off, r), :] = geo.lat_conv_t(geo.lat_window(dbuf, i), w).astype(dz_ref.dtype)
                return c
            lax.fori_loop(0, n_lat // r, chunk_t, 0)

    tile = pl.BlockSpec((n, LANE), lambda j: (0, j))
    wtile = pl.BlockSpec((9, LANE), lambda j: (0, j))
    dzv, dzg, dwv, dwg = _pcall(
        body, name="conv_gate_bwd", grid=(ft,), in_specs=list(_conv_specs(n, ft)) + [tile] * 3,
        out_specs=[tile, tile, wtile, wtile],
        out_shape=[jax.ShapeDtypeStruct((n, f2 // 2), z.dtype)] * 2 + [jax.ShapeDtypeStruct((9, f2 // 2), F32)] * 2,
        scratch_shapes=[pltpu.VMEM((geo.nbuf, LANE), F32)] * 4,
        compiler_params=_params(("parallel",)))(z, z, wc, wc, cv, cg, da)
    return jnp.concatenate([dzv, dzg], axis=1), jnp.concatenate([dwv, dwg], axis=1)


def _make_convgate(n_ctx, n_lat):
    @jax.custom_vjp
    def conv(z, wc):
        return _conv_fwd_call(z, wc, n_ctx, n_lat)[0]

    def fwd(z, wc):
        a, cv, cg = _conv_fwd_call(z, wc, n_ctx, n_lat)
        return a, (z, wc, cv, cg)

    def bwd(res, da):
        return _conv_bwd_call(*res, da, n_ctx, n_lat)

    conv.defvjp(fwd, bwd)
    return conv


GROUPS_PER_BLOCK = LANE // SSM_GROUP
STATE_BLOCK = GROUPS_PER_BLOCK * SSM_STATE
SCAN_LANES = STATE_BLOCK


def _cmul(ar, ai, br, bi):
    return ar * br - ai * bi, ar * bi + ai * br


def _lam_tables(lr, li, asc):
    row = lax.broadcasted_iota(jnp.int32, (SUBLANE, lr.shape[1]), 0)
    l1 = (jnp.broadcast_to(lr, row.shape), jnp.broadcast_to(li, row.shape))
    l2 = _cmul(*l1, *l1)
    l4 = _cmul(*l2, *l2)
    pw = l1
    pr = jnp.zeros(row.shape, F32)
    pi = jnp.zeros(row.shape, F32)
    for e in range(1, SUBLANE + 1):
        s = e - 1 if asc else SUBLANE - e
        pr = jnp.where(row == s, pw[0], pr)
        pi = jnp.where(row == s, pw[1], pi)
        pw = _cmul(*pw, *l1)

    def masked(lam_k, k):
        keep = (row >= k) if asc else (row < SUBLANE - k)
        return jnp.where(keep, lam_k[0], 0.0), jnp.where(keep, lam_k[1], 0.0)

    return masked(l1, 1), masked(l2, 2), masked(l4, 4), (pr, pi)


def _tile_scan(br, bi, cr, ci, tables, asc):
    hr, hi = br, bi
    for k, lam_k in zip((1, 2, 4), tables[:3]):
        shift = k if asc else SUBLANE - k
        mr, mi = _cmul(*lam_k, pltpu.roll(hr, shift, 0), pltpu.roll(hi, shift, 0))
        hr, hi = hr + mr, hi + mi
    mr, mi = _cmul(*tables[3], jnp.broadcast_to(cr, br.shape), jnp.broadcast_to(ci, br.shape))
    hr, hi = hr + mr, hi + mi
    last = SUBLANE - 1 if asc else 0
    return hr, hi, hr[last:last + 1, :], hi[last:last + 1, :]


def _chunk_in_time_order(k, n_chunks, asc, adjoint):
    if asc:
        return n_chunks - 1 - k if adjoint else k
    if adjoint:
        return jnp.where(k == n_chunks - 1, 0, k + 1)
    return jnp.where(k == 0, 0, n_chunks - k)


def _dot(a, b):
    return jnp.dot(a, b, preferred_element_type=F32)


def _dot_nt(a, b):
    return lax.dot_general(a, b, (((1,), (1,)), ((), ())), preferred_element_type=F32)


def _scan_chunk(r_buf, i_buf, base, rows, carry, tables, asc, lam_grad=None):
    tiles = rows // SUBLANE
    out_carry, grads = [], []
    for h in range(STATE_BLOCK // SCAN_LANES):
        cols = slice(h * SCAN_LANES, (h + 1) * SCAN_LANES)

        def tile(kt, c, h=h, cols=cols):
            pt = kt if asc else tiles - 1 - kt
            t0 = pl.multiple_of(base + pt * SUBLANE, SUBLANE)
            sr, si, ncr, nci = _tile_scan(r_buf[pl.ds(t0, SUBLANE), cols], i_buf[pl.ds(t0, SUBLANE), cols],
                                          c[0], c[1], tables[h], asc)
            r_buf[pl.ds(t0, SUBLANE), cols] = sr
            i_buf[pl.ds(t0, SUBLANE), cols] = si
            if lam_grad is None:
                return ncr, nci
            h_r, h_i, h_base = lam_grad
            g0 = pl.multiple_of(h_base + pt * SUBLANE, SUBLANE)
            pr, pi = h_r[pl.ds(g0, SUBLANE), cols], h_i[pl.ds(g0, SUBLANE), cols]
            row = lax.broadcasted_iota(jnp.int32, sr.shape, 0)
            if asc:
                nr = jnp.where(row == 0, jnp.broadcast_to(c[0], sr.shape), pltpu.roll(sr, 1, 0))
                ni = jnp.where(row == 0, jnp.broadcast_to(c[1], sr.shape), pltpu.roll(si, 1, 0))
            else:
                nr = jnp.where(row == SUBLANE - 1, jnp.broadcast_to(c[0], sr.shape), pltpu.roll(sr, SUBLANE - 1, 0))
                ni = jnp.where(row == SUBLANE - 1, jnp.broadcast_to(c[1], sr.shape), pltpu.roll(si, SUBLANE - 1, 0))
            return ncr, nci, c[2] + nr * pr + ni * pi, c[3] + ni * pr - nr * pi

        init = (carry[2 * h], carry[2 * h + 1])
        if lam_grad is not None:
            zero = jnp.zeros((SUBLANE, SCAN_LANES), F32)
            init = init + (zero, zero)
        res = lax.fori_loop(0, tiles, tile, init, unroll=2)
        out_carry += [res[0], res[1]]
        grads.append(res[2:])
    return tuple(out_carry), grads


def _ssm_specs(n):
    tok = pl.BlockSpec((n, LANE), lambda q: (0, q))
    lam = pl.BlockSpec((1, STATE_BLOCK), lambda q: (0, q))
    w_in = pl.BlockSpec((None, LANE, STATE_BLOCK), lambda q: (q, 0, 0))
    w_out = pl.BlockSpec((None, STATE_BLOCK, LANE), lambda q: (q, 0, 0))
    return tok, lam, w_in, w_out


def _zero_carry():
    return tuple(jnp.zeros((1, SCAN_LANES), F32) for _ in range(2 * (STATE_BLOCK // SCAN_LANES)))


def _half_tables(lr_ref, li_ref, asc, conj):
    out = []
    for h in range(STATE_BLOCK // SCAN_LANES):
        cols = slice(h * SCAN_LANES, (h + 1) * SCAN_LANES)
        li = li_ref[:, cols]
        out.append(_lam_tables(lr_ref[:, cols], -li if conj else li, asc))
    return out


def _ssm_fwd_call(u, lr, li, b_re, b_im, c_re, c_imn, rc, asc, side=None):
    n, cs = u.shape
    nq, nchunks = cs // LANE, n // rc
    w_states = lr.shape[1]
    tok, lam, w_in, w_out = _ssm_specs(n)
    states = pl.BlockSpec((n, STATE_BLOCK), lambda q: (0, q))

    def body(u_ref, lr_ref, li_ref, br_ref, bi_ref, cr_ref, ci_ref, y_ref, sr_ref, si_ref, h_r, h_i):
        tables = _half_tables(lr_ref, li_ref, asc, False)
        wbr, wbi = br_ref[...].astype(BF16), bi_ref[...].astype(BF16)
        wcr, wci = cr_ref[...].astype(BF16), ci_ref[...].astype(BF16)

        def chunk(k, carry):
            r0 = pl.multiple_of(_chunk_in_time_order(k, nchunks, asc, False) * rc, rc)
            ub = u_ref[pl.ds(r0, rc), :].astype(BF16)
            h_r[...] = _dot(ub, wbr)
            h_i[...] = _dot(ub, wbi)
            carry, _ = _scan_chunk(h_r, h_i, 0, rc, carry, tables, asc)
            hb_r, hb_i = h_r[...].astype(BF16), h_i[...].astype(BF16)
            sr_ref[pl.ds(r0, rc), :] = hb_r
            si_ref[pl.ds(r0, rc), :] = hb_i
            y_ref[pl.ds(r0, rc), :] = _dot(hb_r, wcr) + _dot(hb_i, wci)
            return carry
        lax.fori_loop(0, nchunks, chunk, _zero_carry())

    return _call_with_side(body, name="s5_head", grid=(nq,), in_specs=[tok, lam, lam, w_in, w_in, w_out, w_out],
                           out_specs=[tok, states, states],
                           out_shape=[jax.ShapeDtypeStruct((n, cs), F32)] + [jax.ShapeDtypeStruct((n, w_states), BF16)] * 2,
                           scratch_shapes=[pltpu.VMEM((rc, STATE_BLOCK), F32)] * 2,
                           args=(u, lr, li, b_re, b_im, c_re, c_imn), side=side, semantics=("parallel",))


def _ssm_bwd_call(u, dy, s_re, s_im, lr, li, b_re, b_im, c_re, c_imn, rc, asc, side=None):
    n, cs = u.shape
    nq, nchunks = cs // LANE, n // rc
    tok, lam, w_in, w_out = _ssm_specs(n)
    states = pl.BlockSpec((n, STATE_BLOCK), lambda q: (0, q))

    def body(u_ref, dy_ref, sr_ref, si_ref, lr_ref, li_ref, br_ref, bi_ref, cr_ref, ci_ref,
             du_ref, glr_ref, gli_ref, dbr_ref, dbi_ref, dcr_ref, dci_ref, h_r, h_i, a_r, a_i):
        wbr, wbi = br_ref[...].astype(BF16), bi_ref[...].astype(BF16)
        wcr, wci = cr_ref[...].astype(BF16), ci_ref[...].astype(BF16)

        adj = _half_tables(lr_ref, li_ref, not asc, True)
        for ref in (glr_ref, gli_ref, dbr_ref, dbi_ref, dcr_ref, dci_ref):
            ref[...] = jnp.zeros_like(ref)

        def chunk_adj(k, carry):
            r0 = pl.multiple_of(_chunk_in_time_order(k, nchunks, asc, True) * rc, rc)
            dyc = dy_ref[pl.ds(r0, rc), :]
            dyb = dyc.astype(BF16)
            a_r[...] = _dot_nt(dyb, wcr)
            a_i[...] = _dot_nt(dyb, wci)
            sb_r, sb_i = sr_ref[pl.ds(r0, rc), :], si_ref[pl.ds(r0, rc), :]
            h_r[...] = sb_r.astype(F32)
            h_i[...] = sb_i.astype(F32)
            carry, grads = _scan_chunk(a_r, a_i, 0, rc, carry, adj, not asc, lam_grad=(h_r, h_i, 0))
            for h, (gr, gi) in enumerate(grads):
                cols = slice(h * SCAN_LANES, (h + 1) * SCAN_LANES)
                glr_ref[:, cols] += gr
                gli_ref[:, cols] += gi
            ab_r, ab_i = a_r[...].astype(BF16), a_i[...].astype(BF16)
            du_ref[pl.ds(r0, rc), :] = _dot_nt(ab_r, wbr) + _dot_nt(ab_i, wbi)
            ut = u_ref[pl.ds(r0, rc), :].T.astype(BF16)
            dbr_ref[...] += _dot(ut, ab_r)
            dbi_ref[...] += _dot(ut, ab_i)
            dyt = dyc.T.astype(BF16)
            dcr_ref[...] += _dot(dyt, sb_r)
            dci_ref[...] += _dot(dyt, sb_i)
            return carry
        lax.fori_loop(0, nchunks, chunk_adj, _zero_carry())

    part = pl.BlockSpec((SUBLANE, STATE_BLOCK), lambda q: (0, q))
    w_states = lr.shape[1]
    return _call_with_side(
        body, name="s5_head_bwd", grid=(nq,),
        in_specs=[tok, tok, states, states, lam, lam, w_in, w_in, w_out, w_out],
        out_specs=[tok, part, part, w_in, w_in, w_in, w_in],
        out_shape=[jax.ShapeDtypeStruct((n, cs), F32)] + [jax.ShapeDtypeStruct((SUBLANE, w_states), F32)] * 2
        + [jax.ShapeDtypeStruct((nq, LANE, STATE_BLOCK), F32)] * 4,
        scratch_shapes=[pltpu.VMEM((rc, STATE_BLOCK), F32)] * 4,
        args=(u, dy, s_re, s_im, lr, li, b_re, b_im, c_re, c_imn), side=side, semantics=("parallel",))


def _ssm_cotangents(outs):
    du, glr, gli, dbr, dbi, dcr, dci = outs
    return (du, jnp.sum(glr, axis=0, keepdims=True), jnp.sum(gli, axis=0, keepdims=True), dbr, dbi,
            jnp.swapaxes(dcr, 1, 2), jnp.swapaxes(dci, 1, 2))


def _make_ssm(rc, asc):
    @jax.custom_vjp
    def ssm(u, lr, li, b_re, b_im, c_re, c_imn):
        return _ssm_fwd_call(u, lr, li, b_re, b_im, c_re, c_imn, rc, asc)[0]

    def fwd(*args):
        y, s_re, s_im = _ssm_fwd_call(*args, rc, asc)
        return y, (args, s_re, s_im)

    def bwd(res, dy):
        args, s_re, s_im = res
        return _ssm_cotangents(_ssm_bwd_call(args[0], dy, s_re, s_im, *args[1:], rc, asc))

    ssm.defvjp(fwd, bwd)
    return ssm


def _make_hosting_ssm(rc, asc, name):
    def run(u, lr, li, b_re, b_im, c_re, c_imn, shard):
        sh = shard.astype(BF16)
        side = (_gather_parts, sh, jax.ShapeDtypeStruct((N_DEV,) + sh.shape, BF16))
        return _ssm_fwd_call(u, lr, li, b_re, b_im, c_re, c_imn, rc, asc, side=side)

    @jax.custom_vjp
    def ssm(*args):
        y, _, _, whole = run(*args)
        return y, whole

    def fwd(*args):
        y, s_re, s_im, whole = run(*args)
        return (y, whole), (args[:-1], s_re, s_im)

    def bwd(res, cts):
        args, s_re, s_im = res
        dy, d_whole = cts
        side = (_scatter_parts, d_whole, jax.ShapeDtypeStruct(d_whole.shape, d_whole.dtype))
        outs = _ssm_bwd_call(args[0], dy, s_re, s_im, *args[1:], rc, asc, side=side)
        return _ssm_cotangents(outs[:-1]) + (_sum8_call(outs[-1], name + "_sum"),)

    ssm.defvjp(fwd, bwd)
    return ssm


def _blocks_in(bb):
    g, p, h = bb.shape
    k = GROUPS_PER_BLOCK
    out = jnp.einsum('qgph,gk->qghkp', bb.reshape(g // k, k, p, h), jnp.eye(k, dtype=F32))
    return out.reshape(g // k, k * h, k * p)


def _blocks_out(cc):
    g, h, p = cc.shape
    k = GROUPS_PER_BLOCK
    out = jnp.einsum('qghp,gk->qgpkh', cc.reshape(g // k, k, h, p), jnp.eye(k, dtype=F32))
    return out.reshape(g // k, k * p, k * h)


def _position():
    return lax.axis_index("x"), lax.axis_index("y"), lax.axis_index("c")


def _linear_index():
    x, y, c = _position()
    return 4 * x + 2 * y + c


COMM_SCRATCH = [pltpu.SemaphoreType.DMA((7,)), pltpu.SemaphoreType.DMA((7,)), pltpu.SemaphoreType.DMA(())]
ANY_SPEC = pl.BlockSpec(memory_space=pl.ANY)


def _gather_parts(x_ref, o_ref, send_sems, recv_sems, local_sem):
    x, y, c = _position()
    me, sibling = (x, y, c), (x, y, 1 - c)
    chips = [(1 - x, y), (x, 1 - y), (1 - x, 1 - y)]

    def block(px, py, pc):
        return o_ref.at[4 * px + 2 * py + pc]

    def copy(k, blk, to, src=None):
        return pltpu.make_async_remote_copy(
            src_ref=block(*blk) if src is None else src, dst_ref=block(*blk),
            send_sem=send_sems.at[k], recv_sem=recv_sems.at[k], device_id=to, device_id_type=MESH)

    mine = pltpu.make_async_copy(x_ref, block(*me), local_sem)
    first = [copy(0, me, sibling, src=x_ref)]
    first += [copy(1 + j, me, (*chip, c), src=x_ref) for j, chip in enumerate(chips)]
    passed = [copy(4 + j, (*chip, c), sibling) for j, chip in enumerate(chips)]

    def start():
        mine.start()
        for cp in first:
            cp.start()

    def finish():
        for j, chip in enumerate(chips):
            copy(1 + j, (*chip, c), me).wait_recv()
            passed[j].start()
        copy(0, sibling, me).wait_recv()
        for j, chip in enumerate(chips):
            copy(4 + j, (*chip, 1 - c), me).wait_recv()
        for cp in first + passed:
            cp.wait_send()
        mine.wait()

    return start, finish


def _scatter_parts(g_ref, o_ref, send_sems, recv_sems, local_sem, rows=None):
    x, y, c = _position()

    def block(p):
        return g_ref.at[p] if rows is None else g_ref.at[p, pl.ds(rows[0], rows[1])]

    mine = pltpu.make_async_copy(block(4 * x + 2 * y + c), o_ref.at[0], local_sem)
    copies = []
    for k in range(1, N_DEV):
        px = 1 - x if k & 4 else x
        py = 1 - y if k & 2 else y
        pc = 1 - c if k & 1 else c
        copies.append(pltpu.make_async_remote_copy(
            src_ref=block(4 * px + 2 * py + pc), dst_ref=o_ref.at[k],
            send_sem=send_sems.at[k - 1], recv_sem=recv_sems.at[k - 1],
            device_id=(px, py, pc), device_id_type=MESH))

    def start():
        mine.start()
        for cp in copies:
            cp.start()

    def finish():
        for cp in copies:
            cp.wait()
        mine.wait()

    return start, finish


def _comm_call(parts, src, out_shape, name):
    def body(s_ref, o_ref, *sems):
        start, finish = parts(s_ref, o_ref, *sems)
        start()
        finish()

    return _pcall(body, name=name, in_specs=[ANY_SPEC], out_specs=ANY_SPEC, out_shape=out_shape,
                  scratch_shapes=COMM_SCRATCH)(src)


def _ag_call(shard, name):
    return _comm_call(_gather_parts, shard, jax.ShapeDtypeStruct((N_DEV,) + shard.shape, shard.dtype), name)


def _rs_call(g, name):
    return _comm_call(_scatter_parts, g, jax.ShapeDtypeStruct(g.shape, g.dtype), name)


def _sum8_call(parts, name):
    _, r, c = parts.shape
    tr = _tile(r, max(PACK, (4 << 20) // (N_DEV * c * parts.dtype.itemsize) // PACK * PACK), PACK)

    def body(p_ref, o_ref):
        acc = p_ref[0].astype(F32)
        for k in range(1, N_DEV):
            acc = acc + p_ref[k].astype(F32)
        o_ref[...] = acc

    return _pcall(body, name=name, grid=(r // tr,),
                  in_specs=[pl.BlockSpec((N_DEV, tr, c), lambda i: (0, i, 0))],
                  out_specs=pl.BlockSpec((tr, c), lambda i: (i, 0)),
                  out_shape=jax.ShapeDtypeStruct((r, c), F32), compiler_params=_params(("parallel",)))(parts)


def _reduce_scatter(g, name):
    return _sum8_call(_rs_call(g, name), name + "_sum")


def _make_gather(dtype, name):
    @jax.custom_vjp
    def gather(shard):
        return _ag_call(shard.astype(dtype), name)

    def fwd(shard):
        return gather(shard), None

    def bwd(_, ct):
        return (_reduce_scatter(ct, name + "_rs"),)

    gather.defvjp(fwd, bwd)
    return gather


def _adam_call(w, g, m, v, name):
    r, c = w.shape
    tr = _tile(r, max(SUBLANE, (1 << 20) // (4 * c) // SUBLANE * SUBLANE), SUBLANE)

    def body(w_ref, g_ref, m_ref, v_ref, d_ref, mo_ref, vo_ref):
        gv = g_ref[...]
        m2 = ADAM_B1 * m_ref[...] + (1.0 - ADAM_B1) * gv
        v2 = ADAM_B2 * v_ref[...] + (1.0 - ADAM_B2) * (gv * gv)
        m_hat = m2 / (1.0 - ADAM_B1 ** ADAM_STEP)
        v_hat = v2 / (1.0 - ADAM_B2 ** ADAM_STEP)
        d_ref[...] = -ADAM_LR * (m_hat / (jnp.sqrt(v_hat) + ADAM_EPS) + ADAM_WD * w_ref[...])
        mo_ref[...] = m2
        vo_ref[...] = v2

    spec = pl.BlockSpec((tr, c), lambda i: (i, 0))
    return _pcall(body, name=name, grid=(r // tr,), in_specs=[spec] * 4, out_specs=[spec] * 3,
                  out_shape=[jax.ShapeDtypeStruct((r, c), F32)] * 3, compiler_params=_params(("parallel",)))(w, g, m, v)


def _loss_call(y, target, tr):
    n, d = y.shape

    def body(y_ref, t_ref, s_ref, dy_ref):
        i = pl.program_id(0)
        err = y_ref[...] - t_ref[...]
        dy_ref[...] = err * (1.0 / d)
        part = jnp.sum(jnp.sum(err * err, axis=1, keepdims=True), axis=0, keepdims=True)

        @pl.when(i == 0)
        def _():
            s_ref[...] = jnp.zeros_like(s_ref)
        s_ref[...] += part

    spec = pl.BlockSpec((tr, d), lambda i: (i, 0))
    return _pcall(body, name="loss_head", grid=(n // tr,), in_specs=[spec, spec],
                  out_specs=[pl.BlockSpec((1, 1), lambda i: (0, 0)), spec],
                  out_shape=[jax.ShapeDtypeStruct((1, 1), F32), jax.ShapeDtypeStruct((n, d), F32)],
                  compiler_params=_params(("arbitrary",)))(y, target)


def _pack(arrays, rows_mult):
    flat = jnp.concatenate([a.reshape(-1).astype(F32) for a in arrays])
    rows = -(-flat.shape[0] // LANE)
    rows = -(-rows // rows_mult) * rows_mult
    return jnp.pad(flat, (0, rows * LANE - flat.shape[0])).reshape(rows, LANE)


def _unpack(buf, shapes, lead=()):
    flat = buf.reshape(lead + (-1,))
    out, pos = [], 0
    for s in shapes:
        size = math.prod(s)
        out.append(flat[..., pos:pos + size].reshape(lead + tuple(s)))
        pos += size
    return out


def _s5_discretise(a_re, a_im, log_dt, b_re, b_im):
    dt = jnp.exp(log_dt)[:, None]
    mag = jnp.exp(a_re * dt)
    lam_re = mag * jnp.cos(a_im * dt)
    lam_im = mag * jnp.sin(a_im * dt)
    denom = a_re * a_re + a_im * a_im
    nr, ni = lam_re - 1.0, lam_im
    f_re = (nr * a_re + ni * a_im) / denom
    f_im = (ni * a_re - nr * a_im) / denom
    bb_re = f_re[..., None] * b_re - f_im[..., None] * b_im
    bb_im = f_re[..., None] * b_im + f_im[..., None] * b_re
    return lam_re, lam_im, bb_re, bb_im


def _forward(x, p, ctx, s_c, n_ctx, n_lat):
    d_model = x.shape[1]
    depth = len(p['w_in'])
    mix = p['w_in'][0].shape[1]
    pool_width = p['pool_scale'].shape[1]
    tr = n_ctx
    me = _linear_index()

    rmsmod = _make_rowop(_fn_rmsmod, 1, 2, 1, [BF16], tr, "rms_modulate", transposed=True)
    resid = _make_rowop(_fn_resid, 2, 1, 1, [F32], tr, "gated_residual")
    gelu_head = _make_rowop(_fn_gelu_head, 2, 0, 1, [F32], tr, "ssm_gelu")
    mixcat = _make_rowop(_fn_mixcat, 6, 0, 1, [BF16], tr, "mix_concat", transposed=True)
    pool = _make_pool(n_ctx, n_lat, pool_width, mix)
    convgate = _make_convgate(n_ctx, n_lat)
    ssm = [_make_ssm(tr, True), _make_ssm(tr, False)]
    lin_f32 = {k: _make_linear(F32, k) for k in ("w_in", "w_pool", "w_glu", "w_out", "w_down", "ada")}
    lin_up = _make_linear(BF16, "w_up")
    gather_big = {k: _make_gather(BF16, "gather_" + k) for k in BIG_SHARDED}
    gather_small = _make_gather(F32, "gather_small")
    gather_mod = _make_gather(F32, "gather_mod")

    whole = {k: gather_big[k](p[k][0]) for k in ('w_in', 'w_out')}
    host = {k: _make_hosting_linear(BF16 if k == 'w_up' else F32, k) for k in BIG_SHARDED}
    ssm_host = [_make_hosting_ssm(tr, True, "s5_w_up"), _make_hosting_ssm(tr, False, "s5_w_down")]

    def big_linear(k, a, l, a_t=None):
        w = whole[k] if k == 'w_up' else whole[k].reshape(-1, whole[k].shape[2])
        if l + 1 == depth:
            return (lin_up if k == 'w_up' else lin_f32[k])(a, a_t, w)
        y, whole[k] = host[k](a, a_t, w, p[k][l + 1])
        return y

    small = [p[k] for k in SMALL_SHARDED]
    packed = gather_small(_pack(small, PACK))
    w_pool, w_glu, w_conv = _unpack(packed, [a.shape for a in small], lead=(N_DEV,))
    w_pool = jnp.moveaxis(w_pool, 0, 2).reshape(depth, len(POOL_WINDOWS), -1, w_pool.shape[-1])
    w_glu = jnp.moveaxis(w_glu, 0, 1).reshape(depth, -1, w_glu.shape[-1])
    w_conv = jnp.moveaxis(w_conv, 0, 3).reshape(depth, 9, -1)

    s_rows = jnp.concatenate([s_c, jax.nn.silu(p['c_ctx'])[None, :],
                              jnp.zeros((PACK - N_DEV - 1, d_model), F32)], axis=0)
    cols = p['w_ada'][0].shape[1]
    b_loc = lax.dynamic_slice_in_dim(p['b_ada'], me * cols, cols, axis=1)
    mod_loc = jnp.stack([lin_f32['ada'](s_rows, None, p['w_ada'][l]) + b_loc[l][None, :] for l in range(depth)])
    mod = gather_mod(mod_loc.reshape(depth * PACK, cols)).reshape(N_DEV, depth, PACK, cols)
    mod = jnp.moveaxis(mod, 0, 2).reshape(depth, PACK, 6, d_model)
    mod_lat = lax.dynamic_index_in_dim(mod, me, axis=1, keepdims=False)
    mod_ctx = mod[:, N_DEV]

    xs = jnp.concatenate([ctx, x], axis=0)
    for l in range(depth):
        def seg(k):
            return jnp.stack([mod_ctx[l, k], mod_lat[l, k]]).reshape(2, 1, d_model)

        def row(name):
            return p[name][l].reshape(1, -1)

        h1, h1_t = rmsmod(xs, seg(0), seg(1), row('g_pre_mix'))
        u = big_linear('w_in', h1, l, h1_t)
        pooled = pool(u)
        yp = [lin_f32['w_pool'](pooled[g], None, w_pool[l, g]) for g in range(len(POOL_WINDOWS))]
        u_ssm = u[:, pool_width:]
        ys = []
        for d in range(2):
            lam_re, lam_im, bb_re, bb_im = _s5_discretise(
                p['ssm_a_re'][l, d], p['ssm_a_im'][l, d], p['ssm_log_dt'][l, d], p['ssm_b_re'][l, d], p['ssm_b_im'][l, d])
            args = (u_ssm, lam_re.reshape(1, -1), lam_im.reshape(1, -1), _blocks_in(bb_re), _blocks_in(bb_im),
                    _blocks_out(p['ssm_c_re'][l, d]), _blocks_out(-p['ssm_c_im'][l, d]))
            if l == 0:
                k = ('w_up', 'w_down')[d]
                y_dir, whole[k] = ssm_host[d](*args, p[k][0])
                ys.append(y_dir)
            else:
                ys.append(ssm[d](*args))
        gl, = gelu_head(ys[0] + ys[1], u_ssm, row('ssm_d'))
        s = lin_f32['w_glu'](gl, None, w_glu[l])
        cat, cat_t = mixcat(*yp, gl, s, row('pool_scale'))
        mixed = big_linear('w_out', cat, l, cat_t)
        x1, = resid(xs, mixed, seg(2), row('g_post_mix'))
        h2, h2_t = rmsmod(x1, seg(3), seg(4), row('g_pre_ffn'))
        z = big_linear('w_up', h2, l, h2_t)
        a = convgate(z, w_conv[l])
        f = big_linear('w_down', a, l)
        xs, = resid(x1, f, seg(5), row('g_post_ffn'))
    return xs[n_ctx:]


def _as_rows(a):
    return a.reshape(-1, a.shape[-1])


def kernel(x, c, ctx, c_ctx, w_ada, b_ada, w_in, w_pool, pool_scale, ssm_a_re, ssm_a_im, ssm_log_dt, ssm_b_re, ssm_b_im, ssm_c_re, ssm_c_im, ssm_d, w_glu, w_out, g_pre_mix, g_post_mix, g_pre_ffn, g_post_ffn, w_up, w_conv, w_down, loss_target, m_c_ctx, m_w_ada, m_b_ada, m_w_in, m_w_pool, m_pool_scale, m_ssm_a_re, m_ssm_a_im, m_ssm_log_dt, m_ssm_b_re, m_ssm_b_im, m_ssm_c_re, m_ssm_c_im, m_ssm_d, m_w_glu, m_w_out, m_g_pre_mix, m_g_post_mix, m_g_pre_ffn, m_g_post_ffn, m_w_up, m_w_conv, m_w_down, v_c_ctx, v_w_ada, v_b_ada, v_w_in, v_w_pool, v_pool_scale, v_ssm_a_re, v_ssm_a_im, v_ssm_log_dt, v_ssm_b_re, v_ssm_b_im, v_ssm_c_re, v_ssm_c_im, v_ssm_d, v_w_glu, v_w_out, v_g_pre_mix, v_g_post_mix, v_g_pre_ffn, v_g_post_ffn, v_w_up, v_w_conv, v_w_down):
    weights = dict(zip(WEIGHTS, (c_ctx, w_ada, b_ada, w_in, w_pool, pool_scale, ssm_a_re, ssm_a_im, ssm_log_dt, ssm_b_re,
                                 ssm_b_im, ssm_c_re, ssm_c_im, ssm_d, w_glu, w_out, g_pre_mix, g_post_mix, g_pre_ffn,
                                 g_post_ffn, w_up, w_conv, w_down)))
    m_in = dict(zip(WEIGHTS, (m_c_ctx, m_w_ada, m_b_ada, m_w_in, m_w_pool, m_pool_scale, m_ssm_a_re, m_ssm_a_im,
                              m_ssm_log_dt, m_ssm_b_re, m_ssm_b_im, m_ssm_c_re, m_ssm_c_im, m_ssm_d, m_w_glu, m_w_out,
                              m_g_pre_mix, m_g_post_mix, m_g_pre_ffn, m_g_post_ffn, m_w_up, m_w_conv, m_w_down)))
    v_in = dict(zip(WEIGHTS, (v_c_ctx, v_w_ada, v_b_ada, v_w_in, v_w_pool, v_pool_scale, v_ssm_a_re, v_ssm_a_im,
                              v_ssm_log_dt, v_ssm_b_re, v_ssm_b_im, v_ssm_c_re, v_ssm_c_im, v_ssm_d, v_w_glu, v_w_out,
                              v_g_pre_mix, v_g_post_mix, v_g_pre_ffn, v_g_post_ffn, v_w_up, v_w_conv, v_w_down)))
    depth = w_in.shape[0]
    n_lat, d_model = x.shape[1], x.shape[2]
    n_ctx = ctx.shape[1]
    per_layer = BIG_SHARDED + ['w_ada']

    c_rows = jnp.concatenate([c, jnp.zeros((SUBLANE - 1, d_model), F32)], axis=0)
    s_c = jax.nn.silu(_ag_call(c_rows, "gather_c")[:, 0, :])

    params = {k: ([w[l] for l in range(depth)] if k in per_layer else w) for k, w in weights.items()}

    def run(x2d, prm):
        return _forward(x2d, prm, ctx[0], s_c, n_ctx, n_lat)

    y, vjp = jax.vjp(run, x[0], params)
    sq, dy = _loss_call(y, loss_target[0], n_ctx)
    loss = lax.psum(0.5 * sq[0, 0] / d_model, ("x", "y", "c"))
    gx, grads = vjp(dy)
    grads = {k: (jnp.stack(g) if k in per_layer else g) for k, g in grads.items()}

    rep_shapes = [weights[k].shape for k in REPLICATED]
    contrib = _pack([grads[k] for k in REPLICATED], N_DEV * PACK)
    rows = contrib.shape[0] // N_DEV
    total = _ag_call(_reduce_scatter(contrib.reshape(N_DEV, rows, LANE), "reduce_replicated"), "gather_replicated")
    total = total.reshape(N_DEV * rows, LANE)
    for k, g in zip(REPLICATED, _unpack(total, rep_shapes)):
        grads[k] = g

    delta, new_m, new_v = {}, {}, {}
    rep = [_pack([src[k] for k in REPLICATED], N_DEV * PACK) for src in (weights, m_in, v_in)]
    upd = _adam_call(rep[0], total, rep[1], rep[2], "adamw_replicated")
    for out, buf in zip((delta, new_m, new_v), upd):
        out.update(zip(REPLICATED, _unpack(buf, rep_shapes)))
    for k in WEIGHTS:
        if k in REPLICATED:
            continue
        upd = _adam_call(_as_rows(weights[k]), _as_rows(grads[k]), _as_rows(m_in[k]), _as_rows(v_in[k]), "adamw_" + k)
        for out, buf in zip((delta, new_m, new_v), upd):
            out[k] = buf.reshape(weights[k].shape)

    return (loss, gx[None], *[grads[k] for k in WEIGHTS], *[delta[k] for k in WEIGHTS],
            *[new_m[k] for k in WEIGHTS], *[new_v[k] for k in WEIGHTS])
```

```python
import functools
import math

import jax
import jax.numpy as jnp
from jax import lax
from jax.experimental import pallas as pl
from jax.experimental.pallas import tpu as pltpu

F32 = jnp.float32
BF16 = jnp.bfloat16

N_DEV = 8
GRID_W = 64
POOL_WINDOWS = (2, 4, 8, 16)
SSM_GROUP = 16
SSM_STATE = 64
EPS = 1e-6
ADAM_LR = 0.001
ADAM_B1 = 0.9
ADAM_B2 = 0.999
ADAM_EPS = 1e-08
ADAM_WD = 0.01
ADAM_STEP = 10

SUBLANE = 8
PACK = 16
LANE = 128
VMEM_LIMIT = 56 * 1024 * 1024
MM_VMEM_BUDGET = 40 * 1024 * 1024
MESH = pl.DeviceIdType.MESH

WEIGHTS = ['c_ctx', 'w_ada', 'b_ada', 'w_in', 'w_pool', 'pool_scale', 'ssm_a_re', 'ssm_a_im', 'ssm_log_dt',
           'ssm_b_re', 'ssm_b_im', 'ssm_c_re', 'ssm_c_im', 'ssm_d', 'w_glu', 'w_out', 'g_pre_mix',
           'g_post_mix', 'g_pre_ffn', 'g_post_ffn', 'w_up', 'w_conv', 'w_down']
REPLICATED = ['c_ctx', 'b_ada', 'pool_scale', 'ssm_a_re', 'ssm_a_im', 'ssm_log_dt', 'ssm_b_re', 'ssm_b_im',
              'ssm_c_re', 'ssm_c_im', 'ssm_d', 'g_pre_mix', 'g_post_mix', 'g_pre_ffn', 'g_post_ffn']
BIG_SHARDED = ['w_in', 'w_out', 'w_up', 'w_down']
SMALL_SHARDED = ['w_pool', 'w_glu', 'w_conv']


def _pcall(body, **kw):
    return pl.pallas_call(body, **kw)


def _params(sem=None):
    return pltpu.CompilerParams(dimension_semantics=sem, vmem_limit_bytes=VMEM_LIMIT)


def _tile(n, cap, mult):
    if n <= cap:
        return n
    best = None
    d = mult
    while d <= cap:
        if n % d == 0:
            best = d
        d += mult
    assert best is not None, (n, cap, mult)
    return best


def _sigmoid(x):
    return 1.0 / (1.0 + jnp.exp(-x))


def _mm(a, b, *, tb=False, out_dtype=F32, out_blocked=False, side=None, name):
    b_blocked = b.ndim == 3
    M, K = a.shape
    if b_blocked:
        nb, br, bc = b.shape
        N, Kb = (br, nb * bc) if tb else (nb * bc, br)
    else:
        N, Kb = (b.shape if tb else b.shape[::-1])
    assert K == Kb, (a.shape, b.shape, tb)
    tm = _tile(M, 1408, PACK)
    tk = bc if (b_blocked and tb) else _tile(K, 2304, LANE)

    def vmem_bytes(tn):
        tiles = tm * tk * a.dtype.itemsize + tk * tn * b.dtype.itemsize + tm * tn * jnp.dtype(out_dtype).itemsize
        return 2 * tiles + tm * tn * 4

    if out_blocked:
        assert N % N_DEV == 0
        tn = N // N_DEV
    elif b_blocked and not tb:
        tn = bc
    else:
        tn = _tile(N, 1024, LANE)
        if vmem_bytes(tn) > MM_VMEM_BUDGET:
            tn = _tile(N, 512, LANE)
    if b_blocked and not tb:
        assert tn == bc
    nm, nn, nk = M // tm, N // tn, K // tk

    a_spec = pl.BlockSpec((tm, tk), lambda i, j, k: (i, k))
    if b_blocked:
        if tb:
            b_spec = pl.BlockSpec((None, tn, tk), lambda i, j, k: (k, j, 0))
        else:
            b_spec = pl.BlockSpec((None, tk, tn), lambda i, j, k: (j, k, 0))
    else:
        b_spec = pl.BlockSpec((tn, tk), lambda i, j, k: (j, k)) if tb else pl.BlockSpec((tk, tn), lambda i, j, k: (k, j))
    if out_blocked:
        o_spec = pl.BlockSpec((None, tm, tn), lambda i, j, k: (j, i, 0))
        o_shape = jax.ShapeDtypeStruct((N_DEV, M, tn), out_dtype)
    else:
        o_spec = pl.BlockSpec((tm, tn), lambda i, j, k: (i, j))
        o_shape = jax.ShapeDtypeStruct((M, N), out_dtype)
    dims = (((1,), ((1 if tb else 0),)), ((), ()))

    def matmul(a_ref, b_ref, o_ref, acc_ref):
        k = pl.program_id(2)
        part = lax.dot_general(a_ref[...].astype(BF16), b_ref[...].astype(BF16), dims, preferred_element_type=F32)
        if nk == 1:
            o_ref[...] = part.astype(o_ref.dtype)
        else:
            @pl.when(k == 0)
            def _():
                acc_ref[...] = part

            @pl.when(k > 0)
            def _():
                acc_ref[...] += part

            @pl.when(k == nk - 1)
            def _():
                o_ref[...] = acc_ref[...].astype(o_ref.dtype)

    out = _call_with_side(matmul, name=name, grid=(nm, nn, nk), in_specs=[a_spec, b_spec], out_specs=[o_spec],
                          out_shape=[o_shape], scratch_shapes=[pltpu.VMEM((tm, tn), F32)], args=(a, b), side=side,
                          semantics=("parallel", "parallel", "arbitrary"))
    return out[0] if side is None else out


def _call_with_side(core, *, name, grid, in_specs, out_specs, out_shape, scratch_shapes, args, side, semantics):
    if side is None:
        return _pcall(core, name=name, grid=grid, in_specs=in_specs, out_specs=out_specs, out_shape=out_shape,
                      scratch_shapes=scratch_shapes, compiler_params=_params(semantics))(*args)
    parts, src, side_shape = side
    n_in, n_out, n_scr = len(in_specs), len(out_specs), len(scratch_shapes)

    def body(*refs):
        ins, s_ref = refs[:n_in], refs[n_in]
        outs, so_ref = refs[n_in + 1:n_in + 1 + n_out], refs[n_in + 1 + n_out]
        scr, sems = refs[n_in + 2 + n_out:n_in + 2 + n_out + n_scr], refs[n_in + 2 + n_out + n_scr:]
        step = 0
        for d, g in enumerate(grid):
            step = step * g + pl.program_id(d)
        start, finish = parts(s_ref, so_ref, *sems)
        pl.when(step == 0)(start)
        core(*ins, *outs, *scr)
        pl.when(step == math.prod(grid) - 1)(finish)

    return _pcall(body, name=name, grid=grid, in_specs=list(in_specs) + [ANY_SPEC],
                  out_specs=list(out_specs) + [ANY_SPEC], out_shape=list(out_shape) + [side_shape],
                  scratch_shapes=list(scratch_shapes) + COMM_SCRATCH,
                  compiler_params=_params(("arbitrary",) * len(grid)))(*args, src)


def _make_linear(out_dtype, name):
    @jax.custom_vjp
    def lin(a, a_t, w):
        return _mm(a, w, out_dtype=out_dtype, name=name)

    def fwd(a, a_t, w):
        return lin(a, a_t, w), (a, a_t, w)

    def bwd(res, dy):
        a, a_t, w = res
        da = _mm(dy, w, tb=True, out_dtype=a.dtype, name=name + "_da")
        dw = _mm(a.T if a_t is None else a_t, dy, out_dtype=w.dtype, out_blocked=(w.ndim == 3), name=name + "_dw")
        return da, None if a_t is None else jnp.zeros_like(a_t), dw

    lin.defvjp(fwd, bwd)
    return lin


def _make_hosting_linear(out_dtype, name):
    @jax.custom_vjp
    def lin(a, a_t, w, next_shard):
        nxt = next_shard.astype(BF16)
        side = (_gather_parts, nxt, jax.ShapeDtypeStruct((N_DEV,) + nxt.shape, BF16))
        return tuple(_mm(a, w, out_dtype=out_dtype, side=side, name=name + "_gather"))

    def fwd(a, a_t, w, next_shard):
        return lin(a, a_t, w, next_shard), (a, a_t, w)

    def bwd(res, cts):
        a, a_t, w = res
        dy, d_next = cts
        nb, r, c = d_next.shape
        half = r // 2

        def side(first, count):
            return (functools.partial(_scatter_parts, rows=(first, count)), d_next,
                    jax.ShapeDtypeStruct((nb, count, c), d_next.dtype))

        dw, lo = _mm(a.T if a_t is None else a_t, dy, out_dtype=w.dtype, out_blocked=(w.ndim == 3),
                     side=side(0, half), name=name + "_dw_scatter")
        da, hi = _mm(dy, w, tb=True, out_dtype=a.dtype, side=side(half, r - half), name=name + "_da_scatter")
        d_shard = jnp.concatenate([_sum8_call(lo, name + "_sum_lo"), _sum8_call(hi, name + "_sum_hi")], axis=0)
        return da, None if a_t is None else jnp.zeros_like(a_t), dw, d_shard

    lin.defvjp(fwd, bwd)
    return lin


def _make_rowop(fn, n_row, n_seg, n_bc, out_dtypes, tr, name, transposed=False):
    def specs(args):
        rows, segs, bcs = args[:n_row], args[n_row:n_row + n_seg], args[n_row + n_seg:]
        sp = [pl.BlockSpec((tr, r.shape[1]), lambda i: (i, 0)) for r in rows]
        sp += [pl.BlockSpec((None, 1, s.shape[2]), lambda i: (jnp.minimum(i, 1), 0, 0)) for s in segs]
        sp += [pl.BlockSpec((1, b.shape[1]), lambda i: (0, 0)) for b in bcs]
        return sp

    def out_widths(args):
        tiles = [jax.ShapeDtypeStruct((tr, a.shape[-1]), a.dtype) for a in args[:n_row]]
        tiles += [jax.ShapeDtypeStruct((1, a.shape[-1]), a.dtype) for a in args[n_row:]]
        return [o.shape[1] for o in jax.eval_shape(fn, *tiles)]

    def fwd_call(*args):
        n = args[0].shape[0]
        widths = out_widths(args)
        n_in = len(args)

        def body(*refs):
            vals = [r[...] for r in refs[:n_in]]
            outs = fn(*vals)
            for o_ref, o in zip(refs[n_in:], outs):
                o_ref[...] = o.astype(o_ref.dtype)
            if transposed:
                refs[-1][...] = outs[0].T.astype(refs[-1].dtype)

        out_specs = [pl.BlockSpec((tr, w), lambda i: (i, 0)) for w in widths]
        out_shape = [jax.ShapeDtypeStruct((n, w), d) for w, d in zip(widths, out_dtypes)]
        if transposed:
            out_specs.append(pl.BlockSpec((widths[0], tr), lambda i: (0, i)))
            out_shape.append(jax.ShapeDtypeStruct((widths[0], n), out_dtypes[0]))
        return _pcall(body, name=name, grid=(n // tr,), in_specs=specs(args), out_specs=out_specs,
                      out_shape=out_shape, compiler_params=_params(("parallel",)))(*args)

    def bwd_call(args, cots):
        n = args[0].shape[0]
        n_in = len(args)
        n_ct = len(cots)
        rows, segs, bcs = args[:n_row], args[n_row:n_row + n_seg], args[n_row + n_seg:]

        def body(*refs):
            i = pl.program_id(0)
            vals = [r[...] for r in refs[:n_in]]
            cts = [r[...].astype(F32) for r in refs[n_in:n_in + n_ct]]
            outs = refs[n_in + n_ct:]
            _, vjp = jax.vjp(lambda *v: tuple(fn(*v)), *vals)
            grads = vjp(tuple(cts))
            for o_ref, g in zip(outs[:n_row], grads[:n_row]):
                o_ref[...] = g.astype(o_ref.dtype)
            for o_ref, g in zip(outs[n_row:n_row + n_seg], grads[n_row:n_row + n_seg]):
                @pl.when(i <= 1)
                def _():
                    o_ref[...] = jnp.zeros_like(o_ref)
                o_ref[...] += g.astype(F32)
            for o_ref, g in zip(outs[n_row + n_seg:], grads[n_row + n_seg:]):
                @pl.when(i == 0)
                def _():
                    o_ref[...] = jnp.zeros_like(o_ref)
                o_ref[...] += g.astype(F32)

        out_specs = [pl.BlockSpec((tr, r.shape[1]), lambda i: (i, 0)) for r in rows]
        out_specs += [pl.BlockSpec((None, 1, s.shape[2]), lambda i: (jnp.minimum(i, 1), 0, 0)) for s in segs]
        out_specs += [pl.BlockSpec((1, b.shape[1]), lambda i: (0, 0)) for b in bcs]
        out_shape = [jax.ShapeDtypeStruct(r.shape, r.dtype) for r in rows]
        out_shape += [jax.ShapeDtypeStruct(s.shape, F32) for s in segs]
        out_shape += [jax.ShapeDtypeStruct(b.shape, F32) for b in bcs]
        in_specs = specs(args) + [pl.BlockSpec((tr, c.shape[1]), lambda i: (i, 0)) for c in cots]
        return _pcall(body, name=name + "_bwd", grid=(n // tr,), in_specs=in_specs, out_specs=out_specs,
                      out_shape=out_shape, compiler_params=_params(("arbitrary",)))(*args, *cots)

    @jax.custom_vjp
    def op(*args):
        return tuple(fwd_call(*args))

    def op_fwd(*args):
        return op(*args), args

    def op_bwd(args, cots):
        return tuple(bwd_call(args, list(cots)[:len(out_dtypes)]))

    op.defvjp(op_fwd, op_bwd)
    return op


def _rms(x, g):
    return x * lax.rsqrt(jnp.mean(x * x, axis=-1, keepdims=True) + EPS) * g


def _fn_rmsmod(x, shift, scale, g):
    return (_rms(x.astype(F32), g) * (1.0 + scale) + shift,)


def _fn_resid(x, m, gate, g):
    return (x + gate * _rms(m.astype(F32), g),)


def _fn_gelu_head(y, u, d):
    return (jax.nn.gelu(y + d * u.astype(F32)),)


def _fn_mixcat(y0, y1, y2, y3, gl, s, ps):
    pool = jnp.concatenate([y0, y1, y2, y3], axis=1) * ps
    return (jnp.concatenate([pool, gl * _sigmoid(s)], axis=1),)


def _pool_call(x, col_blk0, ncol, w, n_ctx, n_lat, transpose, out_dtype, name):
    n = n_ctx + n_lat
    cw = LANE
    r = n_ctx
    gap = SUBLANE
    half = w // 2
    lat0 = 2 * gap + n_ctx
    nbuf = 3 * gap + n

    def body(x_ref, o_ref, buf):
        def inv_cnt(seg_len, t0):
            t = t0 + lax.broadcasted_iota(jnp.int32, (r, 1), 0)
            cnt = jnp.minimum(t + half, seg_len) - jnp.maximum(t - half, 0)
            return 1.0 / cnt.astype(F32)

        zero = jnp.zeros((gap, cw), F32)
        buf[0:gap, :] = zero
        buf[gap + n_ctx:lat0, :] = zero
        buf[lat0 + n_lat:nbuf, :] = zero

        def fill(src0, dst0, seg_len, t0):
            v = x_ref[pl.ds(src0, r), :].astype(F32)
            if transpose:
                v = v * inv_cnt(seg_len, t0)
            buf[pl.ds(dst0, r), :] = v

        def compute(src0, dst0, seg_len, t0):
            win = buf[pl.ds(dst0 - gap, r + 2 * gap), :]
            nw = r + 2 * gap
            s = win + pltpu.roll(win, (nw - 1) if transpose else 1, 0)
            for sh in (1, 2, 4):
                if w >= 4 * sh:
                    s = pltpu.roll(s, sh, 0) + pltpu.roll(s, nw - sh, 0)
            ws = s[gap:gap + r]
            if transpose:
                out = ws - x_ref[pl.ds(src0, r), :].astype(F32)
            else:
                out = ws * inv_cnt(seg_len, t0) - win[gap:gap + r]
            o_ref[pl.ds(src0, r), :] = out.astype(o_ref.dtype)

        for step in (fill, compute):
            step(0, gap, n_ctx, 0)

            def lat(i, c, step=step):
                off = pl.multiple_of(i * r, r)
                step(n_ctx + off, lat0 + off, n_lat, off)
                return c
            lax.fori_loop(0, n_lat // r, lat, 0)

    return _pcall(body, name=name, grid=(ncol,),
                  in_specs=[pl.BlockSpec((n, cw), lambda j: (0, col_blk0 + j))],
                  out_specs=pl.BlockSpec((n, cw), lambda j: (0, j)),
                  out_shape=jax.ShapeDtypeStruct((n, ncol * cw), out_dtype),
                  scratch_shapes=[pltpu.VMEM((nbuf, cw), F32)],
                  compiler_params=_params(("parallel",)))(x)


def _make_pool(n_ctx, n_lat, pool_width, mix_width):
    ncol = pool_width // len(POOL_WINDOWS) // LANE

    @jax.custom_vjp
    def pool(u):
        return tuple(_pool_call(u, g * ncol, ncol, w, n_ctx, n_lat, False, BF16, "pool_w%d" % w)
                     for g, w in enumerate(POOL_WINDOWS))

    def fwd(u):
        return pool(u), None

    def bwd(_, cots):
        parts = [_pool_call(ct, 0, ncol, w, n_ctx, n_lat, True, F32, "pool_w%d_bwd" % w)
                 for ct, w in zip(cots, POOL_WINDOWS)]
        parts.append(jnp.zeros((n_ctx + n_lat, mix_width - pool_width), F32))
        return (jnp.concatenate(parts, axis=1),)

    pool.defvjp(fwd, bwd)
    return pool


class _ConvGeom:
    def __init__(self, n_ctx, n_lat):
        self.n_ctx, self.n_lat = n_ctx, n_lat
        self.gap = GRID_W + SUBLANE
        self.r = 2 * GRID_W
        self.ctx0 = self.gap
        self.lat0 = 2 * self.gap + n_ctx
        self.nbuf = 3 * self.gap + n_ctx + n_lat
        self.nwin = self.r + 2 * self.gap
        self.n16 = self.r + 2 * SUBLANE
        assert n_lat % self.r == 0 and n_ctx % self.r == 0

    def zero_gaps(self, buf):
        z = jnp.zeros((self.gap, LANE), F32)
        buf[0:self.gap, :] = z
        buf[self.ctx0 + self.n_ctx:self.lat0, :] = z
        buf[self.lat0 + self.n_lat:self.nbuf, :] = z

    def fill(self, src_ref, buf):
        r = self.r

        def seg(src0, dst0, count):
            def one(i, c):
                off = pl.multiple_of(i * r, r)
                buf[pl.ds(dst0 + off, r), :] = src_ref[pl.ds(src0 + off, r), :].astype(F32)
                return c
            lax.fori_loop(0, count, one, 0)
        seg(0, self.ctx0, self.n_ctx // r)
        seg(self.n_ctx, self.lat0, self.n_lat // r)

    def col_masks(self, nrows, first_col):
        col = (lax.broadcasted_iota(jnp.int32, (nrows, 1), 0) + first_col) & (GRID_W - 1)
        return col == GRID_W - 1, col == 0

    def lat_window(self, buf, i):
        ws = pl.multiple_of(self.lat0 - self.gap + i * self.r, SUBLANE)
        return buf[pl.ds(ws, self.nwin), :]

    def ctx_window(self, buf):
        return buf[self.ctx0 - SUBLANE:self.ctx0 + self.n_ctx + SUBLANE, :]

    def lat_sources(self, win):
        last, first = self.col_masks(self.nwin, GRID_W - SUBLANE)
        return jnp.where(last, 0.0, win), win, jnp.where(first, 0.0, win)

    def row_slice(self, x, di, sign):
        st = self.gap - SUBLANE + sign * (di - 1) * GRID_W
        return x[st:st + self.n16]

    def lat_conv(self, win, w):
        srcs = self.lat_sources(win)
        cs = []
        for dj in range(3):
            acc = None
            for di in range(3):
                term = w[di * 3 + dj] * self.row_slice(srcs[dj], di, 1)
                acc = term if acc is None else acc + term
            cs.append(acc)
        out = pltpu.roll(cs[0], 1, 0) + cs[1] + pltpu.roll(cs[2], self.n16 - 1, 0)
        return out[SUBLANE:SUBLANE + self.r]

    def ctx_conv(self, win, w, transpose=False):
        n = win.shape[0]
        lo, hi = (w[5], w[3]) if transpose else (w[3], w[5])
        out = lo * pltpu.roll(win, 1, 0) + w[4] * win + hi * pltpu.roll(win, n - 1, 0)
        return out[SUBLANE:SUBLANE + self.n_ctx]

    def lat_conv_t(self, dwin, w):
        es = []
        for dj in range(3):
            acc = None
            for di in range(3):
                term = w[di * 3 + dj] * self.row_slice(dwin, di, -1)
                acc = term if acc is None else acc + term
            es.append(acc)
        last, first = self.col_masks(self.n16, GRID_W - SUBLANE)
        out = (jnp.where(last, 0.0, pltpu.roll(es[0], self.n16 - 1, 0)) + es[1]
               + jnp.where(first, 0.0, pltpu.roll(es[2], 1, 0)))
        return out[SUBLANE:SUBLANE + self.r]


def _taps(w_ref):
    return [w_ref[k:k + 1, :] for k in range(9)]


def _conv_specs(n, f_tiles):
    zv = pl.BlockSpec((n, LANE), lambda j: (0, j))
    zg = pl.BlockSpec((n, LANE), lambda j: (0, j + f_tiles))
    wv = pl.BlockSpec((9, LANE), lambda j: (0, j))
    wg = pl.BlockSpec((9, LANE), lambda j: (0, j + f_tiles))
    return zv, zg, wv, wg


def _conv_fwd_call(z, wc, n_ctx, n_lat):
    n, f2 = z.shape
    ft = f2 // 2 // LANE
    geo = _ConvGeom(n_ctx, n_lat)

    def body(zv_ref, zg_ref, wv_ref, wg_ref, o_ref, cv_ref, cg_ref, bv, bg):
        wv, wg = _taps(wv_ref), _taps(wg_ref)
        for src, buf in ((zv_ref, bv), (zg_ref, bg)):
            geo.zero_gaps(buf)
            geo.fill(src, buf)

        def emit(rows, cv, cg):
            o_ref[rows, :] = (cv * cg * _sigmoid(cg)).astype(o_ref.dtype)
            cv_ref[rows, :] = cv.astype(cv_ref.dtype)
            cg_ref[rows, :] = cg.astype(cg_ref.dtype)

        emit(slice(0, n_ctx), geo.ctx_conv(geo.ctx_window(bv), wv), geo.ctx_conv(geo.ctx_window(bg), wg))

        def chunk(i, c):
            off = pl.multiple_of(n_ctx + i * geo.r, SUBLANE)
            emit(pl.ds(off, geo.r), geo.lat_conv(geo.lat_window(bv, i), wv), geo.lat_conv(geo.lat_window(bg, i), wg))
            return c
        lax.fori_loop(0, n_lat // geo.r, chunk, 0)

    tile = pl.BlockSpec((n, LANE), lambda j: (0, j))
    return _pcall(body, name="conv_gate", grid=(ft,), in_specs=list(_conv_specs(n, ft)), out_specs=[tile] * 3,
                  out_shape=[jax.ShapeDtypeStruct((n, f2 // 2), BF16)] * 3,
                  scratch_shapes=[pltpu.VMEM((geo.nbuf, LANE), F32)] * 2,
                  compiler_params=_params(("parallel",)))(z, z, wc, wc)


def _conv_bwd_call(z, wc, cv, cg, da, n_ctx, n_lat):
    n, f2 = z.shape
    ft = f2 // 2 // LANE
    geo = _ConvGeom(n_ctx, n_lat)
    r, n16 = geo.r, geo.n16

    def body(zv_ref, zg_ref, wv_ref, wg_ref, cv_ref, cg_ref, da_ref, dzv_ref, dzg_ref, dwv_ref, dwg_ref,
             bv, bg, dv, dg):
        wv, wg = _taps(wv_ref), _taps(wg_ref)
        for buf in (bv, bg, dv, dg):
            geo.zero_gaps(buf)
        geo.fill(zv_ref, bv)
        geo.fill(zg_ref, bg)

        def gate_grads(cv, cg, d):
            sg = _sigmoid(cg)
            return d * cg * sg, d * cv * sg * (1.0 + cg * (1.0 - sg))

        def tap_sums(d_c, srcs, pad):
            zeros = jnp.zeros((SUBLANE, LANE), F32)
            dce = jnp.concatenate([zeros, d_c, zeros], axis=0)
            m = dce.shape[0]
            shifted = (pltpu.roll(dce, m - 1, 0), dce, pltpu.roll(dce, 1, 0))
            out = []
            for di in range(3):
                for dj in range(3):
                    src = srcs[dj] if pad is None else geo.row_slice(srcs[dj], di, 1)
                    out.append(jnp.sum(shifted[dj] * src, axis=0, keepdims=True))
            return out

        winv, wing = geo.ctx_window(bv), geo.ctx_window(bg)
        d_cv, d_cg = gate_grads(cv_ref[0:n_ctx, :].astype(F32), cg_ref[0:n_ctx, :].astype(F32),
                                da_ref[0:n_ctx, :].astype(F32))
        dv[geo.ctx0:geo.ctx0 + n_ctx, :] = d_cv
        dg[geo.ctx0:geo.ctx0 + n_ctx, :] = d_cg
        zero_row = jnp.zeros((1, LANE), F32)
        acc0 = []
        for d_c, win in ((d_cv, winv), (d_cg, wing)):
            sums = tap_sums(d_c, (win, win, win), None)
            acc0 += [zero_row] * 3 + sums[3:6] + [zero_row] * 3

        def chunk(i, acc):
            winv, wing = geo.lat_window(bv, i), geo.lat_window(bg, i)
            off = pl.multiple_of(n_ctx + i * r, SUBLANE)
            d_cv, d_cg = gate_grads(cv_ref[pl.ds(off, r), :].astype(F32), cg_ref[pl.ds(off, r), :].astype(F32),
                                    da_ref[pl.ds(off, r), :].astype(F32))
            dst = pl.multiple_of(geo.lat0 + i * r, SUBLANE)
            dv[pl.ds(dst, r), :] = d_cv
            dg[pl.ds(dst, r), :] = d_cg
            sums = tap_sums(d_cv, geo.lat_sources(winv), True) + tap_sums(d_cg, geo.lat_sources(wing), True)
            return tuple(a + s for a, s in zip(acc, sums))
        acc = lax.fori_loop(0, n_lat // r, chunk, tuple(acc0))
        for k in range(9):
            dwv_ref[k:k + 1, :] = acc[k]
            dwg_ref[k:k + 1, :] = acc[9 + k]

        for dbuf, w, dz_ref in ((dv, wv, dzv_ref), (dg, wg, dzg_ref)):
            dz_ref[0:n_ctx, :] = geo.ctx_conv(geo.ctx_window(dbuf), w, transpose=True).astype(dz_ref.dtype)

            def chunk_t(i, c, dbuf=dbuf, w=w, dz_ref=dz_ref):
                off = pl.multiple_of(n_ctx + i * r, SUBLANE)
                dz_ref[pl.ds(off, r), :] = geo.lat_conv_t(geo.lat_window(dbuf, i), w).astype(dz_ref.dtype)
                return c
            lax.fori_loop(0, n_lat // r, chunk_t, 0)

    tile = pl.BlockSpec((n, LANE), lambda j: (0, j))
    wtile = pl.BlockSpec((9, LANE), lambda j: (0, j))
    dzv, dzg, dwv, dwg = _pcall(
        body, name="conv_gate_bwd", grid=(ft,), in_specs=list(_conv_specs(n, ft)) + [tile] * 3,
        out_specs=[tile, tile, wtile, wtile],
        out_shape=[jax.ShapeDtypeStruct((n, f2 // 2), z.dtype)] * 2 + [jax.ShapeDtypeStruct((9, f2 // 2), F32)] * 2,
        scratch_shapes=[pltpu.VMEM((geo.nbuf, LANE), F32)] * 4,
        compiler_params=_params(("parallel",)))(z, z, wc, wc, cv, cg, da)
    return jnp.concatenate([dzv, dzg], axis=1), jnp.concatenate([dwv, dwg], axis=1)


def _make_convgate(n_ctx, n_lat):
    @jax.custom_vjp
    def conv(z, wc):
        return _conv_fwd_call(z, wc, n_ctx, n_lat)[0]

    def fwd(z, wc):
        a, cv, cg = _conv_fwd_call(z, wc, n_ctx, n_lat)
        return a, (z, wc, cv, cg)

    def bwd(res, da):
        return _conv_bwd_call(*res, da, n_ctx, n_lat)

    conv.defvjp(fwd, bwd)
    return conv


GROUPS_PER_BLOCK = LANE // SSM_GROUP
STATE_BLOCK = GROUPS_PER_BLOCK * SSM_STATE
SCAN_LANES = STATE_BLOCK


def _cmul(ar, ai, br, bi):
    return ar * br - ai * bi, ar * bi + ai * br


def _lam_tables(lr, li, asc):
    row = lax.broadcasted_iota(jnp.int32, (SUBLANE, lr.shape[1]), 0)
    l1 = (jnp.broadcast_to(lr, row.shape), jnp.broadcast_to(li, row.shape))
    l2 = _cmul(*l1, *l1)
    l4 = _cmul(*l2, *l2)
    pw = l1
    pr = jnp.zeros(row.shape, F32)
    pi = jnp.zeros(row.shape, F32)
    for e in range(1, SUBLANE + 1):
        s = e - 1 if asc else SUBLANE - e
        pr = jnp.where(row == s, pw[0], pr)
        pi = jnp.where(row == s, pw[1], pi)
        pw = _cmul(*pw, *l1)

    def masked(lam_k, k):
        keep = (row >= k) if asc else (row < SUBLANE - k)
        return jnp.where(keep, lam_k[0], 0.0), jnp.where(keep, lam_k[1], 0.0)

    return masked(l1, 1), masked(l2, 2), masked(l4, 4), (pr, pi)


def _tile_scan(br, bi, cr, ci, tables, asc):
    hr, hi = br, bi
    for k, lam_k in zip((1, 2, 4), tables[:3]):
        shift = k if asc else SUBLANE - k
        mr, mi = _cmul(*lam_k, pltpu.roll(hr, shift, 0), pltpu.roll(hi, shift, 0))
        hr, hi = hr + mr, hi + mi
    mr, mi = _cmul(*tables[3], jnp.broadcast_to(cr, br.shape), jnp.broadcast_to(ci, br.shape))
    hr, hi = hr + mr, hi + mi
    last = SUBLANE - 1 if asc else 0
    return hr, hi, hr[last:last + 1, :], hi[last:last + 1, :]


def _chunk_in_time_order(k, n_chunks, asc, adjoint):
    if asc:
        return n_chunks - 1 - k if adjoint else k
    if adjoint:
        return jnp.where(k == n_chunks - 1, 0, k + 1)
    return jnp.where(k == 0, 0, n_chunks - k)


def _dot(a, b):
    return jnp.dot(a, b, preferred_element_type=F32)


def _dot_nt(a, b):
    return lax.dot_general(a, b, (((1,), (1,)), ((), ())), preferred_element_type=F32)


def _scan_chunk(r_buf, i_buf, base, rows, carry, tables, asc, lam_grad=None):
    tiles = rows // SUBLANE
    out_carry, grads = [], []
    for h in range(STATE_BLOCK // SCAN_LANES):
        cols = slice(h * SCAN_LANES, (h + 1) * SCAN_LANES)

        def tile(kt, c, h=h, cols=cols):
            pt = kt if asc else tiles - 1 - kt
            t0 = pl.multiple_of(base + pt * SUBLANE, SUBLANE)
            sr, si, ncr, nci = _tile_scan(r_buf[pl.ds(t0, SUBLANE), cols], i_buf[pl.ds(t0, SUBLANE), cols],
                                          c[0], c[1], tables[h], asc)
            r_buf[pl.ds(t0, SUBLANE), cols] = sr
            i_buf[pl.ds(t0, SUBLANE), cols] = si
            if lam_grad is None:
                return ncr, nci
            h_r, h_i, h_base = lam_grad
            g0 = pl.multiple_of(h_base + pt * SUBLANE, SUBLANE)
            pr, pi = h_r[pl.ds(g0, SUBLANE), cols], h_i[pl.ds(g0, SUBLANE), cols]
            row = lax.broadcasted_iota(jnp.int32, sr.shape, 0)
            if asc:
                nr = jnp.where(row == 0, jnp.broadcast_to(c[0], sr.shape), pltpu.roll(sr, 1, 0))
                ni = jnp.where(row == 0, jnp.broadcast_to(c[1], sr.shape), pltpu.roll(si, 1, 0))
            else:
                nr = jnp.where(row == SUBLANE - 1, jnp.broadcast_to(c[0], sr.shape), pltpu.roll(sr, SUBLANE - 1, 0))
                ni = jnp.where(row == SUBLANE - 1, jnp.broadcast_to(c[1], sr.shape), pltpu.roll(si, SUBLANE - 1, 0))
            return ncr, nci, c[2] + nr * pr + ni * pi, c[3] + ni * pr - nr * pi

        init = (carry[2 * h], carry[2 * h + 1])
        if lam_grad is not None:
            zero = jnp.zeros((SUBLANE, SCAN_LANES), F32)
            init = init + (zero, zero)
        res = lax.fori_loop(0, tiles, tile, init, unroll=2)
        out_carry += [res[0], res[1]]
        grads.append(res[2:])
    return tuple(out_carry), grads


def _ssm_specs(n):
    tok = pl.BlockSpec((n, LANE), lambda q: (0, q))
    lam = pl.BlockSpec((1, STATE_BLOCK), lambda q: (0, q))
    w_in = pl.BlockSpec((None, LANE, STATE_BLOCK), lambda q: (q, 0, 0))
    w_out = pl.BlockSpec((None, STATE_BLOCK, LANE), lambda q: (q, 0, 0))
    return tok, lam, w_in, w_out


def _zero_carry():
    return tuple(jnp.zeros((1, SCAN_LANES), F32) for _ in range(2 * (STATE_BLOCK // SCAN_LANES)))


def _half_tables(lr_ref, li_ref, asc, conj):
    out = []
    for h in range(STATE_BLOCK // SCAN_LANES):
        cols = slice(h * SCAN_LANES, (h + 1) * SCAN_LANES)
        li = li_ref[:, cols]
        out.append(_lam_tables(lr_ref[:, cols], -li if conj else li, asc))
    return out


def _ssm_fwd_call(u, lr, li, b_re, b_im, c_re, c_imn, rc, asc, side=None):
    n, cs = u.shape
    nq, nchunks = cs // LANE, n // rc
    w_states = lr.shape[1]
    tok, lam, w_in, w_out = _ssm_specs(n)
    states = pl.BlockSpec((n, STATE_BLOCK), lambda q: (0, q))

    def body(u_ref, lr_ref, li_ref, br_ref, bi_ref, cr_ref, ci_ref, y_ref, sr_ref, si_ref, h_r, h_i):
        tables = _half_tables(lr_ref, li_ref, asc, False)
        wbr, wbi = br_ref[...].astype(BF16), bi_ref[...].astype(BF16)
        wcr, wci = cr_ref[...].astype(BF16), ci_ref[...].astype(BF16)

        def chunk(k, carry):
            r0 = pl.multiple_of(_chunk_in_time_order(k, nchunks, asc, False) * rc, rc)
            ub = u_ref[pl.ds(r0, rc), :].astype(BF16)
            h_r[...] = _dot(ub, wbr)
            h_i[...] = _dot(ub, wbi)
            carry, _ = _scan_chunk(h_r, h_i, 0, rc, carry, tables, asc)
            hb_r, hb_i = h_r[...].astype(BF16), h_i[...].astype(BF16)
            sr_ref[pl.ds(r0, rc), :] = hb_r
            si_ref[pl.ds(r0, rc), :] = hb_i
            y_ref[pl.ds(r0, rc), :] = _dot(hb_r, wcr) + _dot(hb_i, wci)
            return carry
        lax.fori_loop(0, nchunks, chunk, _zero_carry())

    return _call_with_side(body, name="s5_head", grid=(nq,), in_specs=[tok, lam, lam, w_in, w_in, w_out, w_out],
                           out_specs=[tok, states, states],
                           out_shape=[jax.ShapeDtypeStruct((n, cs), F32)] + [jax.ShapeDtypeStruct((n, w_states), BF16)] * 2,
                           scratch_shapes=[pltpu.VMEM((rc, STATE_BLOCK), F32)] * 2,
                           args=(u, lr, li, b_re, b_im, c_re, c_imn), side=side, semantics=("parallel",))


def _ssm_bwd_call(u, dy, s_re, s_im, lr, li, b_re, b_im, c_re, c_imn, rc, asc, side=None):
    n, cs = u.shape
    nq, nchunks = cs // LANE, n // rc
    tok, lam, w_in, w_out = _ssm_specs(n)
    states = pl.BlockSpec((n, STATE_BLOCK), lambda q: (0, q))

    def body(u_ref, dy_ref, sr_ref, si_ref, lr_ref, li_ref, br_ref, bi_ref, cr_ref, ci_ref,
             du_ref, glr_ref, gli_ref, dbr_ref, dbi_ref, dcr_ref, dci_ref, h_r, h_i, a_r, a_i):
        wbr, wbi = br_ref[...].astype(BF16), bi_ref[...].astype(BF16)
        wcr, wci = cr_ref[...].astype(BF16), ci_ref[...].astype(BF16)

        adj = _half_tables(lr_ref, li_ref, not asc, True)
        for ref in (glr_ref, gli_ref, dbr_ref, dbi_ref, dcr_ref, dci_ref):
            ref[...] = jnp.zeros_like(ref)

        def chunk_adj(k, carry):
            r0 = pl.multiple_of(_chunk_in_time_order(k, nchunks, asc, True) * rc, rc)
            dyc = dy_ref[pl.ds(r0, rc), :]
            dyb = dyc.astype(BF16)
            a_r[...] = _dot_nt(dyb, wcr)
            a_i[...] = _dot_nt(dyb, wci)
            sb_r, sb_i = sr_ref[pl.ds(r0, rc), :], si_ref[pl.ds(r0, rc), :]
            h_r[...] = sb_r.astype(F32)
            h_i[...] = sb_i.astype(F32)
            carry, grads = _scan_chunk(a_r, a_i, 0, rc, carry, adj, not asc, lam_grad=(h_r, h_i, 0))
            for h, (gr, gi) in enumerate(grads):
                cols = slice(h * SCAN_LANES, (h + 1) * SCAN_LANES)
                glr_ref[:, cols] += gr
                gli_ref[:, cols] += gi
            ab_r, ab_i = a_r[...].astype(BF16), a_i[...].astype(BF16)
            du_ref[pl.ds(r0, rc), :] = _dot_nt(ab_r, wbr) + _dot_nt(ab_i, wbi)
            ut = u_ref[pl.ds(r0, rc), :].T.astype(BF16)
            dbr_ref[...] += _dot(ut, ab_r)
            dbi_ref[...] += _dot(ut, ab_i)
            dyt = dyc.T.astype(BF16)
            dcr_ref[...] += _dot(dyt, sb_r)
            dci_ref[...] += _dot(dyt, sb_i)
            return carry
        lax.fori_loop(0, nchunks, chunk_adj, _zero_carry())

    part = pl.BlockSpec((SUBLANE, STATE_BLOCK), lambda q: (0, q))
    w_states = lr.shape[1]
    return _call_with_side(
        body, name="s5_head_bwd", grid=(nq,),
        in_specs=[tok, tok, states, states, lam, lam, w_in, w_in, w_out, w_out],
        out_specs=[tok, part, part, w_in, w_in, w_in, w_in],
        out_shape=[jax.ShapeDtypeStruct((n, cs), F32)] + [jax.ShapeDtypeStruct((SUBLANE, w_states), F32)] * 2
        + [jax.ShapeDtypeStruct((nq, LANE, STATE_BLOCK), F32)] * 4,
        scratch_shapes=[pltpu.VMEM((rc, STATE_BLOCK), F32)] * 4,
        args=(u, dy, s_re, s_im, lr, li, b_re, b_im, c_re, c_imn), side=side, semantics=("parallel",))


def _ssm_cotangents(outs):
    du, glr, gli, dbr, dbi, dcr, dci = outs
    return (du, jnp.sum(glr, axis=0, keepdims=True), jnp.sum(gli, axis=0, keepdims=True), dbr, dbi,
            jnp.swapaxes(dcr, 1, 2), jnp.swapaxes(dci, 1, 2))


def _make_ssm(rc, asc):
    @jax.custom_vjp
    def ssm(u, lr, li, b_re, b_im, c_re, c_imn):
        return _ssm_fwd_call(u, lr, li, b_re, b_im, c_re, c_imn, rc, asc)[0]

    def fwd(*args):
        y, s_re, s_im = _ssm_fwd_call(*args, rc, asc)
        return y, (args, s_re, s_im)

    def bwd(res, dy):
        args, s_re, s_im = res
        return _ssm_cotangents(_ssm_bwd_call(args[0], dy, s_re, s_im, *args[1:], rc, asc))

    ssm.defvjp(fwd, bwd)
    return ssm


def _make_hosting_ssm(rc, asc, name):
    def run(u, lr, li, b_re, b_im, c_re, c_imn, shard):
        sh = shard.astype(BF16)
        side = (_gather_parts, sh, jax.ShapeDtypeStruct((N_DEV,) + sh.shape, BF16))
        return _ssm_fwd_call(u, lr, li, b_re, b_im, c_re, c_imn, rc, asc, side=side)

    @jax.custom_vjp
    def ssm(*args):
        y, _, _, whole = run(*args)
        return y, whole

    def fwd(*args):
        y, s_re, s_im, whole = run(*args)
        return (y, whole), (args[:-1], s_re, s_im)

    def bwd(res, cts):
        args, s_re, s_im = res
        dy, d_whole = cts
        side = (_scatter_parts, d_whole, jax.ShapeDtypeStruct(d_whole.shape, d_whole.dtype))
        outs = _ssm_bwd_call(args[0], dy, s_re, s_im, *args[1:], rc, asc, side=side)
        return _ssm_cotangents(outs[:-1]) + (_sum8_call(outs[-1], name + "_sum"),)

    ssm.defvjp(fwd, bwd)
    return ssm


def _blocks_in(bb):
    g, p, h = bb.shape
    k = GROUPS_PER_BLOCK
    out = jnp.einsum('qgph,gk->qghkp', bb.reshape(g // k, k, p, h), jnp.eye(k, dtype=F32))
    return out.reshape(g // k, k * h, k * p)


def _blocks_out(cc):
    g, h, p = cc.shape
    k = GROUPS_PER_BLOCK
    out = jnp.einsum('qghp,gk->qgpkh', cc.reshape(g // k, k, h, p), jnp.eye(k, dtype=F32))
    return out.reshape(g // k, k * p, k * h)


def _position():
    return lax.axis_index("x"), lax.axis_index("y"), lax.axis_index("c")


def _linear_index():
    x, y, c = _position()
    return 4 * x + 2 * y + c


COMM_SCRATCH = [pltpu.SemaphoreType.DMA((7,)), pltpu.SemaphoreType.DMA((7,)), pltpu.SemaphoreType.DMA(())]
ANY_SPEC = pl.BlockSpec(memory_space=pl.ANY)


def _gather_parts(x_ref, o_ref, send_sems, recv_sems, local_sem):
    x, y, c = _position()
    me, sibling = (x, y, c), (x, y, 1 - c)
    chips = [(1 - x, y), (x, 1 - y), (1 - x, 1 - y)]

    def block(px, py, pc):
        return o_ref.at[4 * px + 2 * py + pc]

    def copy(k, blk, to, src=None):
        return pltpu.make_async_remote_copy(
            src_ref=block(*blk) if src is None else src, dst_ref=block(*blk),
            send_sem=send_sems.at[k], recv_sem=recv_sems.at[k], device_id=to, device_id_type=MESH)

    mine = pltpu.make_async_copy(x_ref, block(*me), local_sem)
    first = [copy(0, me, sibling, src=x_ref)]
    first += [copy(1 + j, me, (*chip, c), src=x_ref) for j, chip in enumerate(chips)]
    passed = [copy(4 + j, (*chip, c), sibling) for j, chip in enumerate(chips)]

    def start():
        mine.start()
        for cp in first:
            cp.start()

    def finish():
        for j, chip in enumerate(chips):
            copy(1 + j, (*chip, c), me).wait_recv()
            passed[j].start()
        copy(0, sibling, me).wait_recv()
        for j, chip in enumerate(chips):
            copy(4 + j, (*chip, 1 - c), me).wait_recv()
        for cp in first + passed:
            cp.wait_send()
        mine.wait()

    return start, finish


def _scatter_parts(g_ref, o_ref, send_sems, recv_sems, local_sem, rows=None):
    x, y, c = _position()

    def block(p):
        return g_ref.at[p] if rows is None else g_ref.at[p, pl.ds(rows[0], rows[1])]

    mine = pltpu.make_async_copy(block(4 * x + 2 * y + c), o_ref.at[0], local_sem)
    copies = []
    for k in range(1, N_DEV):
        px = 1 - x if k & 4 else x
        py = 1 - y if k & 2 else y
        pc = 1 - c if k & 1 else c
        copies.append(pltpu.make_async_remote_copy(
            src_ref=block(4 * px + 2 * py + pc), dst_ref=o_ref.at[k],
            send_sem=send_sems.at[k - 1], recv_sem=recv_sems.at[k - 1],
            device_id=(px, py, pc), device_id_type=MESH))

    def start():
        mine.start()
        for cp in copies:
            cp.start()

    def finish():
        for cp in copies:
            cp.wait()
        mine.wait()

    return start, finish


def _comm_call(parts, src, out_shape, name):
    def body(s_ref, o_ref, *sems):
        start, finish = parts(s_ref, o_ref, *sems)
        start()
        finish()

    return _pcall(body, name=name, in_specs=[ANY_SPEC], out_specs=ANY_SPEC, out_shape=out_shape,
                  scratch_shapes=COMM_SCRATCH)(src)


def _ag_call(shard, name):
    return _comm_call(_gather_parts, shard, jax.ShapeDtypeStruct((N_DEV,) + shard.shape, shard.dtype), name)


def _rs_call(g, name):
    return _comm_call(_scatter_parts, g, jax.ShapeDtypeStruct(g.shape, g.dtype), name)


def _sum8_call(parts, name):
    _, r, c = parts.shape
    tr = _tile(r, max(PACK, (4 << 20) // (N_DEV * c * parts.dtype.itemsize) // PACK * PACK), PACK)

    def body(p_ref, o_ref):
        acc = p_ref[0].astype(F32)
        for k in range(1, N_DEV):
            acc = acc + p_ref[k].astype(F32)
        o_ref[...] = acc

    return _pcall(body, name=name, grid=(r // tr,),
                  in_specs=[pl.BlockSpec((N_DEV, tr, c), lambda i: (0, i, 0))],
                  out_specs=pl.BlockSpec((tr, c), lambda i: (i, 0)),
                  out_shape=jax.ShapeDtypeStruct((r, c), F32), compiler_params=_params(("parallel",)))(parts)


def _reduce_scatter(g, name):
    return _sum8_call(_rs_call(g, name), name + "_sum")


def _make_gather(dtype, name):
    @jax.custom_vjp
    def gather(shard):
        return _ag_call(shard.astype(dtype), name)

    def fwd(shard):
        return gather(shard), None

    def bwd(_, ct):
        return (_reduce_scatter(ct, name + "_rs"),)

    gather.defvjp(fwd, bwd)
    return gather


def _adam_call(w, g, m, v, name):
    r, c = w.shape
    tr = _tile(r, max(SUBLANE, (1 << 20) // (4 * c) // SUBLANE * SUBLANE), SUBLANE)

    def body(w_ref, g_ref, m_ref, v_ref, d_ref, mo_ref, vo_ref):
        gv = g_ref[...]
        m2 = ADAM_B1 * m_ref[...] + (1.0 - ADAM_B1) * gv
        v2 = ADAM_B2 * v_ref[...] + (1.0 - ADAM_B2) * (gv * gv)
        m_hat = m2 / (1.0 - ADAM_B1 ** ADAM_STEP)
        v_hat = v2 / (1.0 - ADAM_B2 ** ADAM_STEP)
        d_ref[...] = -ADAM_LR * (m_hat / (jnp.sqrt(v_hat) + ADAM_EPS) + ADAM_WD * w_ref[...])
        mo_ref[...] = m2
        vo_ref[...] = v2

    spec = pl.BlockSpec((tr, c), lambda i: (i, 0))
    return _pcall(body, name=name, grid=(r // tr,), in_specs=[spec] * 4, out_specs=[spec] * 3,
                  out_shape=[jax.ShapeDtypeStruct((r, c), F32)] * 3, compiler_params=_params(("parallel",)))(w, g, m, v)


def _loss_call(y, target, tr):
    n, d = y.shape

    def body(y_ref, t_ref, s_ref, dy_ref):
        i = pl.program_id(0)
        err = y_ref[...] - t_ref[...]
        dy_ref[...] = err * (1.0 / d)
        part = jnp.sum(jnp.sum(err * err, axis=1, keepdims=True), axis=0, keepdims=True)

        @pl.when(i == 0)
        def _():
            s_ref[...] = jnp.zeros_like(s_ref)
        s_ref[...] += part

    spec = pl.BlockSpec((tr, d), lambda i: (i, 0))
    return _pcall(body, name="loss_head", grid=(n // tr,), in_specs=[spec, spec],
                  out_specs=[pl.BlockSpec((1, 1), lambda i: (0, 0)), spec],
                  out_shape=[jax.ShapeDtypeStruct((1, 1), F32), jax.ShapeDtypeStruct((n, d), F32)],
                  compiler_params=_params(("arbitrary",)))(y, target)


def _pack(arrays, rows_mult):
    flat = jnp.concatenate([a.reshape(-1).astype(F32) for a in arrays])
    rows = -(-flat.shape[0] // LANE)
    rows = -(-rows // rows_mult) * rows_mult
    return jnp.pad(flat, (0, rows * LANE - flat.shape[0])).reshape(rows, LANE)


def _unpack(buf, shapes, lead=()):
    flat = buf.reshape(lead + (-1,))
    out, pos = [], 0
    for s in shapes:
        size = math.prod(s)
        out.append(flat[..., pos:pos + size].reshape(lead + tuple(s)))
        pos += size
    return out


def _s5_discretise(a_re, a_im, log_dt, b_re, b_im):
    dt = jnp.exp(log_dt)[:, None]
    mag = jnp.exp(a_re * dt)
    lam_re = mag * jnp.cos(a_im * dt)
    lam_im = mag * jnp.sin(a_im * dt)
    denom = a_re * a_re + a_im * a_im
    nr, ni = lam_re - 1.0, lam_im
    f_re = (nr * a_re + ni * a_im) / denom
    f_im = (ni * a_re - nr * a_im) / denom
    bb_re = f_re[..., None] * b_re - f_im[..., None] * b_im
    bb_im = f_re[..., None] * b_im + f_im[..., None] * b_re
    return lam_re, lam_im, bb_re, bb_im


def _forward(x, p, ctx, s_c, n_ctx, n_lat):
    d_model = x.shape[1]
    depth = len(p['w_in'])
    mix = p['w_in'][0].shape[1]
    pool_width = p['pool_scale'].shape[1]
    tr = n_ctx
    me = _linear_index()

    rmsmod = _make_rowop(_fn_rmsmod, 1, 2, 1, [BF16], tr, "rms_modulate", transposed=True)
    resid = _make_rowop(_fn_resid, 2, 1, 1, [F32], tr, "gated_residual")
    gelu_head = _make_rowop(_fn_gelu_head, 2, 0, 1, [F32], tr, "ssm_gelu")
    mixcat = _make_rowop(_fn_mixcat, 6, 0, 1, [BF16], tr, "mix_concat", transposed=True)
    pool = _make_pool(n_ctx, n_lat, pool_width, mix)
    convgate = _make_convgate(n_ctx, n_lat)
    ssm = [_make_ssm(tr, True), _make_ssm(tr, False)]
    lin_f32 = {k: _make_linear(F32, k) for k in ("w_in", "w_pool", "w_glu", "w_out", "w_down", "ada")}
    lin_up = _make_linear(BF16, "w_up")
    gather_big = {k: _make_gather(BF16, "gather_" + k) for k in BIG_SHARDED}
    gather_small = _make_gather(BF16, "gather_small")
    gather_mod = _make_gather(F32, "gather_mod")

    whole = {k: gather_big[k](p[k][0]) for k in ('w_in', 'w_out')}
    host = {k: _make_hosting_linear(BF16 if k == 'w_up' else F32, k) for k in BIG_SHARDED}
    ssm_host = [_make_hosting_ssm(tr, True, "s5_w_up"), _make_hosting_ssm(tr, False, "s5_w_down")]

    def big_linear(k, a, l, a_t=None):
        w = whole[k] if k == 'w_up' else whole[k].reshape(-1, whole[k].shape[2])
        if l + 1 == depth:
            return (lin_up if k == 'w_up' else lin_f32[k])(a, a_t, w)
        y, whole[k] = host[k](a, a_t, w, p[k][l + 1])
        return y

    small = [p[k] for k in SMALL_SHARDED]
    packed = gather_small(_pack(small, PACK))
    w_pool, w_glu, w_conv = _unpack(packed, [a.shape for a in small], lead=(N_DEV,))
    w_pool = jnp.moveaxis(w_pool, 0, 2).reshape(depth, len(POOL_WINDOWS), -1, w_pool.shape[-1])
    w_glu = jnp.moveaxis(w_glu, 0, 1).reshape(depth, -1, w_glu.shape[-1])
    w_conv = jnp.moveaxis(w_conv, 0, 3).reshape(depth, 9, -1).astype(F32)

    s_rows = jnp.concatenate([s_c, jax.nn.silu(p['c_ctx'])[None, :],
                              jnp.zeros((PACK - N_DEV - 1, d_model), F32)], axis=0)
    cols = p['w_ada'][0].shape[1]
    b_loc = lax.dynamic_slice_in_dim(p['b_ada'], me * cols, cols, axis=1)
    mod_loc = jnp.stack([lin_f32['ada'](s_rows, None, p['w_ada'][l]) + b_loc[l][None, :] for l in range(depth)])
    mod = gather_mod(mod_loc.reshape(depth * PACK, cols)).reshape(N_DEV, depth, PACK, cols)
    mod = jnp.moveaxis(mod, 0, 2).reshape(depth, PACK, 6, d_model)
    mod_lat = lax.dynamic_index_in_dim(mod, me, axis=1, keepdims=False)
    mod_ctx = mod[:, N_DEV]

    xs = jnp.concatenate([ctx, x], axis=0)
    for l in range(depth):
        def seg(k):
            return jnp.stack([mod_ctx[l, k], mod_lat[l, k]]).reshape(2, 1, d_model)

        def row(name):
            return p[name][l].reshape(1, -1)

        h1, h1_t = rmsmod(xs, seg(0), seg(1), row('g_pre_mix'))
        u = big_linear('w_in', h1, l, h1_t)
        pooled = pool(u)
        yp = [lin_f32['w_pool'](pooled[g], None, w_pool[l, g]) for g in range(len(POOL_WINDOWS))]
        u_ssm = u[:, pool_width:]
        ys = []
        for d in range(2):
            lam_re, lam_im, bb_re, bb_im = _s5_discretise(
                p['ssm_a_re'][l, d], p['ssm_a_im'][l, d], p['ssm_log_dt'][l, d], p['ssm_b_re'][l, d], p['ssm_b_im'][l, d])
            args = (u_ssm, lam_re.reshape(1, -1), lam_im.reshape(1, -1), _blocks_in(bb_re), _blocks_in(bb_im),
                    _blocks_out(p['ssm_c_re'][l, d]), _blocks_out(-p['ssm_c_im'][l, d]))
            if l == 0:
                k = ('w_up', 'w_down')[d]
                y_dir, whole[k] = ssm_host[d](*args, p[k][0])
                ys.append(y_dir)
            else:
                ys.append(ssm[d](*args))
        gl, = gelu_head(ys[0] + ys[1], u_ssm, row('ssm_d'))
        s = lin_f32['w_glu'](gl, None, w_glu[l])
        cat, cat_t = mixcat(*yp, gl, s, row('pool_scale'))
        mixed = big_linear('w_out', cat, l, cat_t)
        x1, = resid(xs, mixed, seg(2), row('g_post_mix'))
        h2, h2_t = rmsmod(x1, seg(3), seg(4), row('g_pre_ffn'))
        z = big_linear('w_up', h2, l, h2_t)
        a = convgate(z, w_conv[l])
        f = big_linear('w_down', a, l)
        xs, = resid(x1, f, seg(5), row('g_post_ffn'))
    return xs[n_ctx:]


def _as_rows(a):
    return a.reshape(-1, a.shape[-1])


def kernel(x, c, ctx, c_ctx, w_ada, b_ada, w_in, w_pool, pool_scale, ssm_a_re, ssm_a_im, ssm_log_dt, ssm_b_re, ssm_b_im, ssm_c_re, ssm_c_im, ssm_d, w_glu, w_out, g_pre_mix, g_post_mix, g_pre_ffn, g_post_ffn, w_up, w_conv, w_down, loss_target, m_c_ctx, m_w_ada, m_b_ada, m_w_in, m_w_pool, m_pool_scale, m_ssm_a_re, m_ssm_a_im, m_ssm_log_dt, m_ssm_b_re, m_ssm_b_im, m_ssm_c_re, m_ssm_c_im, m_ssm_d, m_w_glu, m_w_out, m_g_pre_mix, m_g_post_mix, m_g_pre_ffn, m_g_post_ffn, m_w_up, m_w_conv, m_w_down, v_c_ctx, v_w_ada, v_b_ada, v_w_in, v_w_pool, v_pool_scale, v_ssm_a_re, v_ssm_a_im, v_ssm_log_dt, v_ssm_b_re, v_ssm_b_im, v_ssm_c_re, v_ssm_c_im, v_ssm_d, v_w_glu, v_w_out, v_g_pre_mix, v_g_post_mix, v_g_pre_ffn, v_g_post_ffn, v_w_up, v_w_conv, v_w_down):
    weights = dict(zip(WEIGHTS, (c_ctx, w_ada, b_ada, w_in, w_pool, pool_scale, ssm_a_re, ssm_a_im, ssm_log_dt, ssm_b_re,
                                 ssm_b_im, ssm_c_re, ssm_c_im, ssm_d, w_glu, w_out, g_pre_mix, g_post_mix, g_pre_ffn,
                                 g_post_ffn, w_up, w_conv, w_down)))
    m_in = dict(zip(WEIGHTS, (m_c_ctx, m_w_ada, m_b_ada, m_w_in, m_w_pool, m_pool_scale, m_ssm_a_re, m_ssm_a_im,
                              m_ssm_log_dt, m_ssm_b_re, m_ssm_b_im, m_ssm_c_re, m_ssm_c_im, m_ssm_d, m_w_glu, m_w_out,
                              m_g_pre_mix, m_g_post_mix, m_g_pre_ffn, m_g_post_ffn, m_w_up, m_w_conv, m_w_down)))
    v_in = dict(zip(WEIGHTS, (v_c_ctx, v_w_ada, v_b_ada, v_w_in, v_w_pool, v_pool_scale, v_ssm_a_re, v_ssm_a_im,
                              v_ssm_log_dt, v_ssm_b_re, v_ssm_b_im, v_ssm_c_re, v_ssm_c_im, v_ssm_d, v_w_glu, v_w_out,
                              v_g_pre_mix, v_g_post_mix, v_g_pre_ffn, v_g_post_ffn, v_w_up, v_w_conv, v_w_down)))
    depth = w_in.shape[0]
    n_lat, d_model = x.shape[1], x.shape[2]
    n_ctx = ctx.shape[1]
    per_layer = BIG_SHARDED + ['w_ada']

    c_rows = jnp.concatenate([c, jnp.zeros((SUBLANE - 1, d_model), F32)], axis=0)
    s_c = jax.nn.silu(_ag_call(c_rows, "gather_c")[:, 0, :])

    params = {k: ([w[l] for l in range(depth)] if k in per_layer else w) for k, w in weights.items()}

    def run(x2d, prm):
        return _forward(x2d, prm, ctx[0], s_c, n_ctx, n_lat)

    y, vjp = jax.vjp(run, x[0], params)
    sq, dy = _loss_call(y, loss_target[0], n_ctx)
    loss = lax.psum(0.5 * sq[0, 0] / d_model, ("x", "y", "c"))
    gx, grads = vjp(dy)
    grads = {k: (jnp.stack(g) if k in per_layer else g) for k, g in grads.items()}

    rep_shapes = [weights[k].shape for k in REPLICATED]
    contrib = _pack([grads[k] for k in REPLICATED], N_DEV * PACK)
    rows = contrib.shape[0] // N_DEV
    total = _ag_call(_reduce_scatter(contrib.reshape(N_DEV, rows, LANE), "reduce_replicated"), "gather_replicated")
    total = total.reshape(N_DEV * rows, LANE)
    for k, g in zip(REPLICATED, _unpack(total, rep_shapes)):
        grads[k] = g

    delta, new_m, new_v = {}, {}, {}
    rep = [_pack([src[k] for k in REPLICATED], N_DEV * PACK) for src in (weights, m_in, v_in)]
    upd = _adam_call(rep[0], total, rep[1], rep[2], "adamw_replicated")
    for out, buf in zip((delta, new_m, new_v), upd):
        out.update(zip(REPLICATED, _unpack(buf, rep_shapes)))
    for k in WEIGHTS:
        if k in REPLICATED:
            continue
        upd = _adam_call(_as_rows(weights[k]), _as_rows(grads[k]), _as_rows(m_in[k]), _as_rows(v_in[k]), "adamw_" + k)
        for out, buf in zip((delta, new_m, new_v), upd):
            out[k] = buf.reshape(weights[k].shape)

    return (loss, gx[None], *[grads[k] for k in WEIGHTS], *[delta[k] for k in WEIGHTS],
            *[new_m[k] for k in WEIGHTS], *[new_v[k] for k in WEIGHTS])
```
